```python
import jax, jax.numpy as jnp
from jax import lax
import numpy as np

D_MODEL = 1024
BATCH = 32
SEQ = 2048
DEPTH = 1
DEC_BATCH = 8
DEC_SEQ = 32
PAST_LEN = 2048

CHUNK = 64
EPS = 1e-6
D_POOL = D_MODEL // 2
POOL_WINDOWS = (2, 4, 8, 16)
POOL_GROUPS = len(POOL_WINDOWS)
POOL_GW = D_POOL // POOL_GROUPS
POOL_OUT_GW = D_MODEL // POOL_GROUPS
POOL_PAD = max(POOL_WINDOWS) - 1
D_CONV = D_MODEL // 2
CONV_WIDTH = 31
CONV_PAD = CONV_WIDTH - 1
N_EXPERTS = 32
TOP_K = 4
D_FF = D_MODEL
SWIGLU_LIMIT = 7.0
SWIGLU_ALPHA = 1.702
D_IN = D_POOL + 2 * D_CONV

kernel_name = "hybrid_pool_conformer_moe_stream_step"


def _rms_norm(x, g):
    xf = x.astype(jnp.float32)
    y = xf * lax.rsqrt(jnp.mean(xf * xf, axis=-1, keepdims=True) + EPS)
    return (y * g.astype(jnp.float32)).astype(x.dtype)


def _pool_mix(u, buf, pos0, pool_w, pool_scale):
    B, L, _ = u.shape
    full = jnp.concatenate([buf, u], axis=1)
    cs = jnp.cumsum(full.astype(jnp.float32), axis=1)
    cs = jnp.pad(cs, ((0, 0), (1, 0), (0, 0)))
    pos = pos0 + jnp.arange(L)
    outs = []
    for gi, w in enumerate(POOL_WINDOWS):
        seg = cs[:, :, gi * POOL_GW:(gi + 1) * POOL_GW]
        s = seg[:, POOL_PAD + 1:] - seg[:, POOL_PAD + 1 - w:POOL_PAD + 1 - w + L]
        cnt = jnp.minimum(pos + 1, w).astype(jnp.float32)
        mean = s / cnt[None, :, None]
        outs.append(mean - u[:, :, gi * POOL_GW:(gi + 1) * POOL_GW].astype(jnp.float32))
    pooled = jnp.stack(outs, axis=2).astype(u.dtype)
    y = jnp.einsum('blgc,gco->blgo', pooled, pool_w).reshape(B, L, D_MODEL)
    return y * pool_scale, full[:, -POOL_PAD:]


def _conv_mix(v, buf, conv_dw, conv_b, ln_g, ln_b, w_out_conv):
    a, g = v[..., :D_CONV], v[..., D_CONV:]
    u = a * jax.nn.sigmoid(g)
    full = jnp.concatenate([buf, u], axis=1)
    y = lax.conv_general_dilated(full, conv_dw[:, None, :], window_strides=(1,), padding='VALID',
                                 dimension_numbers=('NWC', 'WIO', 'NWC'),
                                 feature_group_count=D_CONV) + conv_b
    yf = y.astype(jnp.float32)
    mu = jnp.mean(yf, axis=-1, keepdims=True)
    var = jnp.mean(jnp.square(yf - mu), axis=-1, keepdims=True)
    yn = ((yf - mu) * lax.rsqrt(var + EPS) * ln_g + ln_b).astype(v.dtype)
    return jax.nn.silu(yn) @ w_out_conv, full[:, -CONV_PAD:]


def _moe(h, router_w, router_b, exp_w_gate, exp_b_gate, exp_w_up, exp_b_up, exp_w_down, exp_b_down):
    B, L, _ = h.shape
    t = h.reshape(B * L, D_MODEL)
    logits = (t @ router_w + router_b).astype(jnp.float32)
    top_vals, top_idx = lax.top_k(logits, TOP_K)
    top_w = jax.nn.softmax(top_vals, axis=-1)
    combine = jnp.sum(jax.nn.one_hot(top_idx, N_EXPERTS, dtype=jnp.float32) * top_w[..., None], axis=1)
    out = jnp.zeros((B * L, D_MODEL), jnp.float32)
    for e in range(N_EXPERTS):
        gt = jnp.minimum(t @ exp_w_gate[e] + exp_b_gate[e], SWIGLU_LIMIT)
        up = jnp.clip(t @ exp_w_up[e] + exp_b_up[e], -SWIGLU_LIMIT, SWIGLU_LIMIT)
        act = gt * jax.nn.sigmoid(SWIGLU_ALPHA * gt) * (up + 1.0)
        out = out + combine[:, e:e + 1] * (act @ exp_w_down[e] + exp_b_down[e]).astype(jnp.float32)
    return out.astype(h.dtype).reshape(B, L, D_MODEL)


def _layer(x, c, pool_buf, conv_buf, pos0, norm1_g, norm2_g, w_ada, b_ada, w_in, pool_w, pool_scale,
           conv_dw, conv_b, conv_ln_g, conv_ln_b, conv_w_out, gate_w, gate_b, w_out,
           router_w, router_b, exp_w_gate, exp_b_gate, exp_w_up, exp_b_up, exp_w_down, exp_b_down):
    B = x.shape[0]
    mod = (jax.nn.silu(c) @ w_ada + b_ada).reshape(B, 6, 1, D_MODEL)
    sh1, sc1, g1, sh2, sc2, g2 = mod[:, 0], mod[:, 1], mod[:, 2], mod[:, 3], mod[:, 4], mod[:, 5]
    h = _rms_norm(x, norm1_g) * (1.0 + sc1) + sh1
    proj = h @ w_in
    a, new_pool = _pool_mix(proj[..., :D_POOL], pool_buf, pos0, pool_w, pool_scale)
    b, new_conv = _conv_mix(proj[..., D_POOL:], conv_buf, conv_dw, conv_b, conv_ln_g, conv_ln_b, conv_w_out)
    gates = jax.nn.sigmoid(h @ gate_w + gate_b)
    merged = gates[..., :D_MODEL] * a + gates[..., D_MODEL:] * b
    x = x + g1 * (merged @ w_out)
    h2 = _rms_norm(x, norm2_g) * (1.0 + sc2) + sh2
    x = x + g2 * _moe(h2, router_w, router_b, exp_w_gate, exp_b_gate, exp_w_up, exp_b_up, exp_w_down, exp_b_down)
    return x, new_pool, new_conv


def setup_inputs(seed: int = 0) -> dict:
    key = jax.random.key(seed)
    ks = jax.random.split(key, 32)
    f32 = jnp.float32

    def nrm(k, shape, scale):
        return jax.random.normal(k, shape, f32) * scale

    D = D_MODEL
    return {
        "x_prompt": nrm(ks[0], (BATCH, SEQ, D), 1.0),
        "x_sample": nrm(ks[1], (DEC_BATCH, DEC_SEQ, D), 1.0),
        "state_pool": nrm(ks[2], (DEPTH, DEC_BATCH, POOL_PAD, D_POOL), 1.0),
        "state_conv": nrm(ks[3], (DEPTH, DEC_BATCH, CONV_PAD, D_CONV), 1.0),
        "c_prompt": nrm(ks[4], (BATCH, D), 1.0),
        "c_sample": nrm(ks[5], (DEC_BATCH, D), 1.0),
        "norm1_g": 1.0 + nrm(ks[6], (DEPTH, D), 0.02),
        "norm2_g": 1.0 + nrm(ks[7], (DEPTH, D), 0.02),
        "final_g": 1.0 + nrm(ks[8], (D,), 0.02),
        "w_ada": nrm(ks[9], (DEPTH, D, 6 * D), 0.3 * D ** -0.5),
        "b_ada": nrm(ks[10], (DEPTH, 6 * D), 0.02),
        "w_in": nrm(ks[11], (DEPTH, D, D_IN), D ** -0.5),
        "pool_w": nrm(ks[12], (DEPTH, POOL_GROUPS, POOL_GW, POOL_OUT_GW), POOL_GW ** -0.5),
        "pool_scale": 1.0 + nrm(ks[13], (DEPTH, D), 0.02),
        "conv_dw": nrm(ks[14], (DEPTH, CONV_WIDTH, D_CONV), CONV_WIDTH ** -0.5),
        "conv_b": nrm(ks[15], (DEPTH, D_CONV), 0.02),
        "conv_ln_g": 1.0 + nrm(ks[16], (DEPTH, D_CONV), 0.02),
        "conv_ln_b": nrm(ks[17], (DEPTH, D_CONV), 0.02),
        "conv_w_out": nrm(ks[18], (DEPTH, D_CONV, D), D_CONV ** -0.5),
        "gate_w": nrm(ks[19], (DEPTH, D, 2 * D), D ** -0.5),
        "gate_b": nrm(ks[20], (DEPTH, 2 * D), 0.02),
        "w_out": nrm(ks[21], (DEPTH, D, D), D ** -0.5),
        "router_w": nrm(ks[22], (DEPTH, D, N_EXPERTS), D ** -0.5),
        "router_b": nrm(ks[23], (DEPTH, N_EXPERTS), 0.01),
        "exp_w_gate": nrm(ks[24], (DEPTH, N_EXPERTS, D, D_FF), D ** -0.5),
        "exp_b_gate": nrm(ks[25], (DEPTH, N_EXPERTS, D_FF), 0.01),
        "exp_w_up": nrm(ks[26], (DEPTH, N_EXPERTS, D, D_FF), D ** -0.5),
        "exp_b_up": nrm(ks[27], (DEPTH, N_EXPERTS, D_FF), 0.01),
        "exp_w_down": nrm(ks[28], (DEPTH, N_EXPERTS, D_FF, D), D_FF ** -0.5),
        "exp_b_down": nrm(ks[29], (DEPTH, N_EXPERTS, D), 0.01),
    }


def reference(x_prompt, x_sample, state_pool, state_conv, c_prompt, c_sample,
              norm1_g, norm2_g, final_g, w_ada, b_ada, w_in, pool_w, pool_scale,
              conv_dw, conv_b, conv_ln_g, conv_ln_b, conv_w_out, gate_w, gate_b, w_out,
              router_w, router_b, exp_w_gate, exp_b_gate, exp_w_up, exp_b_up, exp_w_down, exp_b_down):
    xp, xs = x_prompt, x_sample
    Bp = x_prompt.shape[0]
    pool_p, conv_p, pool_s, conv_s = [], [], [], []
    for l in range(DEPTH):
        lp = (norm1_g[l], norm2_g[l], w_ada[l], b_ada[l], w_in[l], pool_w[l], pool_scale[l],
              conv_dw[l], conv_b[l], conv_ln_g[l], conv_ln_b[l], conv_w_out[l], gate_w[l], gate_b[l], w_out[l],
              router_w[l], router_b[l], exp_w_gate[l], exp_b_gate[l], exp_w_up[l], exp_b_up[l],
              exp_w_down[l], exp_b_down[l])
        zp = jnp.zeros((Bp, POOL_PAD, D_POOL), xp.dtype)
        zc = jnp.zeros((Bp, CONV_PAD, D_CONV), xp.dtype)
        xp, npp, ncp = _layer(xp, c_prompt, zp, zc, 0, *lp)
        xs, nps, ncs = _layer(xs, c_sample, state_pool[l], state_conv[l], PAST_LEN, *lp)
        pool_p.append(npp); conv_p.append(ncp); pool_s.append(nps); conv_s.append(ncs)
    y_prompt = _rms_norm(xp, final_g)
    y_sample = _rms_norm(xs, final_g)
    new_pool_prompt = jnp.stack(pool_p, axis=0)
    new_conv_prompt = jnp.stack(conv_p, axis=0)
    new_pool_sample = jnp.stack(pool_s, axis=0)
    new_conv_sample = jnp.stack(conv_s, axis=0)
    return (y_prompt, y_sample, new_pool_prompt, new_conv_prompt, new_pool_sample, new_conv_sample)
```

```python
import functools

import jax
import jax.numpy as jnp
from jax import lax
from jax.experimental import pallas as pl
from jax.experimental.pallas import tpu as pltpu

D_MODEL = 1024
D_POOL = 512
D_CONV = 512
POOL_WINDOWS = (2, 4, 8, 16)
POOL_GW = 128
POOL_OUT_GW = 256
POOL_PAD = 15
CONV_WIDTH = 31
CONV_PAD = 30
N_EXPERTS = 32
TOP_K = 4
SWIGLU_LIMIT = 7.0
SWIGLU_ALPHA = 1.702
EPS = 1e-6
PAST_LEN = 2048

POOL_HIST = 16
CONV_HIST = 32
VMEM_LIMIT_BYTES = 56 * 1024 * 1024

F32 = jnp.float32
BF16 = jnp.bfloat16


def _sigmoid(v):
    return 1.0 / (1.0 + jnp.exp(-v))


def _mod_kernel(c_ref, w_ref, b_ref, o_ref):
    c = c_ref[...]
    s = (c * _sigmoid(c)).astype(BF16)
    o_ref[...] = jnp.dot(s, w_ref[...].astype(BF16), preferred_element_type=F32) + b_ref[...]


def _modulation(c, w_ada, b_ada):
    bsz = c.shape[0]
    out = pl.pallas_call(
        _mod_kernel,
        out_shape=jax.ShapeDtypeStruct((bsz, 6 * D_MODEL), F32),
        grid=(6,),
        in_specs=[
            pl.BlockSpec((bsz, D_MODEL), lambda j: (0, 0)),
            pl.BlockSpec((D_MODEL, D_MODEL), lambda j: (0, j)),
            pl.BlockSpec((1, D_MODEL), lambda j: (0, j)),
        ],
        out_specs=pl.BlockSpec((bsz, D_MODEL), lambda j: (0, j)),
        compiler_params=pltpu.CompilerParams(
            dimension_semantics=("arbitrary",), vmem_limit_bytes=VMEM_LIMIT_BYTES),
        name="adaln_mod",
    )(c, w_ada, b_ada.reshape(1, 6 * D_MODEL))
    out = out.reshape(bsz, 6, D_MODEL)
    return jnp.pad(out, ((0, 0), (0, 2), (0, 0)))


def _mix_kernel(x_ref, mod_ref, pinit_ref, cinit_ref, cntin_ref, tri_ref,
                n1_ref, n2_ref, win_ref, pw_ref, ps_ref, dw_ref, cb_ref, lg_ref, lb_ref,
                cwo_ref, gw_ref, gb_ref, wo_ref, rwt_ref, rb_ref,
                x1_ref, h2_ref, ri_ref, rw_ref, comb_ref, npool_ref, nconv_ref, cntout_ref,
                conv_in, pool_in, hbuf, merged, sy, cnt_run, *, tl, pos0, rc):
    b = pl.program_id(0)
    l = pl.program_id(1)

    @pl.when(l == 0)
    def _():
        conv_in[0:CONV_HIST, :] = cinit_ref[0]
        pool_in[0:POOL_HIST, :] = pinit_ref[0]

    @pl.when((b == 0) & (l == 0))
    def _():
        cnt_run[...] = cntin_ref[...]

    x = x_ref[0]
    mod = mod_ref[0]
    sh1, sc1, g1 = mod[0:1], mod[1:2], mod[2:3]
    sh2, sc2, g2 = mod[3:4], mod[4:5], mod[5:6]

    ms = jnp.mean(x * x, axis=-1, keepdims=True)
    h = (x * lax.rsqrt(ms + EPS) * n1_ref[...]) * (1.0 + sc1) + sh1
    hbuf[...] = h.astype(BF16)

    proj = jnp.dot(hbuf[...], win_ref[...], preferred_element_type=F32)
    glu = proj[:, D_POOL:D_POOL + D_CONV] * _sigmoid(proj[:, D_POOL + D_CONV:])
    pool_in[POOL_HIST:POOL_HIST + tl, :] = proj[:, :D_POOL]
    conv_in[CONV_HIST:CONV_HIST + tl, :] = glu
    npool_ref[0] = pool_in[tl:tl + POOL_HIST, :]
    nconv_ref[0] = conv_in[tl:tl + CONV_HIST, :]

    pos = pos0 + l * tl + lax.broadcasted_iota(jnp.int32, (tl, POOL_GW), 0)
    pooled = []
    for gi, w in enumerate(POOL_WINDOWS):
        lanes = slice(gi * POOL_GW, (gi + 1) * POOL_GW)
        cur = pool_in[POOL_HIST:POOL_HIST + tl, lanes]
        s = cur
        for j in range(1, w):
            s = s + pool_in[POOL_HIST - j:POOL_HIST - j + tl, lanes]
        cnt = jnp.minimum(pos + 1, w).astype(F32)
        pooled.append((s / cnt - cur).astype(BF16))

    def conv_chunk(i, carry):
        base = pl.multiple_of(i * rc, rc)
        win = conv_in[pl.ds(base, rc + CONV_HIST), :]
        acc = jnp.broadcast_to(cb_ref[...], (rc, D_CONV))
        for k in range(CONV_WIDTH):
            off = k + CONV_HIST - CONV_PAD
            acc = acc + win[off:off + rc, :] * dw_ref[k:k + 1, :]
        mu = jnp.mean(acc, axis=-1, keepdims=True)
        d = acc - mu
        var = jnp.mean(d * d, axis=-1, keepdims=True)
        yn = d * lax.rsqrt(var + EPS) * lg_ref[...] + lb_ref[...]
        sy[pl.ds(base, rc), :] = (yn * _sigmoid(yn)).astype(BF16)
        return carry

    lax.fori_loop(0, tl // rc, conv_chunk, 0)

    conv_in[0:CONV_HIST, :] = conv_in[tl:tl + CONV_HIST, :]
    pool_in[0:POOL_HIST, :] = pool_in[tl:tl + POOL_HIST, :]

    for j in range(len(POOL_WINDOWS)):
        cs = slice(j * POOL_OUT_GW, (j + 1) * POOL_OUT_GW)
        cs2 = slice(D_MODEL + j * POOL_OUT_GW, D_MODEL + (j + 1) * POOL_OUT_GW)
        ga = _sigmoid(jnp.dot(hbuf[...], gw_ref[:, cs], preferred_element_type=F32) + gb_ref[:, cs])
        gb = _sigmoid(jnp.dot(hbuf[...], gw_ref[:, cs2], preferred_element_type=F32) + gb_ref[:, cs2])
        a_j = jnp.dot(pooled[j], pw_ref[j], preferred_element_type=F32) * ps_ref[:, cs]
        b_j = jnp.dot(sy[...], cwo_ref[:, cs], preferred_element_type=F32)
        merged[:, cs] = (ga * a_j + gb * b_j).astype(BF16)

    x1 = x + g1 * jnp.dot(merged[...], wo_ref[...], preferred_element_type=F32)
    x1_ref[0] = x1

    ms2 = jnp.mean(x1 * x1, axis=-1, keepdims=True)
    h2 = (x1 * lax.rsqrt(ms2 + EPS) * n2_ref[...]) * (1.0 + sc2) + sh2
    h2b = h2.astype(BF16)
    h2_ref[0] = h2b

    logits = lax.dot_general(rwt_ref[...], h2b, (((1,), (1,)), ((), ())),
                             preferred_element_type=F32) + rb_ref[:, 0:1]
    e_iota = lax.broadcasted_iota(jnp.int32, (N_EXPERTS, tl), 0)
    v = logits
    ids, vals = [], []
    for _ in range(TOP_K):
        m = jnp.max(v, axis=0, keepdims=True)
        idx = jnp.min(jnp.where(v == m, e_iota, N_EXPERTS), axis=0, keepdims=True)
        ids.append(idx)
        vals.append(m)
        v = jnp.where(e_iota == idx, -jnp.inf, v)
    ex = [jnp.exp(vk - vals[0]) for vk in vals]
    den = ex[0] + ex[1] + ex[2] + ex[3]
    wts = [e / den for e in ex]

    sel = [e_iota == idx for idx in ids]
    chosen = jnp.where(sel[0] | sel[1] | sel[2] | sel[3], 1.0, 0.0)
    before = jnp.dot(chosen.astype(BF16), tri_ref[...], preferred_element_type=F32)
    rank_all = cnt_run[:, 0:1] + before
    comb = jnp.zeros((N_EXPERTS, tl), F32)
    for k in range(TOP_K):
        ri_ref[0, k:k + 1, :] = ids[k]
        rk = jnp.sum(jnp.where(sel[k], rank_all, 0.0), axis=0, keepdims=True)
        ri_ref[0, TOP_K + k:TOP_K + k + 1, :] = rk.astype(jnp.int32)
        rw_ref[0, k:k + 1, :] = wts[k]
        rw_ref[0, TOP_K + k:TOP_K + k + 1, :] = jnp.zeros((1, tl), F32)
        comb = comb + jnp.where(sel[k], wts[k], 0.0)
    comb_ref[0] = comb
    cnt_run[...] = cnt_run[...] + jnp.sum(chosen, axis=1, keepdims=True)
    cntout_ref[...] = cnt_run[...]


def _token_mix(x, mod, pool_init, conv_init, cnt_in, wts, *, tl, pos0, rc):
    bsz, seq, _ = x.shape
    nl = seq // tl
    tri = jnp.triu(jnp.ones((tl, tl), BF16), k=1)
    full = lambda shape: pl.BlockSpec(shape, lambda b, l: (0,) * len(shape))
    per_b = lambda shape: pl.BlockSpec(shape, lambda b, l: (b,) + (0,) * (len(shape) - 1))
    tile3 = lambda shape: pl.BlockSpec(shape, lambda b, l: (b, l, 0))
    rt = lambda shape: pl.BlockSpec(shape, lambda b, l: (b * nl + l, 0, 0))
    in_specs = [
        tile3((1, tl, D_MODEL)),
        per_b((1, 8, D_MODEL)),
        per_b((1, POOL_HIST, D_POOL)),
        per_b((1, CONV_HIST, D_CONV)),
        full((N_EXPERTS, 128)),
        full((tl, tl)),
        full((1, D_MODEL)), full((1, D_MODEL)),
        full((D_MODEL, D_POOL + 2 * D_CONV)),
        full((len(POOL_WINDOWS), POOL_GW, POOL_OUT_GW)),
        full((1, D_MODEL)),
        full((CONV_WIDTH, D_CONV)),
        full((1, D_CONV)), full((1, D_CONV)), full((1, D_CONV)),
        full((D_CONV, D_MODEL)),
        full((D_MODEL, 2 * D_MODEL)),
        full((1, 2 * D_MODEL)),
        full((D_MODEL, D_MODEL)),
        full((N_EXPERTS, D_MODEL)),
        full((N_EXPERTS, 128)),
    ]
    out_shape = (
        jax.ShapeDtypeStruct((bsz, seq, D_MODEL), F32),
        jax.ShapeDtypeStruct((bsz, seq, D_MODEL), BF16),
        jax.ShapeDtypeStruct((bsz * nl, 2 * TOP_K, tl), jnp.int32),
        jax.ShapeDtypeStruct((bsz * nl, 2 * TOP_K, tl), F32),
        jax.ShapeDtypeStruct((bsz * nl, N_EXPERTS, tl), F32),
        jax.ShapeDtypeStruct((bsz, POOL_HIST, D_POOL), F32),
        jax.ShapeDtypeStruct((bsz, CONV_HIST, D_CONV), F32),
        jax.ShapeDtypeStruct((N_EXPERTS, 128), F32),
    )
    out_specs = (
        tile3((1, tl, D_MODEL)),
        tile3((1, tl, D_MODEL)),
        rt((1, 2 * TOP_K, tl)),
        rt((1, 2 * TOP_K, tl)),
        rt((1, N_EXPERTS, tl)),
        per_b((1, POOL_HIST, D_POOL)),
        per_b((1, CONV_HIST, D_CONV)),
        full((N_EXPERTS, 128)),
    )
    scratch = [
        pltpu.VMEM((tl + CONV_HIST, D_CONV), F32),
        pltpu.VMEM((tl + POOL_HIST, D_POOL), F32),
        pltpu.VMEM((tl, D_MODEL), BF16),
        pltpu.VMEM((tl, D_MODEL), BF16),
        pltpu.VMEM((tl, D_CONV), BF16),
        pltpu.VMEM((N_EXPERTS, 128), F32),
    ]
    return pl.pallas_call(
        functools.partial(_mix_kernel, tl=tl, pos0=pos0, rc=rc),
        out_shape=out_shape,
        grid=(bsz, nl),
        in_specs=in_specs,
        out_specs=out_specs,
        scratch_shapes=scratch,
        compiler_params=pltpu.CompilerParams(
            dimension_semantics=("arbitrary", "arbitrary"), vmem_limit_bytes=VMEM_LIMIT_BYTES),
        name="token_mix",
    )(x, mod, pool_init, conv_init, cnt_in, tri, *wts)


def _moe_kernel(x1_ref, h2_ref, comb_ref, mod_ref, wg_ref, bg_ref, wu_ref, bu_ref, wd_ref, bd_ref,
                fg_ref, y_ref, acc):
    e = pl.program_id(2)

    @pl.when(e == 0)
    def _():
        acc[...] = jnp.zeros_like(acc)

    xt = h2_ref[0]
    gt = jnp.minimum(jnp.dot(xt, wg_ref[0], preferred_element_type=F32) + bg_ref[0], SWIGLU_LIMIT)
    up = jnp.clip(jnp.dot(xt, wu_ref[0], preferred_element_type=F32) + bu_ref[0],
                  -SWIGLU_LIMIT, SWIGLU_LIMIT)
    act = gt * _sigmoid(SWIGLU_ALPHA * gt) * (up + 1.0)
    y = jnp.dot(act.astype(BF16), wd_ref[0], preferred_element_type=F32) + bd_ref[0]
    c = comb_ref[0]
    lane = lax.broadcasted_iota(jnp.int32, c.shape, 1)
    col = jnp.sum(jnp.where(lane == e, c, 0.0), axis=1, keepdims=True)
    acc[...] += col * y

    @pl.when(e == N_EXPERTS - 1)
    def _():
        g2 = mod_ref[0][5:6]
        x2 = x1_ref[0] + g2 * acc[...]
        ms = jnp.mean(x2 * x2, axis=-1, keepdims=True)
        y_ref[0] = x2 * lax.rsqrt(ms + EPS) * fg_ref[...]


def _moe_dense(x1, h2, comb, mod, ew, final_g, *, tm):
    bsz, seq, _ = x1.shape
    nt = seq // tm
    wg, bg, wu, bu, wd, bd = ew
    tok = lambda shape: pl.BlockSpec(shape, lambda b, t, e: (b, t, 0))
    exp3 = lambda shape: pl.BlockSpec(shape, lambda b, t, e: (e, 0, 0))
    return pl.pallas_call(
        _moe_kernel,
        out_shape=jax.ShapeDtypeStruct((bsz, seq, D_MODEL), F32),
        grid=(bsz, nt, N_EXPERTS),
        in_specs=[
            tok((1, tm, D_MODEL)),
            tok((1, tm, D_MODEL)),
            tok((1, tm, N_EXPERTS)),
            pl.BlockSpec((1, 8, D_MODEL), lambda b, t, e: (b, 0, 0)),
            exp3((1, D_MODEL, D_MODEL)), exp3((1, 1, D_MODEL)),
            exp3((1, D_MODEL, D_MODEL)), exp3((1, 1, D_MODEL)),
            exp3((1, D_MODEL, D_MODEL)), exp3((1, 1, D_MODEL)),
            pl.BlockSpec((1, D_MODEL), lambda b, t, e: (0, 0)),
        ],
        out_specs=tok((1, tm, D_MODEL)),
        scratch_shapes=[pltpu.VMEM((tm, D_MODEL), F32)],
        compiler_params=pltpu.CompilerParams(
            dimension_semantics=("arbitrary", "arbitrary", "arbitrary"),
            vmem_limit_bytes=VMEM_LIMIT_BYTES),
        name="moe_dense",
    )(x1, h2, comb, mod, wg, bg, wu, bu, wd, bd, final_g.reshape(1, D_MODEL))


def _group(x, c, pool_state, conv_state, cnt_in, pos0, tl, rc, tm, mix_w, ada, ew, final_g):
    bsz, seq, _ = x.shape
    mod = _modulation(c, *ada)
    pool_init = jnp.pad(pool_state, ((0, 0), (POOL_HIST - POOL_PAD, 0), (0, 0)))
    conv_init = jnp.pad(conv_state, ((0, 0), (CONV_HIST - CONV_PAD, 0), (0, 0)))
    x1, h2, r_i, r_w, comb, npool, nconv, cnt_out = _token_mix(
        x, mod, pool_init, conv_init, cnt_in, mix_w, tl=tl, pos0=pos0, rc=rc)
    nl = seq // tl
    comb_tok = comb.reshape(bsz, nl, N_EXPERTS, tl).transpose(0, 1, 3, 2).reshape(bsz, seq, N_EXPERTS)
    y = _moe_dense(x1, h2, comb_tok, mod, ew, final_g, tm=tm)
    return y, npool[:, POOL_HIST - POOL_PAD:], nconv[:, CONV_HIST - CONV_PAD:], cnt_out


def kernel(x_prompt, x_sample, state_pool, state_conv, c_prompt, c_sample, norm1_g, norm2_g, final_g, w_ada, b_ada, w_in, pool_w, pool_scale, conv_dw, conv_b, conv_ln_g, conv_ln_b, conv_w_out, gate_w, gate_b, w_out, router_w, router_b, exp_w_gate, exp_b_gate, exp_w_up, exp_b_up, exp_w_down, exp_b_down):
    assert norm1_g.shape[0] == 1, "single-layer trunk"
    row = lambda v: v.reshape(1, -1)
    mix_w = (
        row(norm1_g[0]), row(norm2_g[0]),
        w_in[0].astype(BF16), pool_w[0].astype(BF16), row(pool_scale[0]),
        conv_dw[0], row(conv_b[0]), row(conv_ln_g[0]), row(conv_ln_b[0]),
        conv_w_out[0].astype(BF16), gate_w[0].astype(BF16), row(gate_b[0]),
        w_out[0].astype(BF16), router_w[0].T.astype(BF16),
        jnp.broadcast_to(router_b[0][:, None], (N_EXPERTS, 128)),
    )
    ada = (w_ada[0], b_ada[0])
    ew = (
        exp_w_gate[0].astype(BF16), exp_b_gate[0][:, None, :],
        exp_w_up[0].astype(BF16), exp_b_up[0][:, None, :],
        exp_w_down[0].astype(BF16), exp_b_down[0][:, None, :],
    )
    bp = x_prompt.shape[0]
    zeros_cnt = jnp.zeros((N_EXPERTS, 128), F32)
    zp = jnp.zeros((bp, POOL_PAD, D_POOL), F32)
    zc = jnp.zeros((bp, CONV_PAD, D_CONV), F32)
    y_p, np_p, nc_p, cnt = _group(x_prompt, c_prompt, zp, zc, zeros_cnt, 0, 512, 16, 1024,
                                  mix_w, ada, ew, final_g)
    y_s, np_s, nc_s, _ = _group(x_sample, c_sample, state_pool[0], state_conv[0], cnt, PAST_LEN,
                                32, 16, 32, mix_w, ada, ew, final_g)
    return (y_p, y_s, np_p[None], nc_p[None], np_s[None], nc_s[None])
```

```python
import functools

import jax
import jax.numpy as jnp
from jax import lax
from jax.experimental import pallas as pl
from jax.experimental.pallas import tpu as pltpu
from jax.experimental.pallas import tpu_sc as plsc

D_MODEL = 1024
D_POOL = 512
D_CONV = 512
POOL_WINDOWS = (2, 4, 8, 16)
POOL_GW = 128
POOL_OUT_GW = 256
POOL_PAD = 15
CONV_WIDTH = 31
CONV_PAD = 30
N_EXPERTS = 32
TOP_K = 4
SWIGLU_LIMIT = 7.0
SWIGLU_ALPHA = 1.702
EPS = 1e-6
PAST_LEN = 2048

POOL_HIST = 16
CONV_HIST = 32
VMEM_LIMIT_BYTES = 56 * 1024 * 1024

LANES = 128
SUBLANES = 8
D_WORDS = D_MODEL // 2
N_CHUNKS = D_WORDS // LANES
MOE_TILE = 512
SC_CORES = 2
SC_SUBCORES = 16
SC_WORKERS = SC_CORES * SC_SUBCORES
SC_ROWS = 128

F32 = jnp.float32
BF16 = jnp.bfloat16
I32 = jnp.int32
HI_MASK = -65536
NEG_LOG2E = -1.4426950408889634


def _sigmoid(v):
    return 1.0 / (1.0 + jnp.exp2(v * NEG_LOG2E))


def _pack_words(v):
    r = v.astype(BF16).astype(F32)
    hi = lax.bitcast_convert_type(r[:, :D_WORDS], I32)
    lo = lax.bitcast_convert_type(r[:, D_WORDS:], I32)
    return (hi & HI_MASK) | lax.shift_right_logical(lo, 16)


def _unpack_words(w):
    hi = lax.bitcast_convert_type(w & HI_MASK, F32)
    lo = lax.bitcast_convert_type(lax.shift_left(w, 16), F32)
    return hi, lo


def _mod_kernel(c_ref, w_ref, b_ref, o_ref):
    c = c_ref[...]
    s = (c * _sigmoid(c)).astype(BF16)
    o_ref[...] = jnp.dot(s, w_ref[...].astype(BF16), preferred_element_type=F32) + b_ref[...]


def _modulation(c, w_ada, b_ada):
    bsz = c.shape[0]
    out = pl.pallas_call(
        _mod_kernel,
        out_shape=jax.ShapeDtypeStruct((bsz, 6 * D_MODEL), F32),
        grid=(6,),
        in_specs=[
            pl.BlockSpec((bsz, D_MODEL), lambda j: (0, 0)),
            pl.BlockSpec((D_MODEL, D_MODEL), lambda j: (0, j)),
            pl.BlockSpec((1, D_MODEL), lambda j: (0, j)),
        ],
        out_specs=pl.BlockSpec((bsz, D_MODEL), lambda j: (0, j)),
        compiler_params=pltpu.CompilerParams(
            dimension_semantics=("arbitrary",), vmem_limit_bytes=VMEM_LIMIT_BYTES),
        name="adaln_mod",
    )(c, w_ada, b_ada.reshape(1, 6 * D_MODEL))
    out = out.reshape(bsz, 6, D_MODEL)
    return jnp.pad(out, ((0, 0), (0, 2), (0, 0)))


def _mix_kernel(x_ref, mod_ref, pinit_ref, cinit_ref, cntin_ref, tri_ref,
                n1_ref, n2_ref, win_ref, pw_ref, ps_ref, dw_ref, cb_ref, lg_ref, lb_ref,
                cwo_ref, gw_ref, gb_ref, wo_ref, rwt_ref, rb_ref,
                x1_ref, hp0_ref, hp1_ref, hp2_ref, hp3_ref, ri_ref, rw_ref,
                npool_ref, nconv_ref, cntout_ref,
                conv_sh, pool_in, hbuf, merged, ybuf, sy, cnt_run, *, tl, pos0, rc):
    b = pl.program_id(0)
    l = pl.program_id(1)

    conv_in = conv_sh.at[0]

    @pl.when(l == 0)
    def _():
        conv_in[0:CONV_HIST, :] = cinit_ref[0]
        pool_in[0:POOL_HIST, :] = pinit_ref[0]

    @pl.when((b == 0) & (l == 0))
    def _():
        cnt_run[...] = cntin_ref[...]

    x = x_ref[...]
    mod = mod_ref[0]
    sh1, sc1, g1 = mod[0:1], mod[1:2], mod[2:3]
    sh2, sc2 = mod[3:4], mod[4:5]

    ms = jnp.mean(x * x, axis=-1, keepdims=True)
    h = (x * lax.rsqrt(ms + EPS)) * (n1_ref[...] * (1.0 + sc1)) + sh1
    hbuf[...] = h.astype(BF16)

    proj = jnp.dot(hbuf[...], win_ref[...], preferred_element_type=F32)
    glu = proj[:, D_POOL:D_POOL + D_CONV] * _sigmoid(proj[:, D_POOL + D_CONV:])
    pool_in[POOL_HIST:POOL_HIST + tl, :] = proj[:, :D_POOL]
    conv_in[CONV_HIST:CONV_HIST + tl, :] = glu
    npool_ref[0] = pool_in[tl:tl + POOL_HIST, :]
    nconv_ref[0] = conv_in[tl:tl + CONV_HIST, :]

    pos = pos0 + l * tl + lax.broadcasted_iota(I32, (tl, POOL_GW), 0)
    pooled = []
    for gi, w in enumerate(POOL_WINDOWS):
        lanes = slice(gi * POOL_GW, (gi + 1) * POOL_GW)
        s = pool_in[:, lanes]
        d = 1
        while d < w:
            s = s + pltpu.roll(s, d, axis=0)
            d *= 2
        cur = pool_in[POOL_HIST:POOL_HIST + tl, lanes]
        cnt = jnp.minimum(pos + 1, w).astype(F32)
        pooled.append((s[POOL_HIST:] / cnt - cur).astype(BF16))

    n_sh = tl + CONV_HIST - SUBLANES
    for r in range(1, SUBLANES):
        conv_sh[r, 0:n_sh, :] = conv_in[r:r + n_sh, :]

    for base in range(0, tl, rc):
        acc = jnp.broadcast_to(cb_ref[...], (rc // SUBLANES, SUBLANES, D_CONV))
        for k in range(CONV_WIDTH):
            q, r = divmod(k + CONV_HIST - CONV_PAD, SUBLANES)
            start = base + q * SUBLANES
            win = conv_sh[r, start:start + rc, :].reshape(rc // SUBLANES, SUBLANES, D_CONV)
            acc = acc + win * dw_ref[k]
        ybuf[base:base + rc, :] = acc.reshape(rc, D_CONV)

    yc = ybuf[...]
    mu = jnp.mean(yc, axis=-1, keepdims=True)
    d = yc - mu
    var = jnp.mean(d * d, axis=-1, keepdims=True)
    yn = d * lax.rsqrt(var + EPS) * lg_ref[...] + lb_ref[...]
    sy[...] = (yn * _sigmoid(yn)).astype(BF16)

    conv_in[0:CONV_HIST, :] = conv_in[tl:tl + CONV_HIST, :]
    pool_in[0:POOL_HIST, :] = pool_in[tl:tl + POOL_HIST, :]

    for j in range(len(POOL_WINDOWS)):
        cs = slice(j * POOL_OUT_GW, (j + 1) * POOL_OUT_GW)
        cs2 = slice(D_MODEL + j * POOL_OUT_GW, D_MODEL + (j + 1) * POOL_OUT_GW)
        ga = _sigmoid(jnp.dot(hbuf[...], gw_ref[:, cs], preferred_element_type=F32) + gb_ref[:, cs])
        gb = _sigmoid(jnp.dot(hbuf[...], gw_ref[:, cs2], preferred_element_type=F32) + gb_ref[:, cs2])
        a_j = jnp.dot(pooled[j], pw_ref[j], preferred_element_type=F32) * ps_ref[:, cs]
        b_j = jnp.dot(sy[...], cwo_ref[:, cs], preferred_element_type=F32)
        merged[:, cs] = (ga * a_j + gb * b_j).astype(BF16)

    x1 = x + g1 * jnp.dot(merged[...], wo_ref[...], preferred_element_type=F32)
    x1_ref[...] = x1

    ms2 = jnp.mean(x1 * x1, axis=-1, keepdims=True)
    h2 = (x1 * lax.rsqrt(ms2 + EPS)) * (n2_ref[...] * (1.0 + sc2)) + sh2
    h2b = h2.astype(BF16)
    words = _pack_words(h2)
    for j, ref in enumerate((hp0_ref, hp1_ref, hp2_ref, hp3_ref)):
        ref[...] = words[:, j * LANES:(j + 1) * LANES]

    logits = lax.dot_general(rwt_ref[...], h2b, (((1,), (1,)), ((), ())),
                             preferred_element_type=F32) + rb_ref[:, 0:1]
    e_iota = lax.broadcasted_iota(I32, (N_EXPERTS, tl), 0)
    v = logits
    ids, vals = [], []
    for _ in range(TOP_K):
        m = jnp.max(v, axis=0, keepdims=True)
        idx = jnp.min(jnp.where(v == m, e_iota, N_EXPERTS), axis=0, keepdims=True)
        ids.append(idx)
        vals.append(m)
        v = jnp.where(e_iota == idx, -jnp.inf, v)
    ex = [jnp.exp(vk - vals[0]) for vk in vals]
    den = ex[0] + ex[1] + ex[2] + ex[3]

    sel = [e_iota == idx for idx in ids]
    chosen = jnp.where(sel[0] | sel[1] | sel[2] | sel[3], 1.0, 0.0)
    before = jnp.dot(chosen.astype(BF16), tri_ref[...], preferred_element_type=F32)
    rank_all = cnt_run[:, 0:1] + before
    for k in range(TOP_K):
        ri_ref[0, k:k + 1, :] = ids[k]
        rk = jnp.sum(jnp.where(sel[k], rank_all, 0.0), axis=0, keepdims=True)
        ri_ref[0, TOP_K + k:TOP_K + k + 1, :] = rk.astype(I32)
        rw_ref[0, k:k + 1, :] = ex[k] / den
        rw_ref[0, TOP_K + k:TOP_K + k + 1, :] = jnp.zeros((1, tl), F32)
    cnt_run[...] = cnt_run[...] + jnp.sum(chosen, axis=1, keepdims=True)
    cntout_ref[...] = cnt_run[...]


def _token_mix(x, mod, pool_init, conv_init, cnt_in, wts, *, tl, pos0, rc):
    bsz, seq, _ = x.shape
    rows_out = bsz * seq
    nl = seq // tl
    x2d = x.reshape(bsz * seq, D_MODEL)
    tri = jnp.triu(jnp.ones((tl, tl), BF16), k=1)
    full = lambda shape: pl.BlockSpec(shape, lambda b, l: (0,) * len(shape))
    per_b = lambda shape: pl.BlockSpec(shape, lambda b, l: (b,) + (0,) * (len(shape) - 1))
    tok = lambda width: pl.BlockSpec((tl, width), lambda b, l: (b * nl + l, 0))
    rt = lambda shape: pl.BlockSpec(shape, lambda b, l: (b * nl + l, 0, 0))
    in_specs = [
        tok(D_MODEL),
        per_b((1, 8, D_MODEL)),
        per_b((1, POOL_HIST, D_POOL)),
        per_b((1, CONV_HIST, D_CONV)),
        full((N_EXPERTS, LANES)),
        full((tl, tl)),
        full((1, D_MODEL)), full((1, D_MODEL)),
        full((D_MODEL, D_POOL + 2 * D_CONV)),
        full((len(POOL_WINDOWS), POOL_GW, POOL_OUT_GW)),
        full((1, D_MODEL)),
        full((CONV_WIDTH, SUBLANES, D_CONV)),
        full((1, D_CONV)), full((1, D_CONV)), full((1, D_CONV)),
        full((D_CONV, D_MODEL)),
        full((D_MODEL, 2 * D_MODEL)),
        full((1, 2 * D_MODEL)),
        full((D_MODEL, D_MODEL)),
        full((N_EXPERTS, D_MODEL)),
        full((N_EXPERTS, LANES)),
    ]
    out_shape = (
        jax.ShapeDtypeStruct((rows_out, D_MODEL), F32),
        *[jax.ShapeDtypeStruct((rows_out, LANES), I32)] * N_CHUNKS,
        jax.ShapeDtypeStruct((bsz * nl, 2 * TOP_K, tl), I32),
        jax.ShapeDtypeStruct((bsz * nl, 2 * TOP_K, tl), F32),
        jax.ShapeDtypeStruct((bsz, POOL_HIST, D_POOL), F32),
        jax.ShapeDtypeStruct((bsz, CONV_HIST, D_CONV), F32),
        jax.ShapeDtypeStruct((N_EXPERTS, LANES), F32),
    )
    out_specs = (
        tok(D_MODEL),
        *[tok(LANES)] * N_CHUNKS,
        rt((1, 2 * TOP_K, tl)),
        rt((1, 2 * TOP_K, tl)),
        per_b((1, POOL_HIST, D_POOL)),
        per_b((1, CONV_HIST, D_CONV)),
        full((N_EXPERTS, LANES)),
    )
    scratch = [
        pltpu.VMEM((SUBLANES, tl + CONV_HIST, D_CONV), F32),
        pltpu.VMEM((tl + POOL_HIST, D_POOL), F32),
        pltpu.VMEM((tl, D_MODEL), BF16),
        pltpu.VMEM((tl, D_MODEL), BF16),
        pltpu.VMEM((tl, D_CONV), F32),
        pltpu.VMEM((tl, D_CONV), BF16),
        pltpu.VMEM((N_EXPERTS, LANES), F32),
    ]
    return pl.pallas_call(
        functools.partial(_mix_kernel, tl=tl, pos0=pos0, rc=rc),
        out_shape=out_shape,
        grid=(bsz, nl),
        in_specs=in_specs,
        out_specs=out_specs,
        scratch_shapes=scratch,
        compiler_params=pltpu.CompilerParams(
            dimension_semantics=("arbitrary", "arbitrary"), vmem_limit_bytes=VMEM_LIMIT_BYTES),
        name="token_mix",
    )(x2d, mod, pool_init, conv_init, cnt_in, tri, *wts)


def _sc_mesh():
    return plsc.VectorSubcoreMesh(core_axis_name="c", subcore_axis_name="s",
                                  num_cores=SC_CORES, num_subcores=SC_SUBCORES)


def _for_each_chunk(n_tok, fn):
    wid = lax.axis_index("s") * SC_CORES + lax.axis_index("c")
    if n_tok % (8 * SC_WORKERS) == 0 and n_tok // SC_WORKERS >= SC_ROWS:
        per_w = n_tok // SC_WORKERS

        @pl.loop(0, pl.cdiv(per_w, SC_ROWS))
        def _(c):
            fn(pl.multiple_of(wid * per_w + jnp.minimum(c * SC_ROWS, per_w - SC_ROWS), 8))
    else:
        assert n_tok % SC_ROWS == 0 and n_tok // SC_ROWS <= SC_WORKERS

        @pl.when(wid < n_tok // SC_ROWS)
        def _():
            fn(pl.multiple_of(wid * SC_ROWS, SC_ROWS))


_SC_SCRATCH = ([pltpu.VMEM((SC_ROWS, LANES), I32)] * N_CHUNKS
               + [pltpu.VMEM((SC_ROWS,), I32)] * TOP_K
               + [pltpu.SemaphoreType.DMA, pltpu.SemaphoreType.DMA])


def _dispatch(h_groups, pos_groups, n_rows):
    n_g = len(h_groups)

    def body(*refs):
        refs = list(refs)
        h = [[refs.pop(0) for _ in range(N_CHUNKS)] for _ in range(n_g)]
        p = [[refs.pop(0) for _ in range(TOP_K)] for _ in range(n_g)]
        o = [refs.pop(0) for _ in range(N_CHUNKS)]
        rows = [refs.pop(0) for _ in range(N_CHUNKS)]
        idx = [refs.pop(0) for _ in range(TOP_K)]
        sem_in, sem_out = refs

        for hg, pg in zip(h, p):
            def move(s, hg=hg, pg=pg):
                loads = [pltpu.async_copy(hg[j].at[pl.ds(s, SC_ROWS)], rows[j], sem_in)
                         for j in range(N_CHUNKS)]
                loads += [pltpu.async_copy(pg[k].at[pl.ds(s, SC_ROWS)], idx[k], sem_in)
                          for k in range(TOP_K)]
                for cp in loads:
                    cp.wait()
                stores = [pltpu.async_copy(rows[j], o[j].at[idx[k]], sem_out)
                          for j in range(N_CHUNKS) for k in range(TOP_K)]
                for cp in stores:
                    cp.wait()

            _for_each_chunk(hg[0].shape[0], move)

    call = pl.kernel(
        body,
        out_type=tuple(jax.ShapeDtypeStruct((n_rows, LANES), I32) for _ in range(N_CHUNKS)),
        mesh=_sc_mesh(), scratch_types=_SC_SCRATCH, name="sc_dispatch")
    flat = [a for g in h_groups for a in g] + [a for g in pos_groups for a in g]
    return call(*flat)


def _combine_gather(y_chunks, pos_groups):
    n_g = len(pos_groups)

    def body(*refs):
        refs = list(refs)
        y = [refs.pop(0) for _ in range(N_CHUNKS)]
        p = [[refs.pop(0) for _ in range(TOP_K)] for _ in range(n_g)]
        g = [refs.pop(0) for _ in range(n_g)]
        rows = [refs.pop(0) for _ in range(N_CHUNKS)]
        idx = [refs.pop(0) for _ in range(TOP_K)]
        sem_in, sem_out = refs

        for pg, gg in zip(p, g):
            def move(s, pg=pg, gg=gg):
                loads = [pltpu.async_copy(pg[k].at[pl.ds(s, SC_ROWS)], idx[k], sem_in)
                         for k in range(TOP_K)]
                for cp in loads:
                    cp.wait()
                for k in range(TOP_K):
                    gathers = [pltpu.async_copy(y[j].at[idx[k]], rows[j], sem_in)
                               for j in range(N_CHUNKS)]
                    for cp in gathers:
                        cp.wait()
                    stores = [pltpu.async_copy(rows[j], gg.at[k * N_CHUNKS + j, pl.ds(s, SC_ROWS)], sem_out)
                              for j in range(N_CHUNKS)]
                    for cp in stores:
                        cp.wait()

            _for_each_chunk(pg[0].shape[0], move)

    call = pl.kernel(
        body,
        out_type=tuple(jax.ShapeDtypeStruct((TOP_K * N_CHUNKS, pg[0].shape[0], LANES), I32)
                       for pg in pos_groups),
        mesh=_sc_mesh(), scratch_types=_SC_SCRATCH, name="sc_combine_gather")
    return call(*y_chunks, *[a for g in pos_groups for a in g])


def _expert_kernel(te_ref, nt_ref, x0_ref, x1_ref, x2_ref, x3_ref,
                   wg_ref, bg_ref, wu_ref, bu_ref, wd_ref, bd_ref,
                   y0_ref, y1_ref, y2_ref, y3_ref):
    i = pl.program_id(0)

    @pl.when(i < nt_ref[0])
    def _():
        words = jnp.concatenate([x0_ref[...], x1_ref[...], x2_ref[...], x3_ref[...]], axis=1)
        hi, lo = _unpack_words(words)
        xt = jnp.concatenate([hi, lo], axis=1).astype(BF16)
        gt = jnp.minimum(jnp.dot(xt, wg_ref[0], preferred_element_type=F32) + bg_ref[0], SWIGLU_LIMIT)
        up = jnp.clip(jnp.dot(xt, wu_ref[0], preferred_element_type=F32) + bu_ref[0],
                      -SWIGLU_LIMIT, SWIGLU_LIMIT)
        act = gt * _sigmoid(SWIGLU_ALPHA * gt) * (up + 1.0)
        y = jnp.dot(act.astype(BF16), wd_ref[0], preferred_element_type=F32) + bd_ref[0]
        out = _pack_words(y)
        for j, ref in enumerate((y0_ref, y1_ref, y2_ref, y3_ref)):
            ref[...] = out[:, j * LANES:(j + 1) * LANES]


def _expert_ffn(xs_chunks, tile_expert, n_tiles, ew):
    n_rows = xs_chunks[0].shape[0]
    nt_max = n_rows // MOE_TILE
    wg, bg, wu, bu, wd, bd = ew
    rows = pl.BlockSpec((MOE_TILE, LANES), lambda i, te, nt: (i, 0))
    mat = pl.BlockSpec((1, D_MODEL, D_MODEL), lambda i, te, nt: (te[i], 0, 0))
    vec = pl.BlockSpec((1, 1, D_MODEL), lambda i, te, nt: (te[i], 0, 0))
    grid_spec = pltpu.PrefetchScalarGridSpec(
        num_scalar_prefetch=2,
        grid=(nt_max,),
        in_specs=[rows] * N_CHUNKS + [mat, vec, mat, vec, mat, vec],
        out_specs=[rows] * N_CHUNKS,
    )
    return pl.pallas_call(
        _expert_kernel,
        out_shape=[jax.ShapeDtypeStruct((n_rows, LANES), I32)] * N_CHUNKS,
        grid_spec=grid_spec,
        compiler_params=pltpu.CompilerParams(
            dimension_semantics=("arbitrary",), vmem_limit_bytes=VMEM_LIMIT_BYTES),
        name="expert_ffn",
    )(tile_expert, n_tiles, *xs_chunks, wg, bg, wu, bu, wd, bd)


def _final_kernel(x1_ref, g_ref, w_ref, mod_ref, fg_ref, y_ref):
    wt = w_ref[...]
    acc = None
    for k in range(TOP_K):
        words = jnp.concatenate([g_ref[k * N_CHUNKS + j] for j in range(N_CHUNKS)], axis=1)
        hi, lo = _unpack_words(words)
        term = wt[:, k:k + 1] * jnp.concatenate([hi, lo], axis=1)
        acc = term if acc is None else acc + term
    g2 = mod_ref[0][5:6]
    x2 = x1_ref[...] + g2 * acc
    ms = jnp.mean(x2 * x2, axis=-1, keepdims=True)
    y_ref[...] = x2 * lax.rsqrt(ms + EPS) * fg_ref[...]


def _final(x1, gathered, wt_tok, mod, final_g, *, tl, tiles_per_batch):
    n_tok = x1.shape[0]
    return pl.pallas_call(
        _final_kernel,
        out_shape=jax.ShapeDtypeStruct((n_tok, D_MODEL), F32),
        grid=(n_tok // tl,),
        in_specs=[
            pl.BlockSpec((tl, D_MODEL), lambda i: (i, 0)),
            pl.BlockSpec((TOP_K * N_CHUNKS, tl, LANES), lambda i: (0, i, 0)),
            pl.BlockSpec((tl, 2 * TOP_K), lambda i: (i, 0)),
            pl.BlockSpec((1, 8, D_MODEL), lambda i: (i // tiles_per_batch, 0, 0)),
            pl.BlockSpec((1, D_MODEL), lambda i: (0, 0)),
        ],
        out_specs=pl.BlockSpec((tl, D_MODEL), lambda i: (i, 0)),
        compiler_params=pltpu.CompilerParams(
            dimension_semantics=("arbitrary",), vmem_limit_bytes=VMEM_LIMIT_BYTES),
        name="combine_final",
    )(x1, gathered, wt_tok, mod, final_g.reshape(1, D_MODEL))


def _route_rows(r, lo, hi):
    return r[:, lo:hi, :].transpose(1, 0, 2).reshape(hi - lo, -1)


def kernel(x_prompt, x_sample, state_pool, state_conv, c_prompt, c_sample, norm1_g, norm2_g, final_g, w_ada, b_ada, w_in, pool_w, pool_scale, conv_dw, conv_b, conv_ln_g, conv_ln_b, conv_w_out, gate_w, gate_b, w_out, router_w, router_b, exp_w_gate, exp_b_gate, exp_w_up, exp_b_up, exp_w_down, exp_b_down):
    assert norm1_g.shape[0] == 1, "single-layer trunk"
    row = lambda v: v.reshape(1, -1)
    mix_w = (
        row(norm1_g[0]), row(norm2_g[0]),
        w_in[0].astype(BF16), pool_w[0].astype(BF16), row(pool_scale[0]),
        jnp.broadcast_to(conv_dw[0][:, None, :], (CONV_WIDTH, SUBLANES, D_CONV)),
        row(conv_b[0]), row(conv_ln_g[0]), row(conv_ln_b[0]),
        conv_w_out[0].astype(BF16), gate_w[0].astype(BF16), row(gate_b[0]),
        w_out[0].astype(BF16), router_w[0].T.astype(BF16),
        jnp.broadcast_to(router_b[0][:, None], (N_EXPERTS, LANES)),
    )
    ew = (
        exp_w_gate[0].astype(BF16), exp_b_gate[0][:, None, :],
        exp_w_up[0].astype(BF16), exp_b_up[0][:, None, :],
        exp_w_down[0].astype(BF16), exp_b_down[0][:, None, :],
    )
    bp, lp, _ = x_prompt.shape
    bs, ls, _ = x_sample.shape
    t_p, t_s = bp * lp, bs * ls
    t_all = t_p + t_s
    tl_p, tl_s = 512, ls
    n_rows = (TOP_K * t_all // MOE_TILE + N_EXPERTS) * MOE_TILE

    mod_p = _modulation(c_prompt, w_ada[0], b_ada[0])
    mod_s = _modulation(c_sample, w_ada[0], b_ada[0])
    pad_state = lambda s, hist: jnp.pad(s, ((0, 0), (hist - s.shape[1], 0), (0, 0)))

    out_p = _token_mix(
        x_prompt, mod_p, jnp.zeros((bp, POOL_HIST, D_POOL), F32), jnp.zeros((bp, CONV_HIST, D_CONV), F32),
        jnp.zeros((N_EXPERTS, LANES), F32), mix_w, tl=tl_p, pos0=0, rc=32)
    x1_p, hp_p, ri_p, rw_p, npool_p, nconv_p, cnt_p = out_p[0], out_p[1:1 + N_CHUNKS], *out_p[1 + N_CHUNKS:]
    out_s = _token_mix(
        x_sample, mod_s, pad_state(state_pool[0], POOL_HIST), pad_state(state_conv[0], CONV_HIST),
        cnt_p, mix_w, tl=tl_s, pos0=PAST_LEN, rc=32)
    x1_s, hp_s, ri_s, rw_s, npool_s, nconv_s, cnt = out_s[0], out_s[1:1 + N_CHUNKS], *out_s[1 + N_CHUNKS:]

    counts = cnt[:, 0].astype(I32)
    tiles_e = (counts + MOE_TILE - 1) // MOE_TILE
    tile_end = jnp.cumsum(tiles_e)
    row_off = (tile_end - tiles_e) * MOE_TILE
    nt_max = n_rows // MOE_TILE
    tile_expert = jnp.minimum(
        jnp.searchsorted(tile_end, jnp.arange(nt_max, dtype=I32), side="right"), N_EXPERTS - 1).astype(I32)
    n_tiles = tile_end[-1:].astype(I32)

    def slots(ri):
        pos = jnp.take(row_off, _route_rows(ri, 0, TOP_K)) + _route_rows(ri, TOP_K, 2 * TOP_K)
        return [pos[k] for k in range(TOP_K)]

    pos_p, pos_s = slots(ri_p), slots(ri_s)
    xs = _dispatch([hp_p, hp_s], [pos_p, pos_s], n_rows)
    ys = _expert_ffn(xs, tile_expert, n_tiles, ew)
    g_p, g_s = _combine_gather(ys, [pos_p, pos_s])

    y_p = _final(x1_p, g_p, _route_rows(rw_p, 0, 2 * TOP_K).T, mod_p, final_g, tl=tl_p,
                 tiles_per_batch=lp // tl_p)
    y_s = _final(x1_s, g_s, _route_rows(rw_s, 0, 2 * TOP_K).T, mod_s, final_g, tl=tl_s,
                 tiles_per_batch=1)
    unpad = lambda s, n: s[:, s.shape[1] - n:][None]
    return (y_p.reshape(bp, lp, D_MODEL), y_s.reshape(bs, ls, D_MODEL),
            unpad(npool_p, POOL_PAD), unpad(nconv_p, CONV_PAD),
            unpad(npool_s, POOL_PAD), unpad(nconv_s, CONV_PAD))
```

```python
import functools

import jax
import jax.numpy as jnp
from jax import lax
from jax.experimental import pallas as pl
from jax.experimental.pallas import tpu as pltpu
from jax.experimental.pallas import tpu_sc as plsc

D_MODEL = 1024
D_POOL = 512
D_CONV = 512
POOL_WINDOWS = (2, 4, 8, 16)
POOL_GW = 128
POOL_OUT_GW = 256
POOL_PAD = 15
CONV_WIDTH = 31
CONV_PAD = 30
N_EXPERTS = 32
TOP_K = 4
SWIGLU_LIMIT = 7.0
SWIGLU_ALPHA = 1.702
EPS = 1e-6
PAST_LEN = 2048

POOL_HIST = 16
CONV_HIST = 32
VMEM_LIMIT_BYTES = 56 * 1024 * 1024

LANES = 128
SUBLANES = 8
D_WORDS = D_MODEL // 2
N_CHUNKS = D_WORDS // LANES
MOE_TILE = 512
SC_CORES = 2
SC_SUBCORES = 16
SC_WORKERS = SC_CORES * SC_SUBCORES
SC_ROWS = 128

F32 = jnp.float32
BF16 = jnp.bfloat16
I32 = jnp.int32
HI_MASK = -65536
NEG_LOG2E = -1.4426950408889634


def _sigmoid(v):
    return 1.0 / (1.0 + jnp.exp2(v * NEG_LOG2E))


def _pack_words(v):
    r = v.astype(BF16).astype(F32)
    hi = lax.bitcast_convert_type(r[:, :D_WORDS], I32)
    lo = lax.bitcast_convert_type(r[:, D_WORDS:], I32)
    return (hi & HI_MASK) | lax.shift_right_logical(lo, 16)


def _unpack_words(w):
    hi = lax.bitcast_convert_type(w & HI_MASK, F32)
    lo = lax.bitcast_convert_type(lax.shift_left(w, 16), F32)
    return hi, lo


def _mod_kernel(c_ref, w_ref, b_ref, o_ref):
    c = c_ref[...]
    s = (c * _sigmoid(c)).astype(BF16)
    o_ref[...] = jnp.dot(s, w_ref[...].astype(BF16), preferred_element_type=F32) + b_ref[...]


def _modulation(c, w_ada, b_ada):
    bsz = c.shape[0]
    out = pl.pallas_call(
        _mod_kernel,
        out_shape=jax.ShapeDtypeStruct((bsz, 6 * D_MODEL), F32),
        grid=(6,),
        in_specs=[
            pl.BlockSpec((bsz, D_MODEL), lambda j: (0, 0)),
            pl.BlockSpec((D_MODEL, D_MODEL), lambda j: (0, j)),
            pl.BlockSpec((1, D_MODEL), lambda j: (0, j)),
        ],
        out_specs=pl.BlockSpec((bsz, D_MODEL), lambda j: (0, j)),
        compiler_params=pltpu.CompilerParams(
            dimension_semantics=("arbitrary",), vmem_limit_bytes=VMEM_LIMIT_BYTES),
        name="adaln_mod",
    )(c, w_ada, b_ada.reshape(1, 6 * D_MODEL))
    out = out.reshape(bsz, 6, D_MODEL)
    return jnp.pad(out, ((0, 0), (0, 2), (0, 0)))


def _mix_kernel(x_ref, mod_ref, pinit_ref, cinit_ref, cntin_ref, tri_ref,
                n1_ref, n2_ref, win_ref, pw_ref, ps_ref, dw_ref, cb_ref, lg_ref, lb_ref,
                cwo_ref, gw_ref, gb_ref, wo_ref, rwt_ref, rb_ref,
                x1_ref, hp0_ref, hp1_ref, hp2_ref, hp3_ref, ri_ref, rw_ref,
                npool_ref, nconv_ref, cntout_ref,
                conv_sh, pool_in, hbuf, merged, ybuf, sy, cnt_run, *, tl, pos0, rc):
    b = pl.program_id(0)
    l = pl.program_id(1)

    conv_in = conv_sh.at[0]

    @pl.when(l == 0)
    def _():
        conv_in[0:CONV_HIST, :] = cinit_ref[0]
        pool_in[0:POOL_HIST, :] = pinit_ref[0]

    @pl.when((b == 0) & (l == 0))
    def _():
        cnt_run[...] = cntin_ref[...]

    x = x_ref[...]
    mod = mod_ref[0]
    sh1, sc1, g1 = mod[0:1], mod[1:2], mod[2:3]
    sh2, sc2 = mod[3:4], mod[4:5]

    ms = jnp.mean(x * x, axis=-1, keepdims=True)
    h = (x * lax.rsqrt(ms + EPS)) * (n1_ref[...] * (1.0 + sc1)) + sh1
    hbuf[...] = h.astype(BF16)

    proj = jnp.dot(hbuf[...], win_ref[...], preferred_element_type=F32)
    glu = proj[:, D_POOL:D_POOL + D_CONV] * _sigmoid(proj[:, D_POOL + D_CONV:])
    pool_in[POOL_HIST:POOL_HIST + tl, :] = proj[:, :D_POOL]
    conv_in[CONV_HIST:CONV_HIST + tl, :] = glu
    npool_ref[0] = pool_in[tl:tl + POOL_HIST, :]
    nconv_ref[0] = conv_in[tl:tl + CONV_HIST, :]

    pos = pos0 + l * tl + lax.broadcasted_iota(I32, (tl, POOL_GW), 0)
    pooled = []
    for gi, w in enumerate(POOL_WINDOWS):
        lanes = slice(gi * POOL_GW, (gi + 1) * POOL_GW)
        s = pool_in[:, lanes]
        d = 1
        while d < w:
            s = s + pltpu.roll(s, d, axis=0)
            d *= 2
        cur = pool_in[POOL_HIST:POOL_HIST + tl, lanes]
        cnt = jnp.minimum(pos + 1, w).astype(F32)
        pooled.append((s[POOL_HIST:] / cnt - cur).astype(BF16))

    n_sh = tl + CONV_HIST - SUBLANES
    for r in range(1, SUBLANES):
        conv_sh[r, 0:n_sh, :] = conv_in[r:r + n_sh, :]

    for base in range(0, tl, rc):
        acc = jnp.broadcast_to(cb_ref[...], (rc // SUBLANES, SUBLANES, D_CONV))
        for k in range(CONV_WIDTH):
            q, r = divmod(k + CONV_HIST - CONV_PAD, SUBLANES)
            start = base + q * SUBLANES
            win = conv_sh[r, start:start + rc, :].reshape(rc // SUBLANES, SUBLANES, D_CONV)
            acc = acc + win * dw_ref[k]
        ybuf[base:base + rc, :] = acc.reshape(rc, D_CONV)

    yc = ybuf[...]
    mu = jnp.mean(yc, axis=-1, keepdims=True)
    d = yc - mu
    var = jnp.mean(d * d, axis=-1, keepdims=True)
    yn = d * lax.rsqrt(var + EPS) * lg_ref[...] + lb_ref[...]
    sy[...] = (yn * _sigmoid(yn)).astype(BF16)

    conv_in[0:CONV_HIST, :] = conv_in[tl:tl + CONV_HIST, :]
    pool_in[0:POOL_HIST, :] = pool_in[tl:tl + POOL_HIST, :]

    for j in range(len(POOL_WINDOWS)):
        cs = slice(j * POOL_OUT_GW, (j + 1) * POOL_OUT_GW)
        cs2 = slice(D_MODEL + j * POOL_OUT_GW, D_MODEL + (j + 1) * POOL_OUT_GW)
        ga = _sigmoid(jnp.dot(hbuf[...], gw_ref[:, cs], preferred_element_type=F32) + gb_ref[:, cs])
        gb = _sigmoid(jnp.dot(hbuf[...], gw_ref[:, cs2], preferred_element_type=F32) + gb_ref[:, cs2])
        a_j = jnp.dot(pooled[j], pw_ref[j], preferred_element_type=F32) * ps_ref[:, cs]
        b_j = jnp.dot(sy[...], cwo_ref[:, cs], preferred_element_type=F32)
        merged[:, cs] = (ga * a_j + gb * b_j).astype(BF16)

    x1 = x + g1 * jnp.dot(merged[...], wo_ref[...], preferred_element_type=F32)
    x1_ref[...] = x1

    ms2 = jnp.mean(x1 * x1, axis=-1, keepdims=True)
    h2 = (x1 * lax.rsqrt(ms2 + EPS)) * (n2_ref[...] * (1.0 + sc2)) + sh2
    h2b = h2.astype(BF16)
    words = _pack_words(h2)
    for j, ref in enumerate((hp0_ref, hp1_ref, hp2_ref, hp3_ref)):
        ref[...] = words[:, j * LANES:(j + 1) * LANES]

    logits = lax.dot_general(rwt_ref[...], h2b, (((1,), (1,)), ((), ())),
                             preferred_element_type=F32) + rb_ref[:, 0:1]
    e_iota = lax.broadcasted_iota(I32, (N_EXPERTS, tl), 0)
    v = logits
    ids, vals = [], []
    for _ in range(TOP_K):
        m = jnp.max(v, axis=0, keepdims=True)
        idx = jnp.min(jnp.where(v == m, e_iota, N_EXPERTS), axis=0, keepdims=True)
        ids.append(idx)
        vals.append(m)
        v = jnp.where(e_iota == idx, -jnp.inf, v)
    ex = [jnp.exp(vk - vals[0]) for vk in vals]
    den = ex[0] + ex[1] + ex[2] + ex[3]

    sel = [e_iota == idx for idx in ids]
    chosen = jnp.where(sel[0] | sel[1] | sel[2] | sel[3], 1.0, 0.0)
    before = jnp.dot(chosen.astype(BF16), tri_ref[...], preferred_element_type=F32)
    rank_all = cnt_run[:, 0:1] + before
    for k in range(TOP_K):
        ri_ref[0, k:k + 1, :] = ids[k]
        rk = jnp.sum(jnp.where(sel[k], rank_all, 0.0), axis=0, keepdims=True)
        ri_ref[0, TOP_K + k:TOP_K + k + 1, :] = rk.astype(I32)
        rw_ref[0, k:k + 1, :] = ex[k] / den
        rw_ref[0, TOP_K + k:TOP_K + k + 1, :] = jnp.zeros((1, tl), F32)
    cnt_run[...] = cnt_run[...] + jnp.sum(chosen, axis=1, keepdims=True)
    cntout_ref[...] = cnt_run[...]


def _token_mix(x, mod, pool_init, conv_init, cnt_in, wts, *, tl, pos0, rc):
    bsz, seq, _ = x.shape
    rows_out = bsz * seq
    nl = seq // tl
    x2d = x.reshape(bsz * seq, D_MODEL)
    tri = jnp.triu(jnp.ones((tl, tl), BF16), k=1)
    full = lambda shape: pl.BlockSpec(shape, lambda b, l: (0,) * len(shape))
    per_b = lambda shape: pl.BlockSpec(shape, lambda b, l: (b,) + (0,) * (len(shape) - 1))
    tok = lambda width: pl.BlockSpec((tl, width), lambda b, l: (b * nl + l, 0))
    rt = lambda shape: pl.BlockSpec(shape, lambda b, l: (b * nl + l, 0, 0))
    in_specs = [
        tok(D_MODEL),
        per_b((1, 8, D_MODEL)),
        per_b((1, POOL_HIST, D_POOL)),
        per_b((1, CONV_HIST, D_CONV)),
        full((N_EXPERTS, LANES)),
        full((tl, tl)),
        full((1, D_MODEL)), full((1, D_MODEL)),
        full((D_MODEL, D_POOL + 2 * D_CONV)),
        full((len(POOL_WINDOWS), POOL_GW, POOL_OUT_GW)),
        full((1, D_MODEL)),
        full((CONV_WIDTH, SUBLANES, D_CONV)),
        full((1, D_CONV)), full((1, D_CONV)), full((1, D_CONV)),
        full((D_CONV, D_MODEL)),
        full((D_MODEL, 2 * D_MODEL)),
        full((1, 2 * D_MODEL)),
        full((D_MODEL, D_MODEL)),
        full((N_EXPERTS, D_MODEL)),
        full((N_EXPERTS, LANES)),
    ]
    out_shape = (
        jax.ShapeDtypeStruct((rows_out, D_MODEL), F32),
        *[jax.ShapeDtypeStruct((rows_out, LANES), I32)] * N_CHUNKS,
        jax.ShapeDtypeStruct((bsz * nl, 2 * TOP_K, tl), I32),
        jax.ShapeDtypeStruct((bsz * nl, 2 * TOP_K, tl), F32),
        jax.ShapeDtypeStruct((bsz, POOL_HIST, D_POOL), F32),
        jax.ShapeDtypeStruct((bsz, CONV_HIST, D_CONV), F32),
        jax.ShapeDtypeStruct((N_EXPERTS, LANES), F32),
    )
    out_specs = (
        tok(D_MODEL),
        *[tok(LANES)] * N_CHUNKS,
        rt((1, 2 * TOP_K, tl)),
        rt((1, 2 * TOP_K, tl)),
        per_b((1, POOL_HIST, D_POOL)),
        per_b((1, CONV_HIST, D_CONV)),
        full((N_EXPERTS, LANES)),
    )
    scratch = [
        pltpu.VMEM((SUBLANES, tl + CONV_HIST, D_CONV), F32),
        pltpu.VMEM((tl + POOL_HIST, D_POOL), F32),
        pltpu.VMEM((tl, D_MODEL), BF16),
        pltpu.VMEM((tl, D_MODEL), BF16),
        pltpu.VMEM((tl, D_CONV), F32),
        pltpu.VMEM((tl, D_CONV), BF16),
        pltpu.VMEM((N_EXPERTS, LANES), F32),
    ]
    return pl.pallas_call(
        functools.partial(_mix_kernel, tl=tl, pos0=pos0, rc=rc),
        out_shape=out_shape,
        grid=(bsz, nl),
        in_specs=in_specs,
        out_specs=out_specs,
        scratch_shapes=scratch,
        compiler_params=pltpu.CompilerParams(
            dimension_semantics=("arbitrary", "arbitrary"), vmem_limit_bytes=VMEM_LIMIT_BYTES),
        name="token_mix",
    )(x2d, mod, pool_init, conv_init, cnt_in, tri, *wts)


def _sc_mesh():
    return plsc.VectorSubcoreMesh(core_axis_name="c", subcore_axis_name="s",
                                  num_cores=SC_CORES, num_subcores=SC_SUBCORES)


def _for_each_chunk(n_tok, fn):
    wid = lax.axis_index("s") * SC_CORES + lax.axis_index("c")
    if n_tok % (8 * SC_WORKERS) == 0 and n_tok // SC_WORKERS >= SC_ROWS:
        per_w = n_tok // SC_WORKERS

        @pl.loop(0, pl.cdiv(per_w, SC_ROWS))
        def _(c):
            fn(pl.multiple_of(wid * per_w + jnp.minimum(c * SC_ROWS, per_w - SC_ROWS), 8))
    else:
        assert n_tok % SC_ROWS == 0 and n_tok // SC_ROWS <= SC_WORKERS

        @pl.when(wid < n_tok // SC_ROWS)
        def _():
            fn(pl.multiple_of(wid * SC_ROWS, SC_ROWS))


_SC_SCRATCH = ([pltpu.VMEM((SC_ROWS, LANES), I32)] * N_CHUNKS
               + [pltpu.VMEM((SC_ROWS,), I32)] * TOP_K
               + [pltpu.SemaphoreType.DMA, pltpu.SemaphoreType.DMA])


def _dispatch(h_groups, pos_groups, n_rows):
    n_g = len(h_groups)

    def body(*refs):
        refs = list(refs)
        h = [[refs.pop(0) for _ in range(N_CHUNKS)] for _ in range(n_g)]
        p = [[refs.pop(0) for _ in range(TOP_K)] for _ in range(n_g)]
        o = [refs.pop(0) for _ in range(N_CHUNKS)]
        rows = [refs.pop(0) for _ in range(N_CHUNKS)]
        idx = [refs.pop(0) for _ in range(TOP_K)]
        sem_in, sem_out = refs

        for hg, pg in zip(h, p):
            def move(s, hg=hg, pg=pg):
                loads = [pltpu.async_copy(hg[j].at[pl.ds(s, SC_ROWS)], rows[j], sem_in)
                         for j in range(N_CHUNKS)]
                loads += [pltpu.async_copy(pg[k].at[pl.ds(s, SC_ROWS)], idx[k], sem_in)
                          for k in range(TOP_K)]
                for cp in loads:
                    cp.wait()
                stores = [pltpu.async_copy(rows[j], o[j].at[idx[k]], sem_out)
                          for j in range(N_CHUNKS) for k in range(TOP_K)]
                for cp in stores:
                    cp.wait()

            _for_each_chunk(hg[0].shape[0], move)

    call = pl.kernel(
        body,
        out_type=tuple(jax.ShapeDtypeStruct((n_rows, LANES), I32) for _ in range(N_CHUNKS)),
        mesh=_sc_mesh(), scratch_types=_SC_SCRATCH, name="sc_dispatch")
    flat = [a for g in h_groups for a in g] + [a for g in pos_groups for a in g]
    return call(*flat)


def _combine_gather(y_chunks, pos_groups):
    n_g = len(pos_groups)

    def body(*refs):
        refs = list(refs)
        y = [refs.pop(0) for _ in range(N_CHUNKS)]
        p = [[refs.pop(0) for _ in range(TOP_K)] for _ in range(n_g)]
        g = [refs.pop(0) for _ in range(n_g)]
        rows = [refs.pop(0) for _ in range(N_CHUNKS)]
        idx = [refs.pop(0) for _ in range(TOP_K)]
        sem_in, sem_out = refs

        for pg, gg in zip(p, g):
            def move(s, pg=pg, gg=gg):
                loads = [pltpu.async_copy(pg[k].at[pl.ds(s, SC_ROWS)], idx[k], sem_in)
                         for k in range(TOP_K)]
                for cp in loads:
                    cp.wait()
                for k in range(TOP_K):
                    gathers = [pltpu.async_copy(y[j].at[idx[k]], rows[j], sem_in)
                               for j in range(N_CHUNKS)]
                    for cp in gathers:
                        cp.wait()
                    stores = [pltpu.async_copy(rows[j], gg.at[k * N_CHUNKS + j, pl.ds(s, SC_ROWS)], sem_out)
                              for j in range(N_CHUNKS)]
                    for cp in stores:
                        cp.wait()

            _for_each_chunk(pg[0].shape[0], move)

    call = pl.kernel(
        body,
        out_type=tuple(jax.ShapeDtypeStruct((TOP_K * N_CHUNKS, pg[0].shape[0], LANES), I32)
                       for pg in pos_groups),
        mesh=_sc_mesh(), scratch_types=_SC_SCRATCH, name="sc_combine_gather")
    return call(*y_chunks, *[a for g in pos_groups for a in g])


def _expert_kernel(te_ref, nt_ref, x0_ref, x1_ref, x2_ref, x3_ref,
                   wg_ref, bg_ref, wu_ref, bu_ref, wd_ref, bd_ref,
                   y0_ref, y1_ref, y2_ref, y3_ref, wg_bf, wu_bf, wd_bf):
    i = pl.program_id(0)

    @pl.when((i == 0) | (te_ref[i] != te_ref[jnp.maximum(i - 1, 0)]))
    def _():
        wg_bf[...] = wg_ref[0].astype(BF16)
        wu_bf[...] = wu_ref[0].astype(BF16)
        wd_bf[...] = wd_ref[0].astype(BF16)

    @pl.when(i < nt_ref[0])
    def _():
        words = jnp.concatenate([x0_ref[...], x1_ref[...], x2_ref[...], x3_ref[...]], axis=1)
        hi, lo = _unpack_words(words)
        xt = jnp.concatenate([hi, lo], axis=1).astype(BF16)
        gt = jnp.minimum(jnp.dot(xt, wg_bf[...], preferred_element_type=F32) + bg_ref[0], SWIGLU_LIMIT)
        up = jnp.clip(jnp.dot(xt, wu_bf[...], preferred_element_type=F32) + bu_ref[0],
                      -SWIGLU_LIMIT, SWIGLU_LIMIT)
        act = gt * _sigmoid(SWIGLU_ALPHA * gt) * (up + 1.0)
        y = jnp.dot(act.astype(BF16), wd_bf[...], preferred_element_type=F32) + bd_ref[0]
        out = _pack_words(y)
        for j, ref in enumerate((y0_ref, y1_ref, y2_ref, y3_ref)):
            ref[...] = out[:, j * LANES:(j + 1) * LANES]


def _expert_ffn(xs_chunks, tile_expert, n_tiles, ew):
    n_rows = xs_chunks[0].shape[0]
    nt_max = n_rows // MOE_TILE
    wg, bg, wu, bu, wd, bd = ew
    rows = pl.BlockSpec((MOE_TILE, LANES), lambda i, te, nt: (i, 0))
    mat = pl.BlockSpec((1, D_MODEL, D_MODEL), lambda i, te, nt: (te[i], 0, 0))
    vec = pl.BlockSpec((1, 1, D_MODEL), lambda i, te, nt: (te[i], 0, 0))
    grid_spec = pltpu.PrefetchScalarGridSpec(
        num_scalar_prefetch=2,
        grid=(nt_max,),
        in_specs=[rows] * N_CHUNKS + [mat, vec, mat, vec, mat, vec],
        out_specs=[rows] * N_CHUNKS,
        scratch_shapes=[pltpu.VMEM((D_MODEL, D_MODEL), BF16)] * 3,
    )
    return pl.pallas_call(
        _expert_kernel,
        out_shape=[jax.ShapeDtypeStruct((n_rows, LANES), I32)] * N_CHUNKS,
        grid_spec=grid_spec,
        compiler_params=pltpu.CompilerParams(
            dimension_semantics=("arbitrary",), vmem_limit_bytes=VMEM_LIMIT_BYTES),
        name="expert_ffn",
    )(tile_expert, n_tiles, *xs_chunks, wg, bg, wu, bu, wd, bd)


def _final_kernel(x1_ref, g_ref, w_ref, mod_ref, fg_ref, y_ref):
    wt = w_ref[...]
    acc = None
    for k in range(TOP_K):
        words = jnp.concatenate([g_ref[k * N_CHUNKS + j] for j in range(N_CHUNKS)], axis=1)
        hi, lo = _unpack_words(words)
        term = wt[:, k:k + 1] * jnp.concatenate([hi, lo], axis=1)
        acc = term if acc is None else acc + term
    g2 = mod_ref[0][5:6]
    x2 = x1_ref[...] + g2 * acc
    ms = jnp.mean(x2 * x2, axis=-1, keepdims=True)
    y_ref[...] = x2 * lax.rsqrt(ms + EPS) * fg_ref[...]


def _final(x1, gathered, wt_tok, mod, final_g, *, tl, tiles_per_batch):
    n_tok = x1.shape[0]
    return pl.pallas_call(
        _final_kernel,
        out_shape=jax.ShapeDtypeStruct((n_tok, D_MODEL), F32),
        grid=(n_tok // tl,),
        in_specs=[
            pl.BlockSpec((tl, D_MODEL), lambda i: (i, 0)),
            pl.BlockSpec((TOP_K * N_CHUNKS, tl, LANES), lambda i: (0, i, 0)),
            pl.BlockSpec((tl, 2 * TOP_K), lambda i: (i, 0)),
            pl.BlockSpec((1, 8, D_MODEL), lambda i: (i // tiles_per_batch, 0, 0)),
            pl.BlockSpec((1, D_MODEL), lambda i: (0, 0)),
        ],
        out_specs=pl.BlockSpec((tl, D_MODEL), lambda i: (i, 0)),
        compiler_params=pltpu.CompilerParams(
            dimension_semantics=("arbitrary",), vmem_limit_bytes=VMEM_LIMIT_BYTES),
        name="combine_final",
    )(x1, gathered, wt_tok, mod, final_g.reshape(1, D_MODEL))


def _route_rows(r, lo, hi):
    return r[:, lo:hi, :].transpose(1, 0, 2).reshape(hi - lo, -1)


def kernel(x_prompt, x_sample, state_pool, state_conv, c_prompt, c_sample, norm1_g, norm2_g, final_g, w_ada, b_ada, w_in, pool_w, pool_scale, conv_dw, conv_b, conv_ln_g, conv_ln_b, conv_w_out, gate_w, gate_b, w_out, router_w, router_b, exp_w_gate, exp_b_gate, exp_w_up, exp_b_up, exp_w_down, exp_b_down):
    assert norm1_g.shape[0] == 1, "single-layer trunk"
    row = lambda v: v.reshape(1, -1)
    mix_w = (
        row(norm1_g[0]), row(norm2_g[0]),
        w_in[0].astype(BF16), pool_w[0].astype(BF16), row(pool_scale[0]),
        jnp.broadcast_to(conv_dw[0][:, None, :], (CONV_WIDTH, SUBLANES, D_CONV)),
        row(conv_b[0]), row(conv_ln_g[0]), row(conv_ln_b[0]),
        conv_w_out[0].astype(BF16), gate_w[0].astype(BF16), row(gate_b[0]),
        w_out[0].astype(BF16), router_w[0].T.astype(BF16),
        jnp.broadcast_to(router_b[0][:, None], (N_EXPERTS, LANES)),
    )
    ew = (
        exp_w_gate[0], exp_b_gate[0][:, None, :],
        exp_w_up[0], exp_b_up[0][:, None, :],
        exp_w_down[0], exp_b_down[0][:, None, :],
    )
    bp, lp, _ = x_prompt.shape
    bs, ls, _ = x_sample.shape
    t_p, t_s = bp * lp, bs * ls
    t_all = t_p + t_s
    tl_p, tl_s = 512, ls
    n_rows = (TOP_K * t_all // MOE_TILE + N_EXPERTS) * MOE_TILE

    mod_p = _modulation(c_prompt, w_ada[0], b_ada[0])
    mod_s = _modulation(c_sample, w_ada[0], b_ada[0])
    pad_state = lambda s, hist: jnp.pad(s, ((0, 0), (hist - s.shape[1], 0), (0, 0)))

    out_p = _token_mix(
        x_prompt, mod_p, jnp.zeros((bp, POOL_HIST, D_POOL), F32), jnp.zeros((bp, CONV_HIST, D_CONV), F32),
        jnp.zeros((N_EXPERTS, LANES), F32), mix_w, tl=tl_p, pos0=0, rc=32)
    x1_p, hp_p, ri_p, rw_p, npool_p, nconv_p, cnt_p = out_p[0], out_p[1:1 + N_CHUNKS], *out_p[1 + N_CHUNKS:]
    out_s = _token_mix(
        x_sample, mod_s, pad_state(state_pool[0], POOL_HIST), pad_state(state_conv[0], CONV_HIST),
        cnt_p, mix_w, tl=tl_s, pos0=PAST_LEN, rc=32)
    x1_s, hp_s, ri_s, rw_s, npool_s, nconv_s, cnt = out_s[0], out_s[1:1 + N_CHUNKS], *out_s[1 + N_CHUNKS:]

    counts = cnt[:, 0].astype(I32)
    tiles_e = (counts + MOE_TILE - 1) // MOE_TILE
    tile_end = jnp.cumsum(tiles_e)
    row_off = (tile_end - tiles_e) * MOE_TILE
    nt_max = n_rows // MOE_TILE
    tile_expert = jnp.minimum(
        jnp.sum(jnp.arange(nt_max, dtype=I32)[:, None] >= tile_end[None, :], axis=1), N_EXPERTS - 1).astype(I32)
    n_tiles = tile_end[-1:].astype(I32)

    def slots(ri):
        ids, ranks = _route_rows(ri, 0, TOP_K), _route_rows(ri, TOP_K, 2 * TOP_K)
        pos = ranks
        for e in range(N_EXPERTS):
            pos = pos + jnp.where(ids == e, row_off[e], 0)
        return [pos[k] for k in range(TOP_K)]

    pos_p, pos_s = slots(ri_p), slots(ri_s)
    xs = _dispatch([hp_p, hp_s], [pos_p, pos_s], n_rows)
    ys = _expert_ffn(xs, tile_expert, n_tiles, ew)
    g_p, g_s = _combine_gather(ys, [pos_p, pos_s])

    y_p = _final(x1_p, g_p, _route_rows(rw_p, 0, 2 * TOP_K).T, mod_p, final_g, tl=tl_p,
                 tiles_per_batch=lp // tl_p)
    y_s = _final(x1_s, g_s, _route_rows(rw_s, 0, 2 * TOP_K).T, mod_s, final_g, tl=tl_s,
                 tiles_per_batch=1)
    unpad = lambda s, n: s[:, s.shape[1] - n:][None]
    return (y_p.reshape(bp, lp, D_MODEL), y_s.reshape(bs, ls, D_MODEL),
            unpad(npool_p, POOL_PAD), unpad(nconv_p, CONV_PAD),
            unpad(npool_s, POOL_PAD), unpad(nconv_s, CONV_PAD))
```

```python
import functools

import jax
import jax.numpy as jnp
from jax import lax
from jax.experimental import pallas as pl
from jax.experimental.pallas import tpu as pltpu
from jax.experimental.pallas import tpu_sc as plsc

D_MODEL = 1024
D_POOL = 512
D_CONV = 512
POOL_WINDOWS = (2, 4, 8, 16)
POOL_GW = 128
POOL_OUT_GW = 256
POOL_PAD = 15
CONV_WIDTH = 31
CONV_PAD = 30
N_EXPERTS = 32
TOP_K = 4
SWIGLU_LIMIT = 7.0
SWIGLU_ALPHA = 1.702
EPS = 1e-6
PAST_LEN = 2048

POOL_HIST = 16
CONV_HIST = 32
VMEM_LIMIT_BYTES = 56 * 1024 * 1024

LANES = 128
SUBLANES = 8
D_WORDS = D_MODEL // 2
N_CHUNKS = D_WORDS // LANES
MOE_TILE = 512
PROMPT_CHUNKS = 2
SC_CORES = 2
SC_SUBCORES = 16
SC_WORKERS = SC_CORES * SC_SUBCORES
SC_ROWS = 128

F32 = jnp.float32
BF16 = jnp.bfloat16
I32 = jnp.int32
HI_MASK = -65536
NEG_LOG2E = -1.4426950408889634


def _sigmoid(v):
    return 1.0 / (1.0 + jnp.exp2(v * NEG_LOG2E))


def _pack_words(v):
    r = v.astype(BF16).astype(F32)
    hi = lax.bitcast_convert_type(r[:, :D_WORDS], I32)
    lo = lax.bitcast_convert_type(r[:, D_WORDS:], I32)
    return (hi & HI_MASK) | lax.shift_right_logical(lo, 16)


def _unpack_words(w):
    hi = lax.bitcast_convert_type(w & HI_MASK, F32)
    lo = lax.bitcast_convert_type(lax.shift_left(w, 16), F32)
    return hi, lo


def _mod_kernel(c_ref, w_ref, b_ref, o_ref):
    c = c_ref[...]
    s = (c * _sigmoid(c)).astype(BF16)
    o_ref[...] = jnp.dot(s, w_ref[...].astype(BF16), preferred_element_type=F32) + b_ref[...]


def _modulation(c, w_ada, b_ada):
    bsz = c.shape[0]
    out = pl.pallas_call(
        _mod_kernel,
        out_shape=jax.ShapeDtypeStruct((bsz, 6 * D_MODEL), F32),
        grid=(6,),
        in_specs=[
            pl.BlockSpec((bsz, D_MODEL), lambda j: (0, 0)),
            pl.BlockSpec((D_MODEL, D_MODEL), lambda j: (0, j)),
            pl.BlockSpec((1, D_MODEL), lambda j: (0, j)),
        ],
        out_specs=pl.BlockSpec((bsz, D_MODEL), lambda j: (0, j)),
        compiler_params=pltpu.CompilerParams(
            dimension_semantics=("arbitrary",), vmem_limit_bytes=VMEM_LIMIT_BYTES),
        name="adaln_mod",
    )(c, w_ada, b_ada.reshape(1, 6 * D_MODEL))
    out = out.reshape(bsz, 6, D_MODEL)
    return jnp.pad(out, ((0, 0), (0, 2), (0, 0)))


def _mix_kernel(x_ref, mod_ref, pinit_ref, cinit_ref, cntin_ref, tri_ref,
                n1_ref, n2_ref, win_ref, pw_ref, ps_ref, dw_ref, cb_ref, lg_ref, lb_ref,
                cwo_ref, gw_ref, gb_ref, wo_ref, rwt_ref, rb_ref,
                x1_ref, hp0_ref, hp1_ref, hp2_ref, hp3_ref, ri_ref, rw_ref,
                npool_ref, nconv_ref, cntout_ref,
                conv_sh, pool_in, hbuf, merged, ybuf, sy, cnt_run, *, tl, th, pos0, rc):
    b = pl.program_id(0)
    l = pl.program_id(1)

    conv_in = conv_sh.at[0]

    @pl.when(l == 0)
    def _():
        conv_in[0:CONV_HIST, :] = cinit_ref[0]
        pool_in[0:POOL_HIST, :] = pinit_ref[0]

    @pl.when((b == 0) & (l == 0))
    def _():
        cnt_run[...] = cntin_ref[...]

    mod = mod_ref[0]
    sh1, sc1, g1 = mod[0:1], mod[1:2], mod[2:3]
    sh2, sc2 = mod[3:4], mod[4:5]
    scale1 = n1_ref[...] * (1.0 + sc1)
    scale2 = n2_ref[...] * (1.0 + sc2)

    for r0 in range(0, tl, th):
        rows = slice(r0, r0 + th)
        x = x_ref[rows, :]
        ms = jnp.mean(x * x, axis=-1, keepdims=True)
        hbuf[rows, :] = ((x * lax.rsqrt(ms + EPS)) * scale1 + sh1).astype(BF16)

        proj = jnp.dot(hbuf[rows, :], win_ref[...], preferred_element_type=F32)
        glu = proj[:, D_POOL:D_POOL + D_CONV] * _sigmoid(proj[:, D_POOL + D_CONV:])
        pool_in[POOL_HIST + r0:POOL_HIST + r0 + th, :] = proj[:, :D_POOL]
        conv_in[CONV_HIST + r0:CONV_HIST + r0 + th, :] = glu

        pos = pos0 + l * tl + r0 + lax.broadcasted_iota(I32, (th, POOL_GW), 0)
        pooled = []
        for gi, w in enumerate(POOL_WINDOWS):
            lanes = slice(gi * POOL_GW, (gi + 1) * POOL_GW)
            s = pool_in[r0:r0 + POOL_HIST + th, lanes]
            d = 1
            while d < w:
                s = s + pltpu.roll(s, d, axis=0)
                d *= 2
            cur = pool_in[POOL_HIST + r0:POOL_HIST + r0 + th, lanes]
            cnt = jnp.minimum(pos + 1, w).astype(F32)
            pooled.append((s[POOL_HIST:] / cnt - cur).astype(BF16))

        sh_lo = 0 if r0 == 0 else r0 + CONV_HIST - SUBLANES
        sh_hi = r0 + th + CONV_HIST - SUBLANES
        for r in range(1, SUBLANES):
            conv_sh[r, sh_lo:sh_hi, :] = conv_in[sh_lo + r:sh_hi + r, :]

        for base in range(r0, r0 + th, rc):
            acc = jnp.broadcast_to(cb_ref[...], (rc // SUBLANES, SUBLANES, D_CONV))
            for k in range(CONV_WIDTH):
                q, r = divmod(k + CONV_HIST - CONV_PAD, SUBLANES)
                start = base + q * SUBLANES
                win = conv_sh[r, start:start + rc, :].reshape(rc // SUBLANES, SUBLANES, D_CONV)
                acc = acc + win * dw_ref[k]
            ybuf[base:base + rc, :] = acc.reshape(rc, D_CONV)

        yc = ybuf[rows, :]
        mu = jnp.mean(yc, axis=-1, keepdims=True)
        dev = yc - mu
        var = jnp.mean(dev * dev, axis=-1, keepdims=True)
        yn = dev * lax.rsqrt(var + EPS) * lg_ref[...] + lb_ref[...]
        sy[rows, :] = (yn * _sigmoid(yn)).astype(BF16)

        for j in range(len(POOL_WINDOWS)):
            cs = slice(j * POOL_OUT_GW, (j + 1) * POOL_OUT_GW)
            cs2 = slice(D_MODEL + j * POOL_OUT_GW, D_MODEL + (j + 1) * POOL_OUT_GW)
            ga = _sigmoid(jnp.dot(hbuf[rows, :], gw_ref[:, cs], preferred_element_type=F32) + gb_ref[:, cs])
            gb = _sigmoid(jnp.dot(hbuf[rows, :], gw_ref[:, cs2], preferred_element_type=F32) + gb_ref[:, cs2])
            a_j = jnp.dot(pooled[j], pw_ref[j], preferred_element_type=F32) * ps_ref[:, cs]
            b_j = jnp.dot(sy[rows, :], cwo_ref[:, cs], preferred_element_type=F32)
            merged[rows, cs] = (ga * a_j + gb * b_j).astype(BF16)

        x1 = x + g1 * jnp.dot(merged[rows, :], wo_ref[...], preferred_element_type=F32)
        x1_ref[rows, :] = x1

        ms2 = jnp.mean(x1 * x1, axis=-1, keepdims=True)
        h2 = (x1 * lax.rsqrt(ms2 + EPS)) * scale2 + sh2
        h2b = h2.astype(BF16)
        words = _pack_words(h2)
        for j, ref in enumerate((hp0_ref, hp1_ref, hp2_ref, hp3_ref)):
            ref[rows, :] = words[:, j * LANES:(j + 1) * LANES]

        logits = lax.dot_general(rwt_ref[...], h2b, (((1,), (1,)), ((), ())),
                                 preferred_element_type=F32) + rb_ref[:, 0:1]
        e_iota = lax.broadcasted_iota(I32, (N_EXPERTS, th), 0)
        v = logits
        ids, vals = [], []
        for _ in range(TOP_K):
            m = jnp.max(v, axis=0, keepdims=True)
            idx = jnp.min(jnp.where(v == m, e_iota, N_EXPERTS), axis=0, keepdims=True)
            ids.append(idx)
            vals.append(m)
            v = jnp.where(e_iota == idx, -jnp.inf, v)
        ex = [jnp.exp(vk - vals[0]) for vk in vals]
        den = ex[0] + ex[1] + ex[2] + ex[3]

        sel = [e_iota == idx for idx in ids]
        chosen = jnp.where(sel[0] | sel[1] | sel[2] | sel[3], 1.0, 0.0)
        before = jnp.dot(chosen.astype(BF16), tri_ref[...], preferred_element_type=F32)
        rank_all = cnt_run[:, 0:1] + before
        for k in range(TOP_K):
            ri_ref[0, k:k + 1, rows] = ids[k]
            rk = jnp.sum(jnp.where(sel[k], rank_all, 0.0), axis=0, keepdims=True)
            ri_ref[0, TOP_K + k:TOP_K + k + 1, rows] = rk.astype(I32)
            rw_ref[0, k:k + 1, rows] = ex[k] / den
            rw_ref[0, TOP_K + k:TOP_K + k + 1, rows] = jnp.zeros((1, th), F32)
        cnt_run[...] = cnt_run[...] + jnp.sum(chosen, axis=1, keepdims=True)

    npool_ref[0] = pool_in[tl:tl + POOL_HIST, :]
    nconv_ref[0] = conv_in[tl:tl + CONV_HIST, :]
    conv_in[0:CONV_HIST, :] = conv_in[tl:tl + CONV_HIST, :]
    pool_in[0:POOL_HIST, :] = pool_in[tl:tl + POOL_HIST, :]
    cntout_ref[...] = cnt_run[...]


def _token_mix(x, mod, pool_init, conv_init, cnt_in, wts, *, b0, tl, th, pos0, rc):
    _, seq, _ = x.shape
    bsz = pool_init.shape[0]
    assert seq % tl == 0 and tl % th == 0 and th % rc == 0 and th >= CONV_HIST
    assert th % LANES == 0 or th == tl
    rows_out = bsz * seq
    nl = seq // tl
    x2d = x.reshape(-1, D_MODEL)
    tri = jnp.triu(jnp.ones((th, th), BF16), k=1)
    full = lambda shape: pl.BlockSpec(shape, lambda b, l: (0,) * len(shape))
    per_b = lambda shape: pl.BlockSpec(shape, lambda b, l: (b,) + (0,) * (len(shape) - 1))
    tok = lambda width: pl.BlockSpec((tl, width), lambda b, l: (b * nl + l, 0))
    rt = lambda shape: pl.BlockSpec(shape, lambda b, l: (b * nl + l, 0, 0))
    in_specs = [
        pl.BlockSpec((tl, D_MODEL), lambda b, l: ((b0 + b) * nl + l, 0)),
        pl.BlockSpec((1, 8, D_MODEL), lambda b, l: (b0 + b, 0, 0)),
        per_b((1, POOL_HIST, D_POOL)),
        per_b((1, CONV_HIST, D_CONV)),
        full((N_EXPERTS, LANES)),
        full((th, th)),
        full((1, D_MODEL)), full((1, D_MODEL)),
        full((D_MODEL, D_POOL + 2 * D_CONV)),
        full((len(POOL_WINDOWS), POOL_GW, POOL_OUT_GW)),
        full((1, D_MODEL)),
        full((CONV_WIDTH, SUBLANES, D_CONV)),
        full((1, D_CONV)), full((1, D_CONV)), full((1, D_CONV)),
        full((D_CONV, D_MODEL)),
        full((D_MODEL, 2 * D_MODEL)),
        full((1, 2 * D_MODEL)),
        full((D_MODEL, D_MODEL)),
        full((N_EXPERTS, D_MODEL)),
        full((N_EXPERTS, LANES)),
    ]
    out_shape = (
        jax.ShapeDtypeStruct((rows_out, D_MODEL), F32),
        *[jax.ShapeDtypeStruct((rows_out, LANES), I32)] * N_CHUNKS,
        jax.ShapeDtypeStruct((bsz * nl, 2 * TOP_K, tl), I32),
        jax.ShapeDtypeStruct((bsz * nl, 2 * TOP_K, tl), F32),
        jax.ShapeDtypeStruct((bsz, POOL_HIST, D_POOL), F32),
        jax.ShapeDtypeStruct((bsz, CONV_HIST, D_CONV), F32),
        jax.ShapeDtypeStruct((N_EXPERTS, LANES), F32),
    )
    out_specs = (
        tok(D_MODEL),
        *[tok(LANES)] * N_CHUNKS,
        rt((1, 2 * TOP_K, tl)),
        rt((1, 2 * TOP_K, tl)),
        per_b((1, POOL_HIST, D_POOL)),
        per_b((1, CONV_HIST, D_CONV)),
        full((N_EXPERTS, LANES)),
    )
    scratch = [
        pltpu.VMEM((SUBLANES, tl + CONV_HIST, D_CONV), F32),
        pltpu.VMEM((tl + POOL_HIST, D_POOL), F32),
        pltpu.VMEM((tl, D_MODEL), BF16),
        pltpu.VMEM((tl, D_MODEL), BF16),
        pltpu.VMEM((tl, D_CONV), F32),
        pltpu.VMEM((tl, D_CONV), BF16),
        pltpu.VMEM((N_EXPERTS, LANES), F32),
    ]
    return pl.pallas_call(
        functools.partial(_mix_kernel, tl=tl, th=th, pos0=pos0, rc=rc),
        out_shape=out_shape,
        grid=(bsz, nl),
        in_specs=in_specs,
        out_specs=out_specs,
        scratch_shapes=scratch,
        compiler_params=pltpu.CompilerParams(
            dimension_semantics=("arbitrary", "arbitrary"), vmem_limit_bytes=VMEM_LIMIT_BYTES),
        name="token_mix",
    )(x2d, mod, pool_init, conv_init, cnt_in, tri, *wts)


def _sc_mesh():
    return plsc.VectorSubcoreMesh(core_axis_name="c", subcore_axis_name="s",
                                  num_cores=SC_CORES, num_subcores=SC_SUBCORES)


def _for_each_chunk(n_tok, fn):
    wid = lax.axis_index("s") * SC_CORES + lax.axis_index("c")
    if n_tok % (8 * SC_WORKERS) == 0 and n_tok // SC_WORKERS >= SC_ROWS:
        per_w = n_tok // SC_WORKERS

        @pl.loop(0, pl.cdiv(per_w, SC_ROWS))
        def _(c):
            fn(pl.multiple_of(wid * per_w + jnp.minimum(c * SC_ROWS, per_w - SC_ROWS), 8))
    else:
        assert n_tok % SC_ROWS == 0 and n_tok // SC_ROWS <= SC_WORKERS

        @pl.when(wid < n_tok // SC_ROWS)
        def _():
            fn(pl.multiple_of(wid * SC_ROWS, SC_ROWS))


_SC_SCRATCH = ([pltpu.VMEM((SC_ROWS, LANES), I32)] * N_CHUNKS
               + [pltpu.VMEM((SC_ROWS,), I32)] * TOP_K
               + [pltpu.SemaphoreType.DMA, pltpu.SemaphoreType.DMA])


def _dispatch(h_groups, pos_groups, n_rows):
    n_g = len(h_groups)

    def body(*refs):
        refs = list(refs)
        h = [[refs.pop(0) for _ in range(N_CHUNKS)] for _ in range(n_g)]
        p = [[refs.pop(0) for _ in range(TOP_K)] for _ in range(n_g)]
        o = [refs.pop(0) for _ in range(N_CHUNKS)]
        rows = [refs.pop(0) for _ in range(N_CHUNKS)]
        idx = [refs.pop(0) for _ in range(TOP_K)]
        sem_in, sem_out = refs

        for hg, pg in zip(h, p):
            def move(s, hg=hg, pg=pg):
                loads = [pltpu.async_copy(hg[j].at[pl.ds(s, SC_ROWS)], rows[j], sem_in)
                         for j in range(N_CHUNKS)]
                loads += [pltpu.async_copy(pg[k].at[pl.ds(s, SC_ROWS)], idx[k], sem_in)
                          for k in range(TOP_K)]
                for cp in loads:
                    cp.wait()
                stores = [pltpu.async_copy(rows[j], o[j].at[idx[k]], sem_out)
                          for j in range(N_CHUNKS) for k in range(TOP_K)]
                for cp in stores:
                    cp.wait()

            _for_each_chunk(hg[0].shape[0], move)

    call = pl.kernel(
        body,
        out_type=tuple(jax.ShapeDtypeStruct((n_rows, LANES), I32) for _ in range(N_CHUNKS)),
        mesh=_sc_mesh(), scratch_types=_SC_SCRATCH, name="sc_dispatch")
    flat = [a for g in h_groups for a in g] + [a for g in pos_groups for a in g]
    return call(*flat)


def _combine_gather(y_chunks, pos_groups):
    n_g = len(pos_groups)

    def body(*refs):
        refs = list(refs)
        y = [refs.pop(0) for _ in range(N_CHUNKS)]
        p = [[refs.pop(0) for _ in range(TOP_K)] for _ in range(n_g)]
        g = [refs.pop(0) for _ in range(n_g)]
        rows = [refs.pop(0) for _ in range(N_CHUNKS)]
        idx = [refs.pop(0) for _ in range(TOP_K)]
        sem_in, sem_out = refs

        for pg, gg in zip(p, g):
            def move(s, pg=pg, gg=gg):
                loads = [pltpu.async_copy(pg[k].at[pl.ds(s, SC_ROWS)], idx[k], sem_in)
                         for k in range(TOP_K)]
                for cp in loads:
                    cp.wait()
                for k in range(TOP_K):
                    gathers = [pltpu.async_copy(y[j].at[idx[k]], rows[j], sem_in)
                               for j in range(N_CHUNKS)]
                    for cp in gathers:
                        cp.wait()
                    stores = [pltpu.async_copy(rows[j], gg.at[k * N_CHUNKS + j, pl.ds(s, SC_ROWS)], sem_out)
                              for j in range(N_CHUNKS)]
                    for cp in stores:
                        cp.wait()

            _for_each_chunk(pg[0].shape[0], move)

    call = pl.kernel(
        body,
        out_type=tuple(jax.ShapeDtypeStruct((TOP_K * N_CHUNKS, pg[0].shape[0], LANES), I32)
                       for pg in pos_groups),
        mesh=_sc_mesh(), scratch_types=_SC_SCRATCH, name="sc_combine_gather")
    return call(*y_chunks, *[a for g in pos_groups for a in g])


def _expert_kernel(te_ref, nt_ref, x0_ref, x1_ref, x2_ref, x3_ref,
                   wg_ref, bg_ref, wu_ref, bu_ref, wd_ref, bd_ref,
                   y0_ref, y1_ref, y2_ref, y3_ref, wg_bf, wu_bf, wd_bf):
    i = pl.program_id(0)

    @pl.when((i == 0) | (te_ref[i] != te_ref[jnp.maximum(i - 1, 0)]))
    def _():
        wg_bf[...] = wg_ref[0].astype(BF16)
        wu_bf[...] = wu_ref[0].astype(BF16)
        wd_bf[...] = wd_ref[0].astype(BF16)

    @pl.when(i < nt_ref[0])
    def _():
        words = jnp.concatenate([x0_ref[...], x1_ref[...], x2_ref[...], x3_ref[...]], axis=1)
        hi, lo = _unpack_words(words)
        xt = jnp.concatenate([hi, lo], axis=1).astype(BF16)
        gt = jnp.minimum(jnp.dot(xt, wg_bf[...], preferred_element_type=F32) + bg_ref[0], SWIGLU_LIMIT)
        up = jnp.clip(jnp.dot(xt, wu_bf[...], preferred_element_type=F32) + bu_ref[0],
                      -SWIGLU_LIMIT, SWIGLU_LIMIT)
        act = gt * _sigmoid(SWIGLU_ALPHA * gt) * (up + 1.0)
        y = jnp.dot(act.astype(BF16), wd_bf[...], preferred_element_type=F32) + bd_ref[0]
        out = _pack_words(y)
        for j, ref in enumerate((y0_ref, y1_ref, y2_ref, y3_ref)):
            ref[...] = out[:, j * LANES:(j + 1) * LANES]


def _expert_ffn(xs_chunks, tile_expert, n_tiles, ew):
    n_rows = xs_chunks[0].shape[0]
    nt_max = n_rows // MOE_TILE
    wg, bg, wu, bu, wd, bd = ew
    rows = pl.BlockSpec((MOE_TILE, LANES), lambda i, te, nt: (i, 0))
    mat = pl.BlockSpec((1, D_MODEL, D_MODEL), lambda i, te, nt: (te[i], 0, 0))
    vec = pl.BlockSpec((1, 1, D_MODEL), lambda i, te, nt: (te[i], 0, 0))
    grid_spec = pltpu.PrefetchScalarGridSpec(
        num_scalar_prefetch=2,
        grid=(nt_max,),
        in_specs=[rows] * N_CHUNKS + [mat, vec, mat, vec, mat, vec],
        out_specs=[rows] * N_CHUNKS,
        scratch_shapes=[pltpu.VMEM((D_MODEL, D_MODEL), BF16)] * 3,
    )
    return pl.pallas_call(
        _expert_kernel,
        out_shape=[jax.ShapeDtypeStruct((n_rows, LANES), I32)] * N_CHUNKS,
        grid_spec=grid_spec,
        compiler_params=pltpu.CompilerParams(
            dimension_semantics=("arbitrary",), vmem_limit_bytes=VMEM_LIMIT_BYTES),
        name="expert_ffn",
    )(tile_expert, n_tiles, *xs_chunks, wg, bg, wu, bu, wd, bd)


def _final_kernel(x1_ref, g_ref, w_ref, mod_ref, fg_ref, *rest):
    y_ref = rest[-1]
    wt = w_ref[...]
    acc = None
    for k in range(TOP_K):
        words = jnp.concatenate([g_ref[k * N_CHUNKS + j] for j in range(N_CHUNKS)], axis=1)
        hi, lo = _unpack_words(words)
        term = wt[:, k:k + 1] * jnp.concatenate([hi, lo], axis=1)
        acc = term if acc is None else acc + term
    g2 = mod_ref[0][5:6]
    x2 = x1_ref[...] + g2 * acc
    ms = jnp.mean(x2 * x2, axis=-1, keepdims=True)
    y_ref[...] = x2 * lax.rsqrt(ms + EPS) * fg_ref[...]


def _final(x1, gathered, wt_tok, mod, final_g, *, tl, tiles_per_batch, b0=0, out_rows=None, y_prev=None):
    n_tok = x1.shape[0]
    out_rows = n_tok if out_rows is None else out_rows
    blk0 = b0 * tiles_per_batch
    in_specs = [
        pl.BlockSpec((tl, D_MODEL), lambda i: (i, 0)),
        pl.BlockSpec((TOP_K * N_CHUNKS, tl, LANES), lambda i: (0, i, 0)),
        pl.BlockSpec((tl, 2 * TOP_K), lambda i: (i, 0)),
        pl.BlockSpec((1, 8, D_MODEL), lambda i: (b0 + i // tiles_per_batch, 0, 0)),
        pl.BlockSpec((1, D_MODEL), lambda i: (0, 0)),
    ]
    args = [x1, gathered, wt_tok, mod, final_g.reshape(1, D_MODEL)]
    aliases = {}
    if y_prev is not None:
        in_specs.append(pl.BlockSpec(memory_space=pl.ANY))
        args.append(y_prev)
        aliases = {len(args) - 1: 0}
    return pl.pallas_call(
        _final_kernel,
        out_shape=jax.ShapeDtypeStruct((out_rows, D_MODEL), F32),
        grid=(n_tok // tl,),
        in_specs=in_specs,
        out_specs=pl.BlockSpec((tl, D_MODEL), lambda i: (blk0 + i, 0)),
        input_output_aliases=aliases,
        compiler_params=pltpu.CompilerParams(
            dimension_semantics=("arbitrary",), vmem_limit_bytes=VMEM_LIMIT_BYTES),
        name="combine_final",
    )(*args)


def _route_rows(r, lo, hi):
    return r[:, lo:hi, :].transpose(1, 0, 2).reshape(hi - lo, -1)


def kernel(x_prompt, x_sample, state_pool, state_conv, c_prompt, c_sample, norm1_g, norm2_g, final_g, w_ada, b_ada, w_in, pool_w, pool_scale, conv_dw, conv_b, conv_ln_g, conv_ln_b, conv_w_out, gate_w, gate_b, w_out, router_w, router_b, exp_w_gate, exp_b_gate, exp_w_up, exp_b_up, exp_w_down, exp_b_down):
    assert norm1_g.shape[0] == 1, "single-layer trunk"
    row = lambda v: v.reshape(1, -1)
    mix_w = (
        row(norm1_g[0]), row(norm2_g[0]),
        w_in[0].astype(BF16), pool_w[0].astype(BF16), row(pool_scale[0]),
        jnp.broadcast_to(conv_dw[0][:, None, :], (CONV_WIDTH, SUBLANES, D_CONV)),
        row(conv_b[0]), row(conv_ln_g[0]), row(conv_ln_b[0]),
        conv_w_out[0].astype(BF16), gate_w[0].astype(BF16), row(gate_b[0]),
        w_out[0].astype(BF16), router_w[0].T.astype(BF16),
        jnp.broadcast_to(router_b[0][:, None], (N_EXPERTS, LANES)),
    )
    ew = (
        exp_w_gate[0], exp_b_gate[0][:, None, :],
        exp_w_up[0], exp_b_up[0][:, None, :],
        exp_w_down[0], exp_b_down[0][:, None, :],
    )
    bp, lp, _ = x_prompt.shape
    bs, ls, _ = x_sample.shape
    t_p = bp * lp
    tl_p, tl_s = 512, ls
    assert bp % PROMPT_CHUNKS == 0
    nb = bp // PROMPT_CHUNKS

    mod_p = _modulation(c_prompt, w_ada[0], b_ada[0])
    mod_s = _modulation(c_sample, w_ada[0], b_ada[0])
    pad_state = lambda s, hist: jnp.pad(s, ((0, 0), (hist - s.shape[1], 0), (0, 0)))
    split = lambda out: (out[0], out[1:1 + N_CHUNKS], *out[1 + N_CHUNKS:])

    mixed = [split(_token_mix(
        x_prompt, mod_p, jnp.zeros((nb, POOL_HIST, D_POOL), F32), jnp.zeros((nb, CONV_HIST, D_CONV), F32),
        jnp.zeros((N_EXPERTS, LANES), F32), mix_w, b0=c * nb, tl=tl_p, th=tl_p, pos0=0, rc=32))
        for c in range(PROMPT_CHUNKS)]
    x1_s, hp_s, ri_s, rw_s, npool_s, nconv_s, cnt_last = split(_token_mix(
        x_sample, mod_s, pad_state(state_pool[0], POOL_HIST), pad_state(state_conv[0], CONV_HIST),
        mixed[-1][6], mix_w, b0=0, tl=tl_s, th=tl_s, pos0=PAST_LEN, rc=32))

    gathered = []
    for c, (x1_c, hp_c, ri_c, rw_c, _, _, cnt_c) in enumerate(mixed):
        last = c == PROMPT_CHUNKS - 1
        n_tok = nb * lp + (bs * ls if last else 0)
        n_rows = (TOP_K * n_tok // MOE_TILE + N_EXPERTS) * MOE_TILE
        counts = (cnt_last if last else cnt_c)[:, 0].astype(I32)
        tiles_e = (counts + MOE_TILE - 1) // MOE_TILE
        tile_end = jnp.cumsum(tiles_e)
        row_off = (tile_end - tiles_e) * MOE_TILE
        tile_expert = jnp.minimum(
            jnp.sum(jnp.arange(n_rows // MOE_TILE, dtype=I32)[:, None] >= tile_end[None, :], axis=1),
            N_EXPERTS - 1).astype(I32)
        n_tiles = tile_end[-1:].astype(I32)

        def slots(ri, row_off=row_off):
            ids, ranks = _route_rows(ri, 0, TOP_K), _route_rows(ri, TOP_K, 2 * TOP_K)
            pos = ranks
            for e in range(N_EXPERTS):
                pos = pos + jnp.where(ids == e, row_off[e], 0)
            return [pos[k] for k in range(TOP_K)]

        h_groups = [hp_c] + ([hp_s] if last else [])
        pos_groups = [slots(ri_c)] + ([slots(ri_s)] if last else [])
        xs = _dispatch(h_groups, pos_groups, n_rows)
        ys = _expert_ffn(xs, tile_expert, n_tiles, ew)
        gathered.append(_combine_gather(ys, pos_groups))

    y_p = None
    for c, (x1_c, _, _, rw_c, _, _, _) in enumerate(mixed):
        y_p = _final(x1_c, gathered[c][0], _route_rows(rw_c, 0, 2 * TOP_K).T, mod_p, final_g, tl=tl_p,
                     tiles_per_batch=lp // tl_p, b0=c * nb, out_rows=t_p, y_prev=y_p)
    y_s = _final(x1_s, gathered[-1][1], _route_rows(rw_s, 0, 2 * TOP_K).T, mod_s, final_g, tl=tl_s,
                 tiles_per_batch=1)
    unpad = lambda s, n: s[:, s.shape[1] - n:][None]
    npool_p = jnp.concatenate([m[4] for m in mixed], axis=0)
    nconv_p = jnp.concatenate([m[5] for m in mixed], axis=0)
    return (y_p.reshape(bp, lp, D_MODEL), y_s.reshape(bs, ls, D_MODEL),
            unpad(npool_p, POOL_PAD), unpad(nconv_p, CONV_PAD),
            unpad(npool_s, POOL_PAD), unpad(nconv_s, CONV_PAD))
```

```python
import functools

import jax
import jax.numpy as jnp
from jax import lax
from jax.experimental import pallas as pl
from jax.experimental.pallas import tpu as pltpu
from jax.experimental.pallas import tpu_sc as plsc

D_MODEL = 1024
D_POOL = 512
D_CONV = 512
POOL_WINDOWS = (2, 4, 8, 16)
POOL_GW = 128
POOL_OUT_GW = 256
POOL_PAD = 15
CONV_WIDTH = 31
CONV_PAD = 30
N_EXPERTS = 32
TOP_K = 4
SWIGLU_LIMIT = 7.0
SWIGLU_ALPHA = 1.702
EPS = 1e-6
PAST_LEN = 2048

POOL_HIST = 16
CONV_HIST = 32
VMEM_LIMIT_BYTES = 56 * 1024 * 1024

LANES = 128
SUBLANES = 8
D_WORDS = D_MODEL // 2
N_CHUNKS = D_WORDS // LANES
MOE_TILE = 512
PROMPT_CHUNKS = 2
SC_CORES = 2
SC_SUBCORES = 16
SC_WORKERS = SC_CORES * SC_SUBCORES
SC_ROWS = 128

F32 = jnp.float32
BF16 = jnp.bfloat16
I32 = jnp.int32
HI_MASK = -65536
NEG_LOG2E = -1.4426950408889634


def _sigmoid(v):
    return 1.0 / (1.0 + jnp.exp2(v * NEG_LOG2E))


def _pack_words(v):
    r = v.astype(BF16).astype(F32)
    hi = lax.bitcast_convert_type(r[:, :D_WORDS], I32)
    lo = lax.bitcast_convert_type(r[:, D_WORDS:], I32)
    return (hi & HI_MASK) | lax.shift_right_logical(lo, 16)


def _unpack_words(w):
    hi = lax.bitcast_convert_type(w & HI_MASK, F32)
    lo = lax.bitcast_convert_type(lax.shift_left(w, 16), F32)
    return hi, lo


def _mod_kernel(c_ref, w_ref, b_ref, o_ref):
    c = c_ref[...]
    s = (c * _sigmoid(c)).astype(BF16)
    o_ref[...] = jnp.dot(s, w_ref[...].astype(BF16), preferred_element_type=F32) + b_ref[...]


def _modulation(c, w_ada, b_ada):
    bsz = c.shape[0]
    out = pl.pallas_call(
        _mod_kernel,
        out_shape=jax.ShapeDtypeStruct((bsz, 6 * D_MODEL), F32),
        grid=(6,),
        in_specs=[
            pl.BlockSpec((bsz, D_MODEL), lambda j: (0, 0)),
            pl.BlockSpec((D_MODEL, D_MODEL), lambda j: (0, j)),
            pl.BlockSpec((1, D_MODEL), lambda j: (0, j)),
        ],
        out_specs=pl.BlockSpec((bsz, D_MODEL), lambda j: (0, j)),
        compiler_params=pltpu.CompilerParams(
            dimension_semantics=("arbitrary",), vmem_limit_bytes=VMEM_LIMIT_BYTES),
        name="adaln_mod",
    )(c, w_ada, b_ada.reshape(1, 6 * D_MODEL))
    out = out.reshape(bsz, 6, D_MODEL)
    return jnp.pad(out, ((0, 0), (0, 2), (0, 0)))


def _mix_kernel(x_ref, mod_ref, pinit_ref, cinit_ref, cntin_ref, tri_ref,
                n1_ref, n2_ref, win_ref, pw_ref, ps_ref, dw_ref, cb_ref, lg_ref, lb_ref,
                cwo_ref, gw_ref, gb_ref, wo_ref, rwt_ref, rb_ref,
                x1_ref, hp0_ref, hp1_ref, hp2_ref, hp3_ref, ri_ref, rw_ref,
                npool_ref, nconv_ref, cntout_ref,
                conv_sh, pool_in, hbuf, merged, ybuf, sy, cnt_run, *, tl, th, pos0, rc):
    b = pl.program_id(0)
    l = pl.program_id(1)

    conv_in = conv_sh.at[0]

    @pl.when(l == 0)
    def _():
        conv_in[0:CONV_HIST, :] = cinit_ref[0]
        pool_in[0:POOL_HIST, :] = pinit_ref[0]

    @pl.when((b == 0) & (l == 0))
    def _():
        cnt_run[...] = cntin_ref[...]

    mod = mod_ref[0]
    sh1, sc1, g1 = mod[0:1], mod[1:2], mod[2:3]
    sh2, sc2 = mod[3:4], mod[4:5]
    scale1 = n1_ref[...] * (1.0 + sc1)
    scale2 = n2_ref[...] * (1.0 + sc2)

    for r0 in range(0, tl, th):
        rows = slice(r0, r0 + th)
        x = x_ref[rows, :]
        ms = jnp.mean(x * x, axis=-1, keepdims=True)
        hbuf[rows, :] = ((x * lax.rsqrt(ms + EPS)) * scale1 + sh1).astype(BF16)

        proj = jnp.dot(hbuf[rows, :], win_ref[...], preferred_element_type=F32)
        glu = proj[:, D_POOL:D_POOL + D_CONV] * _sigmoid(proj[:, D_POOL + D_CONV:])
        pool_in[POOL_HIST + r0:POOL_HIST + r0 + th, :] = proj[:, :D_POOL]
        conv_in[CONV_HIST + r0:CONV_HIST + r0 + th, :] = glu

        pos = pos0 + l * tl + r0 + lax.broadcasted_iota(I32, (th, POOL_GW), 0)
        pooled = []
        for gi, w in enumerate(POOL_WINDOWS):
            lanes = slice(gi * POOL_GW, (gi + 1) * POOL_GW)
            s = pool_in[r0:r0 + POOL_HIST + th, lanes]
            d = 1
            while d < w:
                s = s + pltpu.roll(s, d, axis=0)
                d *= 2
            cur = pool_in[POOL_HIST + r0:POOL_HIST + r0 + th, lanes]
            cnt = jnp.minimum(pos + 1, w).astype(F32)
            pooled.append((s[POOL_HIST:] / cnt - cur).astype(BF16))

        sh_lo = 0 if r0 == 0 else r0 + CONV_HIST - SUBLANES
        sh_hi = r0 + th + CONV_HIST - SUBLANES
        for r in range(1, SUBLANES):
            conv_sh[r, sh_lo:sh_hi, :] = conv_in[sh_lo + r:sh_hi + r, :]

        for base in range(r0, r0 + th, rc):
            acc = jnp.broadcast_to(cb_ref[...], (rc // SUBLANES, SUBLANES, D_CONV))
            for k in range(CONV_WIDTH):
                q, r = divmod(k + CONV_HIST - CONV_PAD, SUBLANES)
                start = base + q * SUBLANES
                win = conv_sh[r, start:start + rc, :].reshape(rc // SUBLANES, SUBLANES, D_CONV)
                acc = acc + win * dw_ref[k]
            ybuf[base:base + rc, :] = acc.reshape(rc, D_CONV)

        yc = ybuf[rows, :]
        mu = jnp.mean(yc, axis=-1, keepdims=True)
        dev = yc - mu
        var = jnp.mean(dev * dev, axis=-1, keepdims=True)
        yn = dev * lax.rsqrt(var + EPS) * lg_ref[...] + lb_ref[...]
        sy[rows, :] = (yn * _sigmoid(yn)).astype(BF16)

        for j in range(len(POOL_WINDOWS)):
            cs = slice(j * POOL_OUT_GW, (j + 1) * POOL_OUT_GW)
            cs2 = slice(D_MODEL + j * POOL_OUT_GW, D_MODEL + (j + 1) * POOL_OUT_GW)
            ga = _sigmoid(jnp.dot(hbuf[rows, :], gw_ref[:, cs], preferred_element_type=F32) + gb_ref[:, cs])
            gb = _sigmoid(jnp.dot(hbuf[rows, :], gw_ref[:, cs2], preferred_element_type=F32) + gb_ref[:, cs2])
            a_j = jnp.dot(pooled[j], pw_ref[j], preferred_element_type=F32) * ps_ref[:, cs]
            b_j = jnp.dot(sy[rows, :], cwo_ref[:, cs], preferred_element_type=F32)
            merged[rows, cs] = (ga * a_j + gb * b_j).astype(BF16)

        x1 = x + g1 * jnp.dot(merged[rows, :], wo_ref[...], preferred_element_type=F32)
        x1_ref[rows, :] = x1

        ms2 = jnp.mean(x1 * x1, axis=-1, keepdims=True)
        h2 = (x1 * lax.rsqrt(ms2 + EPS)) * scale2 + sh2
        h2b = h2.astype(BF16)
        words = _pack_words(h2)
        for j, ref in enumerate((hp0_ref, hp1_ref, hp2_ref, hp3_ref)):
            ref[rows, :] = words[:, j * LANES:(j + 1) * LANES]

        logits = lax.dot_general(rwt_ref[...], h2b, (((1,), (1,)), ((), ())),
                                 preferred_element_type=F32) + rb_ref[:, 0:1]
        e_iota = lax.broadcasted_iota(I32, (N_EXPERTS, th), 0)
        v = logits
        ids, vals = [], []
        for _ in range(TOP_K):
            m = jnp.max(v, axis=0, keepdims=True)
            idx = jnp.min(jnp.where(v == m, e_iota, N_EXPERTS), axis=0, keepdims=True)
            ids.append(idx)
            vals.append(m)
            v = jnp.where(e_iota == idx, -jnp.inf, v)
        ex = [jnp.exp(vk - vals[0]) for vk in vals]
        den = ex[0] + ex[1] + ex[2] + ex[3]

        sel = [e_iota == idx for idx in ids]
        chosen = jnp.where(sel[0] | sel[1] | sel[2] | sel[3], 1.0, 0.0)
        before = jnp.dot(chosen.astype(BF16), tri_ref[...], preferred_element_type=F32)
        rank_all = cnt_run[:, 0:1] + before
        for k in range(TOP_K):
            ri_ref[0, k:k + 1, rows] = ids[k]
            rk = jnp.sum(jnp.where(sel[k], rank_all, 0.0), axis=0, keepdims=True)
            ri_ref[0, TOP_K + k:TOP_K + k + 1, rows] = rk.astype(I32)
            rw_ref[0, k:k + 1, rows] = ex[k] / den
            rw_ref[0, TOP_K + k:TOP_K + k + 1, rows] = jnp.zeros((1, th), F32)
        cnt_run[...] = cnt_run[...] + jnp.sum(chosen, axis=1, keepdims=True)

    npool_ref[0] = pool_in[tl:tl + POOL_HIST, :]
    nconv_ref[0] = conv_in[tl:tl + CONV_HIST, :]
    conv_in[0:CONV_HIST, :] = conv_in[tl:tl + CONV_HIST, :]
    pool_in[0:POOL_HIST, :] = pool_in[tl:tl + POOL_HIST, :]
    cntout_ref[...] = cnt_run[...]


def _token_mix(x, mod, pool_init, conv_init, cnt_in, wts, *, b0, tl, th, pos0, rc):
    _, seq, _ = x.shape
    bsz = pool_init.shape[0]
    assert seq % tl == 0 and tl % th == 0 and th % rc == 0 and th >= CONV_HIST
    assert th % LANES == 0 or th == tl
    rows_out = bsz * seq
    nl = seq // tl
    x2d = x.reshape(-1, D_MODEL)
    tri = jnp.triu(jnp.ones((th, th), BF16), k=1)
    full = lambda shape: pl.BlockSpec(shape, lambda b, l: (0,) * len(shape))
    per_b = lambda shape: pl.BlockSpec(shape, lambda b, l: (b,) + (0,) * (len(shape) - 1))
    tok = lambda width: pl.BlockSpec((tl, width), lambda b, l: (b * nl + l, 0))
    rt = lambda shape: pl.BlockSpec(shape, lambda b, l: (b * nl + l, 0, 0))
    in_specs = [
        pl.BlockSpec((tl, D_MODEL), lambda b, l: ((b0 + b) * nl + l, 0)),
        pl.BlockSpec((1, 8, D_MODEL), lambda b, l: (b0 + b, 0, 0)),
        per_b((1, POOL_HIST, D_POOL)),
        per_b((1, CONV_HIST, D_CONV)),
        full((N_EXPERTS, LANES)),
        full((th, th)),
        full((1, D_MODEL)), full((1, D_MODEL)),
        full((D_MODEL, D_POOL + 2 * D_CONV)),
        full((len(POOL_WINDOWS), POOL_GW, POOL_OUT_GW)),
        full((1, D_MODEL)),
        full((CONV_WIDTH, SUBLANES, D_CONV)),
        full((1, D_CONV)), full((1, D_CONV)), full((1, D_CONV)),
        full((D_CONV, D_MODEL)),
        full((D_MODEL, 2 * D_MODEL)),
        full((1, 2 * D_MODEL)),
        full((D_MODEL, D_MODEL)),
        full((N_EXPERTS, D_MODEL)),
        full((N_EXPERTS, LANES)),
    ]
    out_shape = (
        jax.ShapeDtypeStruct((rows_out, D_MODEL), F32),
        *[jax.ShapeDtypeStruct((rows_out, LANES), I32)] * N_CHUNKS,
        jax.ShapeDtypeStruct((bsz * nl, 2 * TOP_K, tl), I32),
        jax.ShapeDtypeStruct((bsz * nl, 2 * TOP_K, tl), F32),
        jax.ShapeDtypeStruct((bsz, POOL_HIST, D_POOL), F32),
        jax.ShapeDtypeStruct((bsz, CONV_HIST, D_CONV), F32),
        jax.ShapeDtypeStruct((N_EXPERTS, LANES), F32),
    )
    out_specs = (
        tok(D_MODEL),
        *[tok(LANES)] * N_CHUNKS,
        rt((1, 2 * TOP_K, tl)),
        rt((1, 2 * TOP_K, tl)),
        per_b((1, POOL_HIST, D_POOL)),
        per_b((1, CONV_HIST, D_CONV)),
        full((N_EXPERTS, LANES)),
    )
    scratch = [
        pltpu.VMEM((SUBLANES, tl + CONV_HIST, D_CONV), F32),
        pltpu.VMEM((tl + POOL_HIST, D_POOL), F32),
        pltpu.VMEM((tl, D_MODEL), BF16),
        pltpu.VMEM((tl, D_MODEL), BF16),
        pltpu.VMEM((tl, D_CONV), F32),
        pltpu.VMEM((tl, D_CONV), BF16),
        pltpu.VMEM((N_EXPERTS, LANES), F32),
    ]
    return pl.pallas_call(
        functools.partial(_mix_kernel, tl=tl, th=th, pos0=pos0, rc=rc),
        out_shape=out_shape,
        grid=(bsz, nl),
        in_specs=in_specs,
        out_specs=out_specs,
        scratch_shapes=scratch,
        compiler_params=pltpu.CompilerParams(
            dimension_semantics=("arbitrary", "arbitrary"), vmem_limit_bytes=VMEM_LIMIT_BYTES),
        name="token_mix",
    )(x2d, mod, pool_init, conv_init, cnt_in, tri, *wts)


def _sc_mesh():
    return plsc.VectorSubcoreMesh(core_axis_name="c", subcore_axis_name="s",
                                  num_cores=SC_CORES, num_subcores=SC_SUBCORES)


def _for_each_chunk(n_tok, fn):
    wid = lax.axis_index("s") * SC_CORES + lax.axis_index("c")
    if n_tok % (8 * SC_WORKERS) == 0 and n_tok // SC_WORKERS >= SC_ROWS:
        per_w = n_tok // SC_WORKERS

        @pl.loop(0, pl.cdiv(per_w, SC_ROWS))
        def _(c):
            fn(pl.multiple_of(wid * per_w + jnp.minimum(c * SC_ROWS, per_w - SC_ROWS), 8))
    else:
        assert n_tok % SC_ROWS == 0 and n_tok // SC_ROWS <= SC_WORKERS

        @pl.when(wid < n_tok // SC_ROWS)
        def _():
            fn(pl.multiple_of(wid * SC_ROWS, SC_ROWS))


_SC_SCRATCH = ([pltpu.VMEM((SC_ROWS, LANES), I32)] * N_CHUNKS
               + [pltpu.VMEM((SC_ROWS,), I32)] * TOP_K
               + [pltpu.SemaphoreType.DMA, pltpu.SemaphoreType.DMA])


def _dispatch(h_groups, pos_groups, n_rows):
    n_g = len(h_groups)

    def body(*refs):
        refs = list(refs)
        h = [[refs.pop(0) for _ in range(N_CHUNKS)] for _ in range(n_g)]
        p = [[refs.pop(0) for _ in range(TOP_K)] for _ in range(n_g)]
        o = [refs.pop(0) for _ in range(N_CHUNKS)]
        rows = [refs.pop(0) for _ in range(N_CHUNKS)]
        idx = [refs.pop(0) for _ in range(TOP_K)]
        sem_in, sem_out = refs

        for hg, pg in zip(h, p):
            def move(s, hg=hg, pg=pg):
                loads = [pltpu.async_copy(hg[j].at[pl.ds(s, SC_ROWS)], rows[j], sem_in)
                         for j in range(N_CHUNKS)]
                loads += [pltpu.async_copy(pg[k].at[pl.ds(s, SC_ROWS)], idx[k], sem_in)
                          for k in range(TOP_K)]
                for cp in loads:
                    cp.wait()
                stores = [pltpu.async_copy(rows[j], o[j].at[idx[k]], sem_out)
                          for j in range(N_CHUNKS) for k in range(TOP_K)]
                for cp in stores:
                    cp.wait()

            _for_each_chunk(hg[0].shape[0], move)

    call = pl.kernel(
        body,
        out_type=tuple(jax.ShapeDtypeStruct((n_rows, LANES), I32) for _ in range(N_CHUNKS)),
        mesh=_sc_mesh(), scratch_types=_SC_SCRATCH, name="sc_dispatch")
    flat = [a for g in h_groups for a in g] + [a for g in pos_groups for a in g]
    return call(*flat)


def _combine_gather(y_chunks, pos_groups):
    n_g = len(pos_groups)

    def body(*refs):
        refs = list(refs)
        y = [refs.pop(0) for _ in range(N_CHUNKS)]
        p = [[refs.pop(0) for _ in range(TOP_K)] for _ in range(n_g)]
        g = [refs.pop(0) for _ in range(n_g)]
        rows = [refs.pop(0) for _ in range(N_CHUNKS)]
        idx = [refs.pop(0) for _ in range(TOP_K)]
        sem_in, sem_out = refs

        for pg, gg in zip(p, g):
            def move(s, pg=pg, gg=gg):
                loads = [pltpu.async_copy(pg[k].at[pl.ds(s, SC_ROWS)], idx[k], sem_in)
                         for k in range(TOP_K)]
                for cp in loads:
                    cp.wait()
                for k in range(TOP_K):
                    gathers = [pltpu.async_copy(y[j].at[idx[k]], rows[j], sem_in)
                               for j in range(N_CHUNKS)]
                    for cp in gathers:
                        cp.wait()
                    stores = [pltpu.async_copy(rows[j], gg.at[k * N_CHUNKS + j, pl.ds(s, SC_ROWS)], sem_out)
                              for j in range(N_CHUNKS)]
                    for cp in stores:
                        cp.wait()

            _for_each_chunk(pg[0].shape[0], move)

    call = pl.kernel(
        body,
        out_type=tuple(jax.ShapeDtypeStruct((TOP_K * N_CHUNKS, pg[0].shape[0], LANES), I32)
                       for pg in pos_groups),
        mesh=_sc_mesh(), scratch_types=_SC_SCRATCH, name="sc_combine_gather")
    return call(*y_chunks, *[a for g in pos_groups for a in g])


def _expert_kernel(te_ref, nt_ref, x0_ref, x1_ref, x2_ref, x3_ref,
                   wg_ref, bg_ref, wu_ref, bu_ref, wd_ref, bd_ref,
                   y0_ref, y1_ref, y2_ref, y3_ref, wg_bf, wu_bf, wd_bf):
    i = pl.program_id(0)

    @pl.when((i == 0) | (te_ref[i] != te_ref[jnp.maximum(i - 1, 0)]))
    def _():
        wg_bf[...] = wg_ref[0].astype(BF16)
        wu_bf[...] = wu_ref[0].astype(BF16)
        wd_bf[...] = wd_ref[0].astype(BF16)

    @pl.when(i < nt_ref[0])
    def _():
        words = jnp.concatenate([x0_ref[...], x1_ref[...], x2_ref[...], x3_ref[...]], axis=1)
        hi, lo = _unpack_words(words)
        xt = jnp.concatenate([hi, lo], axis=1).astype(BF16)
        gt = jnp.minimum(jnp.dot(xt, wg_bf[...], preferred_element_type=F32) + bg_ref[0], SWIGLU_LIMIT)
        up = jnp.clip(jnp.dot(xt, wu_bf[...], preferred_element_type=F32) + bu_ref[0],
                      -SWIGLU_LIMIT, SWIGLU_LIMIT)
        act = gt * _sigmoid(SWIGLU_ALPHA * gt) * (up + 1.0)
        y = jnp.dot(act.astype(BF16), wd_bf[...], preferred_element_type=F32) + bd_ref[0]
        out = _pack_words(y)
        for j, ref in enumerate((y0_ref, y1_ref, y2_ref, y3_ref)):
            ref[...] = out[:, j * LANES:(j + 1) * LANES]


def _expert_ffn(xs_chunks, tile_expert, n_tiles, ew):
    n_rows = xs_chunks[0].shape[0]
    nt_max = n_rows // MOE_TILE
    wg, bg, wu, bu, wd, bd = ew
    rows = pl.BlockSpec((MOE_TILE, LANES), lambda i, te, nt: (i, 0))
    mat = pl.BlockSpec((1, D_MODEL, D_MODEL), lambda i, te, nt: (te[i], 0, 0))
    vec = pl.BlockSpec((1, 1, D_MODEL), lambda i, te, nt: (te[i], 0, 0))
    grid_spec = pltpu.PrefetchScalarGridSpec(
        num_scalar_prefetch=2,
        grid=(nt_max,),
        in_specs=[rows] * N_CHUNKS + [mat, vec, mat, vec, mat, vec],
        out_specs=[rows] * N_CHUNKS,
        scratch_shapes=[pltpu.VMEM((D_MODEL, D_MODEL), BF16)] * 3,
    )
    return pl.pallas_call(
        _expert_kernel,
        out_shape=[jax.ShapeDtypeStruct((n_rows, LANES), I32)] * N_CHUNKS,
        grid_spec=grid_spec,
        compiler_params=pltpu.CompilerParams(
            dimension_semantics=("arbitrary",), vmem_limit_bytes=VMEM_LIMIT_BYTES),
        name="expert_ffn",
    )(tile_expert, n_tiles, *xs_chunks, wg, bg, wu, bu, wd, bd)


def _final_kernel(x1_ref, g_ref, w_ref, mod_ref, fg_ref, *rest):
    y_ref = rest[-1]
    wt = w_ref[...]
    acc = None
    for k in range(TOP_K):
        words = jnp.concatenate([g_ref[k * N_CHUNKS + j] for j in range(N_CHUNKS)], axis=1)
        hi, lo = _unpack_words(words)
        term = wt[:, k:k + 1] * jnp.concatenate([hi, lo], axis=1)
        acc = term if acc is None else acc + term
    g2 = mod_ref[0][5:6]
    x2 = x1_ref[...] + g2 * acc
    ms = jnp.mean(x2 * x2, axis=-1, keepdims=True)
    y_ref[...] = x2 * lax.rsqrt(ms + EPS) * fg_ref[...]


def _final(x1, gathered, wt_tok, mod, final_g, *, tl, tiles_per_batch, b0=0, out_rows=None, y_prev=None):
    n_tok = x1.shape[0]
    out_rows = n_tok if out_rows is None else out_rows
    blk0 = b0 * tiles_per_batch
    in_specs = [
        pl.BlockSpec((tl, D_MODEL), lambda i: (i, 0)),
        pl.BlockSpec((TOP_K * N_CHUNKS, tl, LANES), lambda i: (0, i, 0)),
        pl.BlockSpec((tl, 2 * TOP_K), lambda i: (i, 0)),
        pl.BlockSpec((1, 8, D_MODEL), lambda i: (b0 + i // tiles_per_batch, 0, 0)),
        pl.BlockSpec((1, D_MODEL), lambda i: (0, 0)),
    ]
    args = [x1, gathered, wt_tok, mod, final_g.reshape(1, D_MODEL)]
    aliases = {}
    if y_prev is not None:
        in_specs.append(pl.BlockSpec(memory_space=pl.ANY))
        args.append(y_prev)
        aliases = {len(args) - 1: 0}
    return pl.pallas_call(
        _final_kernel,
        out_shape=jax.ShapeDtypeStruct((out_rows, D_MODEL), F32),
        grid=(n_tok // tl,),
        in_specs=in_specs,
        out_specs=pl.BlockSpec((tl, D_MODEL), lambda i: (blk0 + i, 0)),
        input_output_aliases=aliases,
        compiler_params=pltpu.CompilerParams(
            dimension_semantics=("arbitrary",), vmem_limit_bytes=VMEM_LIMIT_BYTES),
        name="combine_final",
    )(*args)


def _route_rows(r, lo, hi):
    return r[:, lo:hi, :].transpose(1, 0, 2).reshape(hi - lo, -1)


def kernel(x_prompt, x_sample, state_pool, state_conv, c_prompt, c_sample, norm1_g, norm2_g, final_g, w_ada, b_ada, w_in, pool_w, pool_scale, conv_dw, conv_b, conv_ln_g, conv_ln_b, conv_w_out, gate_w, gate_b, w_out, router_w, router_b, exp_w_gate, exp_b_gate, exp_w_up, exp_b_up, exp_w_down, exp_b_down):
    assert norm1_g.shape[0] == 1, "single-layer trunk"
    row = lambda v: v.reshape(1, -1)
    mix_w = (
        row(norm1_g[0]), row(norm2_g[0]),
        w_in[0].astype(BF16), pool_w[0].astype(BF16), row(pool_scale[0]),
        jnp.broadcast_to(conv_dw[0][:, None, :], (CONV_WIDTH, SUBLANES, D_CONV)),
        row(conv_b[0]), row(conv_ln_g[0]), row(conv_ln_b[0]),
        conv_w_out[0].astype(BF16), gate_w[0].astype(BF16), row(gate_b[0]),
        w_out[0].astype(BF16), router_w[0].T.astype(BF16),
        jnp.broadcast_to(router_b[0][:, None], (N_EXPERTS, LANES)),
    )
    ew = (
        exp_w_gate[0], exp_b_gate[0][:, None, :],
        exp_w_up[0], exp_b_up[0][:, None, :],
        exp_w_down[0], exp_b_down[0][:, None, :],
    )
    bp, lp, _ = x_prompt.shape
    bs, ls, _ = x_sample.shape
    t_p = bp * lp
    tl_p, tl_s = 512, ls
    assert bp % PROMPT_CHUNKS == 0
    nb = bp // PROMPT_CHUNKS

    mod_p = _modulation(c_prompt, w_ada[0], b_ada[0])
    mod_s = _modulation(c_sample, w_ada[0], b_ada[0])
    pad_state = lambda s, hist: jnp.pad(s, ((0, 0), (hist - s.shape[1], 0), (0, 0)))
    split = lambda out: (out[0], out[1:1 + N_CHUNKS], *out[1 + N_CHUNKS:])

    mixed = [split(_token_mix(
        x_prompt, mod_p, jnp.zeros((nb, POOL_HIST, D_POOL), F32), jnp.zeros((nb, CONV_HIST, D_CONV), F32),
        jnp.zeros((N_EXPERTS, LANES), F32), mix_w, b0=c * nb, tl=tl_p, th=tl_p, pos0=0, rc=32))
        for c in range(PROMPT_CHUNKS)]
    x1_s, hp_s, ri_s, rw_s, npool_s, nconv_s, cnt_last = split(_token_mix(
        x_sample, mod_s, pad_state(state_pool[0], POOL_HIST), pad_state(state_conv[0], CONV_HIST),
        mixed[-1][6], mix_w, b0=0, tl=tl_s, th=tl_s, pos0=PAST_LEN, rc=32))

    def finish(c, g_c, y_prev):
        x1_c, rw_c = mixed[c][0], mixed[c][3]
        return _final(x1_c, g_c, _route_rows(rw_c, 0, 2 * TOP_K).T, mod_p, final_g, tl=tl_p,
                      tiles_per_batch=lp // tl_p, b0=c * nb, out_rows=t_p, y_prev=y_prev)

    y_p = None
    gathered = []
    for c, (x1_c, hp_c, ri_c, rw_c, _, _, cnt_c) in enumerate(mixed):
        last = c == PROMPT_CHUNKS - 1
        n_tok = nb * lp + (bs * ls if last else 0)
        n_rows = (TOP_K * n_tok // MOE_TILE + N_EXPERTS) * MOE_TILE
        counts = (cnt_last if last else cnt_c)[:, 0].astype(I32)
        tiles_e = (counts + MOE_TILE - 1) // MOE_TILE
        tile_end = jnp.cumsum(tiles_e)
        row_off = (tile_end - tiles_e) * MOE_TILE
        tile_expert = jnp.minimum(
            jnp.sum(jnp.arange(n_rows // MOE_TILE, dtype=I32)[:, None] >= tile_end[None, :], axis=1),
            N_EXPERTS - 1).astype(I32)
        n_tiles = tile_end[-1:].astype(I32)

        def slots(ri, row_off=row_off):
            ids, ranks = _route_rows(ri, 0, TOP_K), _route_rows(ri, TOP_K, 2 * TOP_K)
            pos = ranks
            for e in range(N_EXPERTS):
                pos = pos + jnp.where(ids == e, row_off[e], 0)
            return [pos[k] for k in range(TOP_K)]

        h_groups = [hp_c] + ([hp_s] if last else [])
        pos_groups = [slots(ri_c)] + ([slots(ri_s)] if last else [])
        xs = _dispatch(h_groups, pos_groups, n_rows)
        ys = _expert_ffn(xs, tile_expert, n_tiles, ew)
        if c > 0:
            y_p = finish(c - 1, gathered[c - 1][0], y_p)
        gathered.append(_combine_gather(ys, pos_groups))

    y_p = finish(PROMPT_CHUNKS - 1, gathered[-1][0], y_p)
    y_s = _final(x1_s, gathered[-1][1], _route_rows(rw_s, 0, 2 * TOP_K).T, mod_s, final_g, tl=tl_s,
                 tiles_per_batch=1)
    unpad = lambda s, n: s[:, s.shape[1] - n:][None]
    npool_p = jnp.concatenate([m[4] for m in mixed], axis=0)
    nconv_p = jnp.concatenate([m[5] for m in mixed], axis=0)
    return (y_p.reshape(bp, lp, D_MODEL), y_s.reshape(bs, ls, D_MODEL),
            unpad(npool_p, POOL_PAD), unpad(nconv_p, CONV_PAD),
            unpad(npool_s, POOL_PAD), unpad(nconv_s, CONV_PAD))
```

```python
import functools

import jax
import jax.numpy as jnp
from jax import lax
from jax.experimental import pallas as pl
from jax.experimental.pallas import tpu as pltpu
from jax.experimental.pallas import tpu_sc as plsc

D_MODEL = 1024
D_POOL = 512
D_CONV = 512
POOL_WINDOWS = (2, 4, 8, 16)
POOL_GW = 128
POOL_OUT_GW = 256
POOL_PAD = 15
CONV_WIDTH = 31
CONV_PAD = 30
N_EXPERTS = 32
TOP_K = 4
SWIGLU_LIMIT = 7.0
SWIGLU_ALPHA = 1.702
EPS = 1e-6
PAST_LEN = 2048

POOL_HIST = 16
CONV_HIST = 32
VMEM_LIMIT_BYTES = 56 * 1024 * 1024

LANES = 128
SUBLANES = 8
D_WORDS = D_MODEL // 2
N_CHUNKS = D_WORDS // LANES
MOE_TILE = 512
MOE_ROW_STEP = 128
PROMPT_TAIL_FRACTION = 4
SC_CORES = 2
SC_SUBCORES = 16
SC_WORKERS = SC_CORES * SC_SUBCORES
SC_ROWS = 128

F32 = jnp.float32
BF16 = jnp.bfloat16
I32 = jnp.int32
HI_MASK = -65536
NEG_LOG2E = -1.4426950408889634


def _sigmoid(v):
    return 1.0 / (1.0 + jnp.exp2(v * NEG_LOG2E))


def _pack_words(v):
    r = v.astype(BF16).astype(F32)
    hi = lax.bitcast_convert_type(r[:, :D_WORDS], I32)
    lo = lax.bitcast_convert_type(r[:, D_WORDS:], I32)
    return (hi & HI_MASK) | lax.shift_right_logical(lo, 16)


def _unpack_words(w):
    hi = lax.bitcast_convert_type(w & HI_MASK, F32)
    lo = lax.bitcast_convert_type(lax.shift_left(w, 16), F32)
    return hi, lo


def _mod_kernel(c_ref, w_ref, b_ref, o_ref):
    c = c_ref[...]
    s = (c * _sigmoid(c)).astype(BF16)
    o_ref[...] = jnp.dot(s, w_ref[...].astype(BF16), preferred_element_type=F32) + b_ref[...]


def _modulation(c, w_ada, b_ada):
    bsz = c.shape[0]
    out = pl.pallas_call(
        _mod_kernel,
        out_shape=jax.ShapeDtypeStruct((bsz, 6 * D_MODEL), F32),
        grid=(6,),
        in_specs=[
            pl.BlockSpec((bsz, D_MODEL), lambda j: (0, 0)),
            pl.BlockSpec((D_MODEL, D_MODEL), lambda j: (0, j)),
            pl.BlockSpec((1, D_MODEL), lambda j: (0, j)),
        ],
        out_specs=pl.BlockSpec((bsz, D_MODEL), lambda j: (0, j)),
        compiler_params=pltpu.CompilerParams(
            dimension_semantics=("arbitrary",), vmem_limit_bytes=VMEM_LIMIT_BYTES),
        name="adaln_mod",
    )(c, w_ada, b_ada.reshape(1, 6 * D_MODEL))
    out = out.reshape(bsz, 6, D_MODEL)
    return jnp.pad(out, ((0, 0), (0, 2), (0, 0)))


def _mix_kernel(x_ref, mod_ref, pinit_ref, cinit_ref, cntin_ref, tri_ref,
                n1_ref, n2_ref, win_ref, pw_ref, ps_ref, dw_ref, cb_ref, lg_ref, lb_ref,
                cwo_ref, gw_ref, gb_ref, wo_ref, rwt_ref, rb_ref,
                x1_ref, hp0_ref, hp1_ref, hp2_ref, hp3_ref, ri_ref, rw_ref,
                npool_ref, nconv_ref, cntout_ref,
                conv_sh, pool_in, hbuf, merged, ybuf, sy, cnt_run, *, tl, th, pos0, rc):
    b = pl.program_id(0)
    l = pl.program_id(1)

    conv_in = conv_sh.at[0]

    @pl.when(l == 0)
    def _():
        conv_in[0:CONV_HIST, :] = cinit_ref[0]
        pool_in[0:POOL_HIST, :] = pinit_ref[0]

    @pl.when((b == 0) & (l == 0))
    def _():
        cnt_run[...] = cntin_ref[...]

    mod = mod_ref[0]
    sh1, sc1, g1 = mod[0:1], mod[1:2], mod[2:3]
    sh2, sc2 = mod[3:4], mod[4:5]
    scale1 = n1_ref[...] * (1.0 + sc1)
    scale2 = n2_ref[...] * (1.0 + sc2)

    for r0 in range(0, tl, th):
        rows = slice(r0, r0 + th)
        x = x_ref[rows, :]
        ms = jnp.mean(x * x, axis=-1, keepdims=True)
        hbuf[rows, :] = ((x * lax.rsqrt(ms + EPS)) * scale1 + sh1).astype(BF16)

        proj = jnp.dot(hbuf[rows, :], win_ref[...], preferred_element_type=F32)
        glu = proj[:, D_POOL:D_POOL + D_CONV] * _sigmoid(proj[:, D_POOL + D_CONV:])
        pool_in[POOL_HIST + r0:POOL_HIST + r0 + th, :] = proj[:, :D_POOL]
        conv_in[CONV_HIST + r0:CONV_HIST + r0 + th, :] = glu

        pos = pos0 + l * tl + r0 + lax.broadcasted_iota(I32, (th, POOL_GW), 0)
        pooled = []
        for gi, w in enumerate(POOL_WINDOWS):
            lanes = slice(gi * POOL_GW, (gi + 1) * POOL_GW)
            s = pool_in[r0:r0 + POOL_HIST + th, lanes]
            d = 1
            while d < w:
                s = s + pltpu.roll(s, d, axis=0)
                d *= 2
            cur = pool_in[POOL_HIST + r0:POOL_HIST + r0 + th, lanes]
            cnt = jnp.minimum(pos + 1, w).astype(F32)
            pooled.append((s[POOL_HIST:] / cnt - cur).astype(BF16))

        sh_lo = 0 if r0 == 0 else r0 + CONV_HIST - SUBLANES
        sh_hi = r0 + th + CONV_HIST - SUBLANES
        for r in range(1, SUBLANES):
            conv_sh[r, sh_lo:sh_hi, :] = conv_in[sh_lo + r:sh_hi + r, :]

        for base in range(r0, r0 + th, rc):
            acc = jnp.broadcast_to(cb_ref[...], (rc // SUBLANES, SUBLANES, D_CONV))
            for k in range(CONV_WIDTH):
                q, r = divmod(k + CONV_HIST - CONV_PAD, SUBLANES)
                start = base + q * SUBLANES
                win = conv_sh[r, start:start + rc, :].reshape(rc // SUBLANES, SUBLANES, D_CONV)
                acc = acc + win * dw_ref[k]
            ybuf[base:base + rc, :] = acc.reshape(rc, D_CONV)

        yc = ybuf[rows, :]
        mu = jnp.mean(yc, axis=-1, keepdims=True)
        dev = yc - mu
        var = jnp.mean(dev * dev, axis=-1, keepdims=True)
        yn = dev * lax.rsqrt(var + EPS) * lg_ref[...] + lb_ref[...]
        sy[rows, :] = (yn * _sigmoid(yn)).astype(BF16)

        for j in range(len(POOL_WINDOWS)):
            cs = slice(j * POOL_OUT_GW, (j + 1) * POOL_OUT_GW)
            cs2 = slice(D_MODEL + j * POOL_OUT_GW, D_MODEL + (j + 1) * POOL_OUT_GW)
            ga = _sigmoid(jnp.dot(hbuf[rows, :], gw_ref[:, cs], preferred_element_type=F32) + gb_ref[:, cs])
            gb = _sigmoid(jnp.dot(hbuf[rows, :], gw_ref[:, cs2], preferred_element_type=F32) + gb_ref[:, cs2])
            a_j = jnp.dot(pooled[j], pw_ref[j], preferred_element_type=F32) * ps_ref[:, cs]
            b_j = jnp.dot(sy[rows, :], cwo_ref[:, cs], preferred_element_type=F32)
            merged[rows, cs] = (ga * a_j + gb * b_j).astype(BF16)

        x1 = x + g1 * jnp.dot(merged[rows, :], wo_ref[...], preferred_element_type=F32)
        x1_ref[rows, :] = x1

        ms2 = jnp.mean(x1 * x1, axis=-1, keepdims=True)
        h2 = (x1 * lax.rsqrt(ms2 + EPS)) * scale2 + sh2
        h2b = h2.astype(BF16)
        words = _pack_words(h2)
        for j, ref in enumerate((hp0_ref, hp1_ref, hp2_ref, hp3_ref)):
            ref[rows, :] = words[:, j * LANES:(j + 1) * LANES]

        logits = lax.dot_general(rwt_ref[...], h2b, (((1,), (1,)), ((), ())),
                                 preferred_element_type=F32) + rb_ref[:, 0:1]
        e_iota = lax.broadcasted_iota(I32, (N_EXPERTS, th), 0)
        v = logits
        ids, vals = [], []
        for _ in range(TOP_K):
            m = jnp.max(v, axis=0, keepdims=True)
            idx = jnp.min(jnp.where(v == m, e_iota, N_EXPERTS), axis=0, keepdims=True)
            ids.append(idx)
            vals.append(m)
            v = jnp.where(e_iota == idx, -jnp.inf, v)
        ex = [jnp.exp(vk - vals[0]) for vk in vals]
        den = ex[0] + ex[1] + ex[2] + ex[3]

        sel = [e_iota == idx for idx in ids]
        chosen = jnp.where(sel[0] | sel[1] | sel[2] | sel[3], 1.0, 0.0)
        before = jnp.dot(chosen.astype(BF16), tri_ref[...], preferred_element_type=F32)
        rank_all = cnt_run[:, 0:1] + before
        for k in range(TOP_K):
            ri_ref[0, k:k + 1, rows] = ids[k]
            rk = jnp.sum(jnp.where(sel[k], rank_all, 0.0), axis=0, keepdims=True)
            ri_ref[0, TOP_K + k:TOP_K + k + 1, rows] = rk.astype(I32)
            rw_ref[0, k:k + 1, rows] = ex[k] / den
            rw_ref[0, TOP_K + k:TOP_K + k + 1, rows] = jnp.zeros((1, th), F32)
        cnt_run[...] = cnt_run[...] + jnp.sum(chosen, axis=1, keepdims=True)

    npool_ref[0] = pool_in[tl:tl + POOL_HIST, :]
    nconv_ref[0] = conv_in[tl:tl + CONV_HIST, :]
    conv_in[0:CONV_HIST, :] = conv_in[tl:tl + CONV_HIST, :]
    pool_in[0:POOL_HIST, :] = pool_in[tl:tl + POOL_HIST, :]
    cntout_ref[...] = cnt_run[...]


def _token_mix(x, mod, pool_init, conv_init, cnt_in, wts, *, b0, tl, th, pos0, rc):
    _, seq, _ = x.shape
    bsz = pool_init.shape[0]
    assert seq % tl == 0 and tl % th == 0 and th % rc == 0 and th >= CONV_HIST
    assert th % LANES == 0 or th == tl
    rows_out = bsz * seq
    nl = seq // tl
    x2d = x.reshape(-1, D_MODEL)
    tri = jnp.triu(jnp.ones((th, th), BF16), k=1)
    full = lambda shape: pl.BlockSpec(shape, lambda b, l: (0,) * len(shape))
    per_b = lambda shape: pl.BlockSpec(shape, lambda b, l: (b,) + (0,) * (len(shape) - 1))
    tok = lambda width: pl.BlockSpec((tl, width), lambda b, l: (b * nl + l, 0))
    rt = lambda shape: pl.BlockSpec(shape, lambda b, l: (b * nl + l, 0, 0))
    in_specs = [
        pl.BlockSpec((tl, D_MODEL), lambda b, l: ((b0 + b) * nl + l, 0)),
        pl.BlockSpec((1, 8, D_MODEL), lambda b, l: (b0 + b, 0, 0)),
        per_b((1, POOL_HIST, D_POOL)),
        per_b((1, CONV_HIST, D_CONV)),
        full((N_EXPERTS, LANES)),
        full((th, th)),
        full((1, D_MODEL)), full((1, D_MODEL)),
        full((D_MODEL, D_POOL + 2 * D_CONV)),
        full((len(POOL_WINDOWS), POOL_GW, POOL_OUT_GW)),
        full((1, D_MODEL)),
        full((CONV_WIDTH, SUBLANES, D_CONV)),
        full((1, D_CONV)), full((1, D_CONV)), full((1, D_CONV)),
        full((D_CONV, D_MODEL)),
        full((D_MODEL, 2 * D_MODEL)),
        full((1, 2 * D_MODEL)),
        full((D_MODEL, D_MODEL)),
        full((N_EXPERTS, D_MODEL)),
        full((N_EXPERTS, LANES)),
    ]
    out_shape = (
        jax.ShapeDtypeStruct((rows_out, D_MODEL), F32),
        *[jax.ShapeDtypeStruct((rows_out, LANES), I32)] * N_CHUNKS,
        jax.ShapeDtypeStruct((bsz * nl, 2 * TOP_K, tl), I32),
        jax.ShapeDtypeStruct((bsz * nl, 2 * TOP_K, tl), F32),
        jax.ShapeDtypeStruct((bsz, POOL_HIST, D_POOL), F32),
        jax.ShapeDtypeStruct((bsz, CONV_HIST, D_CONV), F32),
        jax.ShapeDtypeStruct((N_EXPERTS, LANES), F32),
    )
    out_specs = (
        tok(D_MODEL),
        *[tok(LANES)] * N_CHUNKS,
        rt((1, 2 * TOP_K, tl)),
        rt((1, 2 * TOP_K, tl)),
        per_b((1, POOL_HIST, D_POOL)),
        per_b((1, CONV_HIST, D_CONV)),
        full((N_EXPERTS, LANES)),
    )
    scratch = [
        pltpu.VMEM((SUBLANES, tl + CONV_HIST, D_CONV), F32),
        pltpu.VMEM((tl + POOL_HIST, D_POOL), F32),
        pltpu.VMEM((tl, D_MODEL), BF16),
        pltpu.VMEM((tl, D_MODEL), BF16),
        pltpu.VMEM((tl, D_CONV), F32),
        pltpu.VMEM((tl, D_CONV), BF16),
        pltpu.VMEM((N_EXPERTS, LANES), F32),
    ]
    return pl.pallas_call(
        functools.partial(_mix_kernel, tl=tl, th=th, pos0=pos0, rc=rc),
        out_shape=out_shape,
        grid=(bsz, nl),
        in_specs=in_specs,
        out_specs=out_specs,
        scratch_shapes=scratch,
        compiler_params=pltpu.CompilerParams(
            dimension_semantics=("arbitrary", "arbitrary"), vmem_limit_bytes=VMEM_LIMIT_BYTES),
        name="token_mix",
    )(x2d, mod, pool_init, conv_init, cnt_in, tri, *wts)


def _sc_mesh():
    return plsc.VectorSubcoreMesh(core_axis_name="c", subcore_axis_name="s",
                                  num_cores=SC_CORES, num_subcores=SC_SUBCORES)


def _for_each_chunk(n_tok, fn):
    wid = lax.axis_index("s") * SC_CORES + lax.axis_index("c")
    if n_tok % (8 * SC_WORKERS) == 0 and n_tok // SC_WORKERS >= SC_ROWS:
        per_w = n_tok // SC_WORKERS

        @pl.loop(0, pl.cdiv(per_w, SC_ROWS))
        def _(c):
            fn(pl.multiple_of(wid * per_w + jnp.minimum(c * SC_ROWS, per_w - SC_ROWS), 8))
    else:
        assert n_tok % SC_ROWS == 0 and n_tok // SC_ROWS <= SC_WORKERS

        @pl.when(wid < n_tok // SC_ROWS)
        def _():
            fn(pl.multiple_of(wid * SC_ROWS, SC_ROWS))


_SC_SCRATCH = ([pltpu.VMEM((SC_ROWS, LANES), I32)] * N_CHUNKS
               + [pltpu.VMEM((SC_ROWS,), I32)] * TOP_K
               + [pltpu.SemaphoreType.DMA, pltpu.SemaphoreType.DMA])


def _dispatch(h_groups, pos_groups, n_rows):
    n_g = len(h_groups)

    def body(*refs):
        refs = list(refs)
        h = [[refs.pop(0) for _ in range(N_CHUNKS)] for _ in range(n_g)]
        p = [[refs.pop(0) for _ in range(TOP_K)] for _ in range(n_g)]
        o = [refs.pop(0) for _ in range(N_CHUNKS)]
        rows = [refs.pop(0) for _ in range(N_CHUNKS)]
        idx = [refs.pop(0) for _ in range(TOP_K)]
        sem_in, sem_out = refs

        for hg, pg in zip(h, p):
            def move(s, hg=hg, pg=pg):
                loads = [pltpu.async_copy(hg[j].at[pl.ds(s, SC_ROWS)], rows[j], sem_in)
                         for j in range(N_CHUNKS)]
                loads += [pltpu.async_copy(pg[k].at[pl.ds(s, SC_ROWS)], idx[k], sem_in)
                          for k in range(TOP_K)]
                for cp in loads:
                    cp.wait()
                stores = [pltpu.async_copy(rows[j], o[j].at[idx[k]], sem_out)
                          for j in range(N_CHUNKS) for k in range(TOP_K)]
                for cp in stores:
                    cp.wait()

            _for_each_chunk(hg[0].shape[0], move)

    call = pl.kernel(
        body,
        out_type=tuple(jax.ShapeDtypeStruct((n_rows, LANES), I32) for _ in range(N_CHUNKS)),
        mesh=_sc_mesh(), scratch_types=_SC_SCRATCH, name="sc_dispatch")
    flat = [a for g in h_groups for a in g] + [a for g in pos_groups for a in g]
    return call(*flat)


def _combine_gather(y_chunks, pos_groups):
    n_g = len(pos_groups)

    def body(*refs):
        refs = list(refs)
        y = [refs.pop(0) for _ in range(N_CHUNKS)]
        p = [[refs.pop(0) for _ in range(TOP_K)] for _ in range(n_g)]
        g = [refs.pop(0) for _ in range(n_g)]
        rows = [refs.pop(0) for _ in range(N_CHUNKS)]
        idx = [refs.pop(0) for _ in range(TOP_K)]
        sem_in, sem_out = refs

        for pg, gg in zip(p, g):
            def move(s, pg=pg, gg=gg):
                loads = [pltpu.async_copy(pg[k].at[pl.ds(s, SC_ROWS)], idx[k], sem_in)
                         for k in range(TOP_K)]
                for cp in loads:
                    cp.wait()
                for k in range(TOP_K):
                    gathers = [pltpu.async_copy(y[j].at[idx[k]], rows[j], sem_in)
                               for j in range(N_CHUNKS)]
                    for cp in gathers:
                        cp.wait()
                    stores = [pltpu.async_copy(rows[j], gg.at[k * N_CHUNKS + j, pl.ds(s, SC_ROWS)], sem_out)
                              for j in range(N_CHUNKS)]
                    for cp in stores:
                        cp.wait()

            _for_each_chunk(pg[0].shape[0], move)

    call = pl.kernel(
        body,
        out_type=tuple(jax.ShapeDtypeStruct((TOP_K * N_CHUNKS, pg[0].shape[0], LANES), I32)
                       for pg in pos_groups),
        mesh=_sc_mesh(), scratch_types=_SC_SCRATCH, name="sc_combine_gather")
    return call(*y_chunks, *[a for g in pos_groups for a in g])


def _expert_kernel(te_ref, tr_ref, x0_ref, x1_ref, x2_ref, x3_ref,
                   wg_ref, bg_ref, wu_ref, bu_ref, wd_ref, bd_ref,
                   y0_ref, y1_ref, y2_ref, y3_ref, wg_bf, wu_bf, wd_bf):
    i = pl.program_id(0)

    @pl.when((i == 0) | (te_ref[i] != te_ref[jnp.maximum(i - 1, 0)]))
    def _():
        wg_bf[...] = wg_ref[0].astype(BF16)
        wu_bf[...] = wu_ref[0].astype(BF16)
        wd_bf[...] = wd_ref[0].astype(BF16)

    def ffn(n):
        words = jnp.concatenate([r[0:n, :] for r in (x0_ref, x1_ref, x2_ref, x3_ref)], axis=1)
        hi, lo = _unpack_words(words)
        xt = jnp.concatenate([hi, lo], axis=1).astype(BF16)
        gt = jnp.minimum(jnp.dot(xt, wg_bf[...], preferred_element_type=F32) + bg_ref[0], SWIGLU_LIMIT)
        up = jnp.clip(jnp.dot(xt, wu_bf[...], preferred_element_type=F32) + bu_ref[0],
                      -SWIGLU_LIMIT, SWIGLU_LIMIT)
        act = gt * _sigmoid(SWIGLU_ALPHA * gt) * (up + 1.0)
        y = jnp.dot(act.astype(BF16), wd_bf[...], preferred_element_type=F32) + bd_ref[0]
        out = _pack_words(y)
        for j, ref in enumerate((y0_ref, y1_ref, y2_ref, y3_ref)):
            ref[0:n, :] = out[:, j * LANES:(j + 1) * LANES]

    valid = tr_ref[i]
    for n in range(MOE_ROW_STEP, MOE_TILE + 1, MOE_ROW_STEP):
        pl.when((valid > n - MOE_ROW_STEP) & (valid <= n))(functools.partial(ffn, n))


def _expert_ffn(xs_chunks, tile_expert, tile_rows, ew):
    n_rows = xs_chunks[0].shape[0]
    nt_max = n_rows // MOE_TILE
    wg, bg, wu, bu, wd, bd = ew
    rows = pl.BlockSpec((MOE_TILE, LANES), lambda i, te, nt: (i, 0))
    mat = pl.BlockSpec((1, D_MODEL, D_MODEL), lambda i, te, nt: (te[i], 0, 0))
    vec = pl.BlockSpec((1, 1, D_MODEL), lambda i, te, nt: (te[i], 0, 0))
    grid_spec = pltpu.PrefetchScalarGridSpec(
        num_scalar_prefetch=2,
        grid=(nt_max,),
        in_specs=[rows] * N_CHUNKS + [mat, vec, mat, vec, mat, vec],
        out_specs=[rows] * N_CHUNKS,
        scratch_shapes=[pltpu.VMEM((D_MODEL, D_MODEL), BF16)] * 3,
    )
    return pl.pallas_call(
        _expert_kernel,
        out_shape=[jax.ShapeDtypeStruct((n_rows, LANES), I32)] * N_CHUNKS,
        grid_spec=grid_spec,
        compiler_params=pltpu.CompilerParams(
            dimension_semantics=("arbitrary",), vmem_limit_bytes=VMEM_LIMIT_BYTES),
        name="expert_ffn",
    )(tile_expert, tile_rows, *xs_chunks, wg, bg, wu, bu, wd, bd)


def _final_kernel(x1_ref, g_ref, w_ref, mod_ref, fg_ref, *rest):
    y_ref = rest[-1]
    wt = w_ref[...]
    acc = None
    for k in range(TOP_K):
        words = jnp.concatenate([g_ref[k * N_CHUNKS + j] for j in range(N_CHUNKS)], axis=1)
        hi, lo = _unpack_words(words)
        term = wt[:, k:k + 1] * jnp.concatenate([hi, lo], axis=1)
        acc = term if acc is None else acc + term
    g2 = mod_ref[0][5:6]
    x2 = x1_ref[...] + g2 * acc
    ms = jnp.mean(x2 * x2, axis=-1, keepdims=True)
    y_ref[...] = x2 * lax.rsqrt(ms + EPS) * fg_ref[...]


def _final(x1, gathered, wt_tok, mod, final_g, *, tl, tiles_per_batch, b0=0, out_rows=None, y_prev=None):
    n_tok = x1.shape[0]
    out_rows = n_tok if out_rows is None else out_rows
    blk0 = b0 * tiles_per_batch
    in_specs = [
        pl.BlockSpec((tl, D_MODEL), lambda i: (i, 0)),
        pl.BlockSpec((TOP_K * N_CHUNKS, tl, LANES), lambda i: (0, i, 0)),
        pl.BlockSpec((tl, 2 * TOP_K), lambda i: (i, 0)),
        pl.BlockSpec((1, 8, D_MODEL), lambda i: (b0 + i // tiles_per_batch, 0, 0)),
        pl.BlockSpec((1, D_MODEL), lambda i: (0, 0)),
    ]
    args = [x1, gathered, wt_tok, mod, final_g.reshape(1, D_MODEL)]
    aliases = {}
    if y_prev is not None:
        in_specs.append(pl.BlockSpec(memory_space=pl.ANY))
        args.append(y_prev)
        aliases = {len(args) - 1: 0}
    return pl.pallas_call(
        _final_kernel,
        out_shape=jax.ShapeDtypeStruct((out_rows, D_MODEL), F32),
        grid=(n_tok // tl,),
        in_specs=in_specs,
        out_specs=pl.BlockSpec((tl, D_MODEL), lambda i: (blk0 + i, 0)),
        input_output_aliases=aliases,
        compiler_params=pltpu.CompilerParams(
            dimension_semantics=("arbitrary",), vmem_limit_bytes=VMEM_LIMIT_BYTES),
        name="combine_final",
    )(*args)


def _route_rows(r, lo, hi):
    return r[:, lo:hi, :].transpose(1, 0, 2).reshape(hi - lo, -1)


def kernel(x_prompt, x_sample, state_pool, state_conv, c_prompt, c_sample, norm1_g, norm2_g, final_g, w_ada, b_ada, w_in, pool_w, pool_scale, conv_dw, conv_b, conv_ln_g, conv_ln_b, conv_w_out, gate_w, gate_b, w_out, router_w, router_b, exp_w_gate, exp_b_gate, exp_w_up, exp_b_up, exp_w_down, exp_b_down):
    assert norm1_g.shape[0] == 1, "single-layer trunk"
    row = lambda v: v.reshape(1, -1)
    mix_w = (
        row(norm1_g[0]), row(norm2_g[0]),
        w_in[0].astype(BF16), pool_w[0].astype(BF16), row(pool_scale[0]),
        jnp.broadcast_to(conv_dw[0][:, None, :], (CONV_WIDTH, SUBLANES, D_CONV)),
        row(conv_b[0]), row(conv_ln_g[0]), row(conv_ln_b[0]),
        conv_w_out[0].astype(BF16), gate_w[0].astype(BF16), row(gate_b[0]),
        w_out[0].astype(BF16), router_w[0].T.astype(BF16),
        jnp.broadcast_to(router_b[0][:, None], (N_EXPERTS, LANES)),
    )
    ew = (
        exp_w_gate[0], exp_b_gate[0][:, None, :],
        exp_w_up[0], exp_b_up[0][:, None, :],
        exp_w_down[0], exp_b_down[0][:, None, :],
    )
    bp, lp, _ = x_prompt.shape
    bs, ls, _ = x_sample.shape
    t_p = bp * lp
    tl_p, tl_s = 512, ls
    nb_tail = max(bp // PROMPT_TAIL_FRACTION, 1)
    chunk_nb = [bp - nb_tail, nb_tail] if bp > nb_tail else [bp]
    chunk_b0 = [sum(chunk_nb[:c]) for c in range(len(chunk_nb))]
    n_chunks = len(chunk_nb)

    mod_p = _modulation(c_prompt, w_ada[0], b_ada[0])
    mod_s = _modulation(c_sample, w_ada[0], b_ada[0])
    pad_state = lambda s, hist: jnp.pad(s, ((0, 0), (hist - s.shape[1], 0), (0, 0)))
    split = lambda out: (out[0], out[1:1 + N_CHUNKS], *out[1 + N_CHUNKS:])

    mixed = [split(_token_mix(
        x_prompt, mod_p, jnp.zeros((nb, POOL_HIST, D_POOL), F32), jnp.zeros((nb, CONV_HIST, D_CONV), F32),
        jnp.zeros((N_EXPERTS, LANES), F32), mix_w, b0=b0, tl=tl_p, th=tl_p, pos0=0, rc=32))
        for b0, nb in zip(chunk_b0, chunk_nb)]
    x1_s, hp_s, ri_s, rw_s, npool_s, nconv_s, cnt_last = split(_token_mix(
        x_sample, mod_s, pad_state(state_pool[0], POOL_HIST), pad_state(state_conv[0], CONV_HIST),
        mixed[-1][6], mix_w, b0=0, tl=tl_s, th=tl_s, pos0=PAST_LEN, rc=32))

    def finish(c, g_c, y_prev):
        x1_c, rw_c = mixed[c][0], mixed[c][3]
        return _final(x1_c, g_c, _route_rows(rw_c, 0, 2 * TOP_K).T, mod_p, final_g, tl=tl_p,
                      tiles_per_batch=lp // tl_p, b0=chunk_b0[c], out_rows=t_p, y_prev=y_prev)

    y_p = None
    gathered = []
    for c, (x1_c, hp_c, ri_c, rw_c, _, _, cnt_c) in enumerate(mixed):
        last = c == n_chunks - 1
        n_tok = chunk_nb[c] * lp + (bs * ls if last else 0)
        n_rows = (TOP_K * n_tok // MOE_TILE + N_EXPERTS) * MOE_TILE
        counts = (cnt_last if last else cnt_c)[:, 0].astype(I32)
        tiles_e = (counts + MOE_TILE - 1) // MOE_TILE
        tile_end = jnp.cumsum(tiles_e)
        tile_start = tile_end - tiles_e
        row_off = tile_start * MOE_TILE
        tile_id = jnp.arange(n_rows // MOE_TILE, dtype=I32)
        tile_expert = jnp.minimum(
            jnp.sum(tile_id[:, None] >= tile_end[None, :], axis=1), N_EXPERTS - 1).astype(I32)
        is_e = tile_expert[:, None] == jnp.arange(N_EXPERTS, dtype=I32)[None, :]
        left = jnp.sum(jnp.where(is_e, (counts - (tile_id[:, None] - tile_start[None, :]) * MOE_TILE), 0), axis=1)
        tile_rows = jnp.clip(left, 0, MOE_TILE).astype(I32)

        def slots(ri, row_off=row_off):
            ids, ranks = _route_rows(ri, 0, TOP_K), _route_rows(ri, TOP_K, 2 * TOP_K)
            pos = ranks
            for e in range(N_EXPERTS):
                pos = pos + jnp.where(ids == e, row_off[e], 0)
            return [pos[k] for k in range(TOP_K)]

        h_groups = [hp_c] + ([hp_s] if last else [])
        pos_groups = [slots(ri_c)] + ([slots(ri_s)] if last else [])
        xs = _dispatch(h_groups, pos_groups, n_rows)
        ys = _expert_ffn(xs, tile_expert, tile_rows, ew)
        if c > 0:
            y_p = finish(c - 1, gathered[c - 1][0], y_p)
        gathered.append(_combine_gather(ys, pos_groups))

    y_p = finish(n_chunks - 1, gathered[-1][0], y_p)
    y_s = _final(x1_s, gathered[-1][1], _route_rows(rw_s, 0, 2 * TOP_K).T, mod_s, final_g, tl=tl_s,
                 tiles_per_batch=1)
    unpad = lambda s, n: s[:, s.shape[1] - n:][None]
    npool_p = jnp.concatenate([m[4] for m in mixed], axis=0)
    nconv_p = jnp.concatenate([m[5] for m in mixed], axis=0)
    return (y_p.reshape(bp, lp, D_MODEL), y_s.reshape(bs, ls, D_MODEL),
            unpad(npool_p, POOL_PAD), unpad(nconv_p, CONV_PAD),
            unpad(npool_s, POOL_PAD), unpad(nconv_s, CONV_PAD))
```

```python
import functools

import jax
import jax.numpy as jnp
from jax import lax
from jax.experimental import pallas as pl
from jax.experimental.pallas import tpu as pltpu
from jax.experimental.pallas import tpu_sc as plsc

D_MODEL = 1024
D_POOL = 512
D_CONV = 512
POOL_WINDOWS = (2, 4, 8, 16)
POOL_GW = 128
POOL_OUT_GW = 256
POOL_PAD = 15
CONV_WIDTH = 31
CONV_PAD = 30
N_EXPERTS = 32
TOP_K = 4
SWIGLU_LIMIT = 7.0
SWIGLU_ALPHA = 1.702
EPS = 1e-6
PAST_LEN = 2048

POOL_HIST = 16
CONV_HIST = 32
VMEM_LIMIT_BYTES = 56 * 1024 * 1024

LANES = 128
SUBLANES = 8
D_WORDS = D_MODEL // 2
N_CHUNKS = D_WORDS // LANES
MOE_TILE = 512
PROMPT_TAIL_FRACTION = 4
SC_CORES = 2
SC_SUBCORES = 16
SC_WORKERS = SC_CORES * SC_SUBCORES
SC_ROWS = 128

F32 = jnp.float32
BF16 = jnp.bfloat16
I32 = jnp.int32
HI_MASK = -65536
NEG_LOG2E = -1.4426950408889634


def _sigmoid(v):
    return 1.0 / (1.0 + jnp.exp2(v * NEG_LOG2E))


def _pack_words(v):
    r = v.astype(BF16).astype(F32)
    hi = lax.bitcast_convert_type(r[:, :D_WORDS], I32)
    lo = lax.bitcast_convert_type(r[:, D_WORDS:], I32)
    return (hi & HI_MASK) | lax.shift_right_logical(lo, 16)


def _unpack_words(w):
    hi = lax.bitcast_convert_type(w & HI_MASK, F32)
    lo = lax.bitcast_convert_type(lax.shift_left(w, 16), F32)
    return hi, lo


def _mod_kernel(c_ref, w_ref, b_ref, o_ref):
    c = c_ref[...]
    s = (c * _sigmoid(c)).astype(BF16)
    o_ref[...] = jnp.dot(s, w_ref[...].astype(BF16), preferred_element_type=F32) + b_ref[...]


def _modulation(c, w_ada, b_ada):
    bsz = c.shape[0]
    out = pl.pallas_call(
        _mod_kernel,
        out_shape=jax.ShapeDtypeStruct((bsz, 6 * D_MODEL), F32),
        grid=(6,),
        in_specs=[
            pl.BlockSpec((bsz, D_MODEL), lambda j: (0, 0)),
            pl.BlockSpec((D_MODEL, D_MODEL), lambda j: (0, j)),
            pl.BlockSpec((1, D_MODEL), lambda j: (0, j)),
        ],
        out_specs=pl.BlockSpec((bsz, D_MODEL), lambda j: (0, j)),
        compiler_params=pltpu.CompilerParams(
            dimension_semantics=("arbitrary",), vmem_limit_bytes=VMEM_LIMIT_BYTES),
        name="adaln_mod",
    )(c, w_ada, b_ada.reshape(1, 6 * D_MODEL))
    out = out.reshape(bsz, 6, D_MODEL)
    return jnp.pad(out, ((0, 0), (0, 2), (0, 0)))


def _mix_kernel(x_ref, mod_ref, pinit_ref, cinit_ref, cntin_ref, tri_ref,
                n1_ref, n2_ref, win_ref, pw_ref, ps_ref, dw_ref, cb_ref, lg_ref, lb_ref,
                cwo_ref, gw_ref, gb_ref, wo_ref, rwt_ref, rb_ref,
                x1_ref, hp0_ref, hp1_ref, hp2_ref, hp3_ref, ri_ref, rw_ref,
                npool_ref, nconv_ref, cntout_ref,
                conv_sh, pool_in, hbuf, merged, ybuf, sy, cnt_run, *, tl, th, pos0, rc):
    b = pl.program_id(0)
    l = pl.program_id(1)

    conv_in = conv_sh.at[0]

    @pl.when(l == 0)
    def _():
        conv_in[0:CONV_HIST, :] = cinit_ref[0]
        pool_in[0:POOL_HIST, :] = pinit_ref[0]

    @pl.when((b == 0) & (l == 0))
    def _():
        cnt_run[...] = cntin_ref[...]

    mod = mod_ref[0]
    sh1, sc1, g1 = mod[0:1], mod[1:2], mod[2:3]
    sh2, sc2 = mod[3:4], mod[4:5]
    scale1 = n1_ref[...] * (1.0 + sc1)
    scale2 = n2_ref[...] * (1.0 + sc2)

    for r0 in range(0, tl, th):
        rows = slice(r0, r0 + th)
        x = x_ref[rows, :]
        ms = jnp.mean(x * x, axis=-1, keepdims=True)
        hbuf[rows, :] = ((x * lax.rsqrt(ms + EPS)) * scale1 + sh1).astype(BF16)

        proj = jnp.dot(hbuf[rows, :], win_ref[...], preferred_element_type=F32)
        glu = proj[:, D_POOL:D_POOL + D_CONV] * _sigmoid(proj[:, D_POOL + D_CONV:])
        pool_in[POOL_HIST + r0:POOL_HIST + r0 + th, :] = proj[:, :D_POOL]
        conv_in[CONV_HIST + r0:CONV_HIST + r0 + th, :] = glu

        pos = pos0 + l * tl + r0 + lax.broadcasted_iota(I32, (th, POOL_GW), 0)
        pooled = []
        for gi, w in enumerate(POOL_WINDOWS):
            lanes = slice(gi * POOL_GW, (gi + 1) * POOL_GW)
            s = pool_in[r0:r0 + POOL_HIST + th, lanes]
            d = 1
            while d < w:
                s = s + pltpu.roll(s, d, axis=0)
                d *= 2
            cur = pool_in[POOL_HIST + r0:POOL_HIST + r0 + th, lanes]
            cnt = jnp.minimum(pos + 1, w).astype(F32)
            pooled.append((s[POOL_HIST:] / cnt - cur).astype(BF16))

        sh_lo = 0 if r0 == 0 else r0 + CONV_HIST - SUBLANES
        sh_hi = r0 + th + CONV_HIST - SUBLANES
        for r in range(1, SUBLANES):
            conv_sh[r, sh_lo:sh_hi, :] = conv_in[sh_lo + r:sh_hi + r, :]

        for base in range(r0, r0 + th, rc):
            acc = jnp.broadcast_to(cb_ref[...], (rc // SUBLANES, SUBLANES, D_CONV))
            for k in range(CONV_WIDTH):
                q, r = divmod(k + CONV_HIST - CONV_PAD, SUBLANES)
                start = base + q * SUBLANES
                win = conv_sh[r, start:start + rc, :].reshape(rc // SUBLANES, SUBLANES, D_CONV)
                acc = acc + win * dw_ref[k]
            ybuf[base:base + rc, :] = acc.reshape(rc, D_CONV)

        yc = ybuf[rows, :]
        mu = jnp.mean(yc, axis=-1, keepdims=True)
        dev = yc - mu
        var = jnp.mean(dev * dev, axis=-1, keepdims=True)
        yn = dev * lax.rsqrt(var + EPS) * lg_ref[...] + lb_ref[...]
        sy[rows, :] = (yn * _sigmoid(yn)).astype(BF16)

        for j in range(len(POOL_WINDOWS)):
            cs = slice(j * POOL_OUT_GW, (j + 1) * POOL_OUT_GW)
            cs2 = slice(D_MODEL + j * POOL_OUT_GW, D_MODEL + (j + 1) * POOL_OUT_GW)
            ga = _sigmoid(jnp.dot(hbuf[rows, :], gw_ref[:, cs], preferred_element_type=F32) + gb_ref[:, cs])
            gb = _sigmoid(jnp.dot(hbuf[rows, :], gw_ref[:, cs2], preferred_element_type=F32) + gb_ref[:, cs2])
            a_j = jnp.dot(pooled[j], pw_ref[j], preferred_element_type=F32) * ps_ref[:, cs]
            b_j = jnp.dot(sy[rows, :], cwo_ref[:, cs], preferred_element_type=F32)
            merged[rows, cs] = (ga * a_j + gb * b_j).astype(BF16)

        x1 = x + g1 * jnp.dot(merged[rows, :], wo_ref[...], preferred_element_type=F32)
        x1_ref[rows, :] = x1

        ms2 = jnp.mean(x1 * x1, axis=-1, keepdims=True)
        h2 = (x1 * lax.rsqrt(ms2 + EPS)) * scale2 + sh2
        h2b = h2.astype(BF16)
        words = _pack_words(h2)
        for j, ref in enumerate((hp0_ref, hp1_ref, hp2_ref, hp3_ref)):
            ref[rows, :] = words[:, j * LANES:(j + 1) * LANES]

        logits = lax.dot_general(rwt_ref[...], h2b, (((1,), (1,)), ((), ())),
                                 preferred_element_type=F32) + rb_ref[:, 0:1]
        e_iota = lax.broadcasted_iota(I32, (N_EXPERTS, th), 0)
        v = logits
        ids, vals = [], []
        for _ in range(TOP_K):
            m = jnp.max(v, axis=0, keepdims=True)
            idx = jnp.min(jnp.where(v == m, e_iota, N_EXPERTS), axis=0, keepdims=True)
            ids.append(idx)
            vals.append(m)
            v = jnp.where(e_iota == idx, -jnp.inf, v)
        ex = [jnp.exp(vk - vals[0]) for vk in vals]
        den = ex[0] + ex[1] + ex[2] + ex[3]

        sel = [e_iota == idx for idx in ids]
        chosen = jnp.where(sel[0] | sel[1] | sel[2] | sel[3], 1.0, 0.0)
        before = jnp.dot(chosen.astype(BF16), tri_ref[...], preferred_element_type=F32)
        rank_all = cnt_run[:, 0:1] + before
        for k in range(TOP_K):
            ri_ref[0, k:k + 1, rows] = ids[k]
            rk = jnp.sum(jnp.where(sel[k], rank_all, 0.0), axis=0, keepdims=True)
            ri_ref[0, TOP_K + k:TOP_K + k + 1, rows] = rk.astype(I32)
            rw_ref[0, k:k + 1, rows] = ex[k] / den
            rw_ref[0, TOP_K + k:TOP_K + k + 1, rows] = jnp.zeros((1, th), F32)
        cnt_run[...] = cnt_run[...] + jnp.sum(chosen, axis=1, keepdims=True)

    npool_ref[0] = pool_in[tl:tl + POOL_HIST, :]
    nconv_ref[0] = conv_in[tl:tl + CONV_HIST, :]
    conv_in[0:CONV_HIST, :] = conv_in[tl:tl + CONV_HIST, :]
    pool_in[0:POOL_HIST, :] = pool_in[tl:tl + POOL_HIST, :]
    cntout_ref[...] = cnt_run[...]


def _token_mix(x, mod, pool_init, conv_init, cnt_in, wts, *, b0, tl, th, pos0, rc):
    _, seq, _ = x.shape
    bsz = pool_init.shape[0]
    assert seq % tl == 0 and tl % th == 0 and th % rc == 0 and th >= CONV_HIST
    assert th % LANES == 0 or th == tl
    rows_out = bsz * seq
    nl = seq // tl
    x2d = x.reshape(-1, D_MODEL)
    tri = jnp.triu(jnp.ones((th, th), BF16), k=1)
    full = lambda shape: pl.BlockSpec(shape, lambda b, l: (0,) * len(shape))
    per_b = lambda shape: pl.BlockSpec(shape, lambda b, l: (b,) + (0,) * (len(shape) - 1))
    tok = lambda width: pl.BlockSpec((tl, width), lambda b, l: (b * nl + l, 0))
    rt = lambda shape: pl.BlockSpec(shape, lambda b, l: (b * nl + l, 0, 0))
    in_specs = [
        pl.BlockSpec((tl, D_MODEL), lambda b, l: ((b0 + b) * nl + l, 0)),
        pl.BlockSpec((1, 8, D_MODEL), lambda b, l: (b0 + b, 0, 0)),
        per_b((1, POOL_HIST, D_POOL)),
        per_b((1, CONV_HIST, D_CONV)),
        full((N_EXPERTS, LANES)),
        full((th, th)),
        full((1, D_MODEL)), full((1, D_MODEL)),
        full((D_MODEL, D_POOL + 2 * D_CONV)),
        full((len(POOL_WINDOWS), POOL_GW, POOL_OUT_GW)),
        full((1, D_MODEL)),
        full((CONV_WIDTH, SUBLANES, D_CONV)),
        full((1, D_CONV)), full((1, D_CONV)), full((1, D_CONV)),
        full((D_CONV, D_MODEL)),
        full((D_MODEL, 2 * D_MODEL)),
        full((1, 2 * D_MODEL)),
        full((D_MODEL, D_MODEL)),
        full((N_EXPERTS, D_MODEL)),
        full((N_EXPERTS, LANES)),
    ]
    out_shape = (
        jax.ShapeDtypeStruct((rows_out, D_MODEL), F32),
        *[jax.ShapeDtypeStruct((rows_out, LANES), I32)] * N_CHUNKS,
        jax.ShapeDtypeStruct((bsz * nl, 2 * TOP_K, tl), I32),
        jax.ShapeDtypeStruct((bsz * nl, 2 * TOP_K, tl), F32),
        jax.ShapeDtypeStruct((bsz, POOL_HIST, D_POOL), F32),
        jax.ShapeDtypeStruct((bsz, CONV_HIST, D_CONV), F32),
        jax.ShapeDtypeStruct((N_EXPERTS, LANES), F32),
    )
    out_specs = (
        tok(D_MODEL),
        *[tok(LANES)] * N_CHUNKS,
        rt((1, 2 * TOP_K, tl)),
        rt((1, 2 * TOP_K, tl)),
        per_b((1, POOL_HIST, D_POOL)),
        per_b((1, CONV_HIST, D_CONV)),
        full((N_EXPERTS, LANES)),
    )
    scratch = [
        pltpu.VMEM((SUBLANES, tl + CONV_HIST, D_CONV), F32),
        pltpu.VMEM((tl + POOL_HIST, D_POOL), F32),
        pltpu.VMEM((tl, D_MODEL), BF16),
        pltpu.VMEM((tl, D_MODEL), BF16),
        pltpu.VMEM((tl, D_CONV), F32),
        pltpu.VMEM((tl, D_CONV), BF16),
        pltpu.VMEM((N_EXPERTS, LANES), F32),
    ]
    return pl.pallas_call(
        functools.partial(_mix_kernel, tl=tl, th=th, pos0=pos0, rc=rc),
        out_shape=out_shape,
        grid=(bsz, nl),
        in_specs=in_specs,
        out_specs=out_specs,
        scratch_shapes=scratch,
        compiler_params=pltpu.CompilerParams(
            dimension_semantics=("arbitrary", "arbitrary"), vmem_limit_bytes=VMEM_LIMIT_BYTES),
        name="token_mix",
    )(x2d, mod, pool_init, conv_init, cnt_in, tri, *wts)


def _sc_mesh():
    return plsc.VectorSubcoreMesh(core_axis_name="c", subcore_axis_name="s",
                                  num_cores=SC_CORES, num_subcores=SC_SUBCORES)


def _for_each_chunk(n_tok, fn):
    wid = lax.axis_index("s") * SC_CORES + lax.axis_index("c")
    if n_tok % (8 * SC_WORKERS) == 0 and n_tok // SC_WORKERS >= SC_ROWS:
        per_w = n_tok // SC_WORKERS

        @pl.loop(0, pl.cdiv(per_w, SC_ROWS))
        def _(c):
            fn(pl.multiple_of(wid * per_w + jnp.minimum(c * SC_ROWS, per_w - SC_ROWS), 8))
    else:
        assert n_tok % SC_ROWS == 0 and n_tok // SC_ROWS <= SC_WORKERS

        @pl.when(wid < n_tok // SC_ROWS)
        def _():
            fn(pl.multiple_of(wid * SC_ROWS, SC_ROWS))


_SC_SCRATCH = ([pltpu.VMEM((SC_ROWS, LANES), I32)] * N_CHUNKS
               + [pltpu.VMEM((SC_ROWS,), I32)] * TOP_K
               + [pltpu.SemaphoreType.DMA, pltpu.SemaphoreType.DMA])


def _dispatch(h_groups, pos_groups, n_rows):
    n_g = len(h_groups)

    def body(*refs):
        refs = list(refs)
        h = [[refs.pop(0) for _ in range(N_CHUNKS)] for _ in range(n_g)]
        p = [[refs.pop(0) for _ in range(TOP_K)] for _ in range(n_g)]
        o = [refs.pop(0) for _ in range(N_CHUNKS)]
        rows = [refs.pop(0) for _ in range(N_CHUNKS)]
        idx = [refs.pop(0) for _ in range(TOP_K)]
        sem_in, sem_out = refs

        for hg, pg in zip(h, p):
            def move(s, hg=hg, pg=pg):
                loads = [pltpu.async_copy(hg[j].at[pl.ds(s, SC_ROWS)], rows[j], sem_in)
                         for j in range(N_CHUNKS)]
                loads += [pltpu.async_copy(pg[k].at[pl.ds(s, SC_ROWS)], idx[k], sem_in)
                          for k in range(TOP_K)]
                for cp in loads:
                    cp.wait()
                stores = [pltpu.async_copy(rows[j], o[j].at[idx[k]], sem_out)
                          for j in range(N_CHUNKS) for k in range(TOP_K)]
                for cp in stores:
                    cp.wait()

            _for_each_chunk(hg[0].shape[0], move)

    call = pl.kernel(
        body,
        out_type=tuple(jax.ShapeDtypeStruct((n_rows, LANES), I32) for _ in range(N_CHUNKS)),
        mesh=_sc_mesh(), scratch_types=_SC_SCRATCH, name="sc_dispatch")
    flat = [a for g in h_groups for a in g] + [a for g in pos_groups for a in g]
    return call(*flat)


def _combine_gather(y_chunks, pos_groups):
    n_g = len(pos_groups)

    def body(*refs):
        refs = list(refs)
        y = [refs.pop(0) for _ in range(N_CHUNKS)]
        p = [[refs.pop(0) for _ in range(TOP_K)] for _ in range(n_g)]
        g = [refs.pop(0) for _ in range(n_g)]
        rows = [refs.pop(0) for _ in range(N_CHUNKS)]
        idx = [refs.pop(0) for _ in range(TOP_K)]
        sem_in, sem_out = refs

        for pg, gg in zip(p, g):
            def move(s, pg=pg, gg=gg):
                loads = [pltpu.async_copy(pg[k].at[pl.ds(s, SC_ROWS)], idx[k], sem_in)
                         for k in range(TOP_K)]
                for cp in loads:
                    cp.wait()
                for k in range(TOP_K):
                    gathers = [pltpu.async_copy(y[j].at[idx[k]], rows[j], sem_in)
                               for j in range(N_CHUNKS)]
                    for cp in gathers:
                        cp.wait()
                    stores = [pltpu.async_copy(rows[j], gg.at[k * N_CHUNKS + j, pl.ds(s, SC_ROWS)], sem_out)
                              for j in range(N_CHUNKS)]
                    for cp in stores:
                        cp.wait()

            _for_each_chunk(pg[0].shape[0], move)

    call = pl.kernel(
        body,
        out_type=tuple(jax.ShapeDtypeStruct((TOP_K * N_CHUNKS, pg[0].shape[0], LANES), I32)
                       for pg in pos_groups),
        mesh=_sc_mesh(), scratch_types=_SC_SCRATCH, name="sc_combine_gather")
    return call(*y_chunks, *[a for g in pos_groups for a in g])


def _expert_kernel(te_ref, tr_ref, x0_ref, x1_ref, x2_ref, x3_ref,
                   wg_ref, bg_ref, wu_ref, bu_ref, wd_ref, bd_ref, *rest):
    y0_ref, y1_ref, y2_ref, y3_ref, wg_bf, wu_bf, wd_bf = rest[-7:]
    i = pl.program_id(0)

    @pl.when((i == 0) | (te_ref[i] != te_ref[jnp.maximum(i - 1, 0)]))
    def _():
        wg_bf[...] = wg_ref[0].astype(BF16)
        wu_bf[...] = wu_ref[0].astype(BF16)
        wd_bf[...] = wd_ref[0].astype(BF16)

    @pl.when(tr_ref[i] > 0)
    def _():
        words = jnp.concatenate([x0_ref[...], x1_ref[...], x2_ref[...], x3_ref[...]], axis=1)
        hi, lo = _unpack_words(words)
        xt = jnp.concatenate([hi, lo], axis=1).astype(BF16)
        gt = jnp.minimum(jnp.dot(xt, wg_bf[...], preferred_element_type=F32) + bg_ref[0], SWIGLU_LIMIT)
        up = jnp.clip(jnp.dot(xt, wu_bf[...], preferred_element_type=F32) + bu_ref[0],
                      -SWIGLU_LIMIT, SWIGLU_LIMIT)
        act = gt * _sigmoid(SWIGLU_ALPHA * gt) * (up + 1.0)
        y = jnp.dot(act.astype(BF16), wd_bf[...], preferred_element_type=F32) + bd_ref[0]
        out = _pack_words(y)
        for j, ref in enumerate((y0_ref, y1_ref, y2_ref, y3_ref)):
            ref[...] = out[:, j * LANES:(j + 1) * LANES]


def _expert_ffn(xs_chunks, tile_expert, tile_rows, ew, after=None):
    n_rows = xs_chunks[0].shape[0]
    nt_max = n_rows // MOE_TILE
    wg, bg, wu, bu, wd, bd = ew
    rows = pl.BlockSpec((MOE_TILE, LANES), lambda i, te, nt: (i, 0))
    mat = pl.BlockSpec((1, D_MODEL, D_MODEL), lambda i, te, nt: (te[i], 0, 0))
    vec = pl.BlockSpec((1, 1, D_MODEL), lambda i, te, nt: (te[i], 0, 0))
    grid_spec = pltpu.PrefetchScalarGridSpec(
        num_scalar_prefetch=2,
        grid=(nt_max,),
        in_specs=([rows] * N_CHUNKS + [mat, vec, mat, vec, mat, vec]
                  + ([] if after is None else [pl.BlockSpec(memory_space=pl.ANY)])),
        out_specs=[rows] * N_CHUNKS,
        scratch_shapes=[pltpu.VMEM((D_MODEL, D_MODEL), BF16)] * 3,
    )
    extra = [] if after is None else [after]
    return pl.pallas_call(
        _expert_kernel,
        out_shape=[jax.ShapeDtypeStruct((n_rows, LANES), I32)] * N_CHUNKS,
        grid_spec=grid_spec,
        compiler_params=pltpu.CompilerParams(
            dimension_semantics=("arbitrary",), vmem_limit_bytes=VMEM_LIMIT_BYTES),
        name="expert_ffn",
    )(tile_expert, tile_rows, *xs_chunks, wg, bg, wu, bu, wd, bd, *extra)


def _final_kernel(x1_ref, g_ref, w_ref, mod_ref, fg_ref, *rest):
    y_ref = rest[-1]
    wt = w_ref[...]
    acc = None
    for k in range(TOP_K):
        words = jnp.concatenate([g_ref[k * N_CHUNKS + j] for j in range(N_CHUNKS)], axis=1)
        hi, lo = _unpack_words(words)
        term = wt[:, k:k + 1] * jnp.concatenate([hi, lo], axis=1)
        acc = term if acc is None else acc + term
    g2 = mod_ref[0][5:6]
    x2 = x1_ref[...] + g2 * acc
    ms = jnp.mean(x2 * x2, axis=-1, keepdims=True)
    y_ref[...] = x2 * lax.rsqrt(ms + EPS) * fg_ref[...]


def _final(x1, gathered, wt_tok, mod, final_g, *, tl, tiles_per_batch, b0=0, out_rows=None, y_prev=None):
    n_tok = x1.shape[0]
    out_rows = n_tok if out_rows is None else out_rows
    blk0 = b0 * tiles_per_batch
    in_specs = [
        pl.BlockSpec((tl, D_MODEL), lambda i: (i, 0)),
        pl.BlockSpec((TOP_K * N_CHUNKS, tl, LANES), lambda i: (0, i, 0)),
        pl.BlockSpec((tl, 2 * TOP_K), lambda i: (i, 0)),
        pl.BlockSpec((1, 8, D_MODEL), lambda i: (b0 + i // tiles_per_batch, 0, 0)),
        pl.BlockSpec((1, D_MODEL), lambda i: (0, 0)),
    ]
    args = [x1, gathered, wt_tok, mod, final_g.reshape(1, D_MODEL)]
    aliases = {}
    if y_prev is not None:
        in_specs.append(pl.BlockSpec(memory_space=pl.ANY))
        args.append(y_prev)
        aliases = {len(args) - 1: 0}
    return pl.pallas_call(
        _final_kernel,
        out_shape=jax.ShapeDtypeStruct((out_rows, D_MODEL), F32),
        grid=(n_tok // tl,),
        in_specs=in_specs,
        out_specs=pl.BlockSpec((tl, D_MODEL), lambda i: (blk0 + i, 0)),
        input_output_aliases=aliases,
        compiler_params=pltpu.CompilerParams(
            dimension_semantics=("arbitrary",), vmem_limit_bytes=VMEM_LIMIT_BYTES),
        name="combine_final",
    )(*args)


def _route_rows(r, lo, hi):
    return r[:, lo:hi, :].transpose(1, 0, 2).reshape(hi - lo, -1)


def kernel(x_prompt, x_sample, state_pool, state_conv, c_prompt, c_sample, norm1_g, norm2_g, final_g, w_ada, b_ada, w_in, pool_w, pool_scale, conv_dw, conv_b, conv_ln_g, conv_ln_b, conv_w_out, gate_w, gate_b, w_out, router_w, router_b, exp_w_gate, exp_b_gate, exp_w_up, exp_b_up, exp_w_down, exp_b_down):
    assert norm1_g.shape[0] == 1, "single-layer trunk"
    row = lambda v: v.reshape(1, -1)
    mix_w = (
        row(norm1_g[0]), row(norm2_g[0]),
        w_in[0].astype(BF16), pool_w[0].astype(BF16), row(pool_scale[0]),
        jnp.broadcast_to(conv_dw[0][:, None, :], (CONV_WIDTH, SUBLANES, D_CONV)),
        row(conv_b[0]), row(conv_ln_g[0]), row(conv_ln_b[0]),
        conv_w_out[0].astype(BF16), gate_w[0].astype(BF16), row(gate_b[0]),
        w_out[0].astype(BF16), router_w[0].T.astype(BF16),
        jnp.broadcast_to(router_b[0][:, None], (N_EXPERTS, LANES)),
    )
    ew = (
        exp_w_gate[0], exp_b_gate[0][:, None, :],
        exp_w_up[0], exp_b_up[0][:, None, :],
        exp_w_down[0], exp_b_down[0][:, None, :],
    )
    bp, lp, _ = x_prompt.shape
    bs, ls, _ = x_sample.shape
    t_p = bp * lp
    tl_p, tl_s = 512, ls
    nb_tail = max(bp // PROMPT_TAIL_FRACTION, 1)
    chunk_nb = [bp - nb_tail, nb_tail] if bp > nb_tail else [bp]
    chunk_b0 = [sum(chunk_nb[:c]) for c in range(len(chunk_nb))]
    n_chunks = len(chunk_nb)

    mod_p = _modulation(c_prompt, w_ada[0], b_ada[0])
    mod_s = _modulation(c_sample, w_ada[0], b_ada[0])
    pad_state = lambda s, hist: jnp.pad(s, ((0, 0), (hist - s.shape[1], 0), (0, 0)))
    split = lambda out: (out[0], out[1:1 + N_CHUNKS], *out[1 + N_CHUNKS:])

    cnt_at = [jnp.zeros((N_EXPERTS, LANES), F32)]
    mixed = []
    for b0, nb in zip(chunk_b0, chunk_nb):
        mixed.append(split(_token_mix(
            x_prompt, mod_p, jnp.zeros((nb, POOL_HIST, D_POOL), F32), jnp.zeros((nb, CONV_HIST, D_CONV), F32),
            cnt_at[-1], mix_w, b0=b0, tl=tl_p, th=tl_p, pos0=0, rc=32)))
        cnt_at.append(mixed[-1][6])
    x1_s, hp_s, ri_s, rw_s, npool_s, nconv_s, cnt_end = split(_token_mix(
        x_sample, mod_s, pad_state(state_pool[0], POOL_HIST), pad_state(state_conv[0], CONV_HIST),
        cnt_at[-1], mix_w, b0=0, tl=tl_s, th=tl_s, pos0=PAST_LEN, rc=32))
    cnt_at[-1] = cnt_end

    def finish(c, g_c, y_prev):
        x1_c, rw_c = mixed[c][0], mixed[c][3]
        return _final(x1_c, g_c, _route_rows(rw_c, 0, 2 * TOP_K).T, mod_p, final_g, tl=tl_p,
                      tiles_per_batch=lp // tl_p, b0=chunk_b0[c], out_rows=t_p, y_prev=y_prev)

    y_p = None
    ys = None
    gathered = []
    for c, (x1_c, hp_c, ri_c, rw_c, _, _, _) in enumerate(mixed):
        last = c == n_chunks - 1
        n_tok = chunk_nb[c] * lp + (bs * ls if last else 0)
        n_rows = (TOP_K * n_tok // MOE_TILE + N_EXPERTS) * MOE_TILE
        cnt_before = cnt_at[c][:, 0].astype(I32)
        counts = cnt_at[c + 1][:, 0].astype(I32) - cnt_before
        tiles_e = (counts + MOE_TILE - 1) // MOE_TILE
        tile_end = jnp.cumsum(tiles_e)
        tile_start = tile_end - tiles_e
        row_off = tile_start * MOE_TILE - cnt_before
        tile_id = jnp.arange(n_rows // MOE_TILE, dtype=I32)
        tile_expert = jnp.minimum(
            jnp.sum(tile_id[:, None] >= tile_end[None, :], axis=1), N_EXPERTS - 1).astype(I32)
        is_e = tile_expert[:, None] == jnp.arange(N_EXPERTS, dtype=I32)[None, :]
        left = jnp.sum(jnp.where(is_e, (counts - (tile_id[:, None] - tile_start[None, :]) * MOE_TILE), 0), axis=1)
        tile_rows = jnp.clip(left, 0, MOE_TILE).astype(I32)

        def slots(ri, row_off=row_off):
            ids, ranks = _route_rows(ri, 0, TOP_K), _route_rows(ri, TOP_K, 2 * TOP_K)
            pos = ranks
            for e in range(N_EXPERTS):
                pos = pos + jnp.where(ids == e, row_off[e], 0)
            return [pos[k] for k in range(TOP_K)]

        h_groups = [hp_c] + ([hp_s] if last else [])
        pos_groups = [slots(ri_c)] + ([slots(ri_s)] if last else [])
        xs = _dispatch(h_groups, pos_groups, n_rows)
        ys = _expert_ffn(xs, tile_expert, tile_rows, ew, after=None if ys is None else ys[0])
        if c > 0:
            y_p = finish(c - 1, gathered[c - 1][0], y_p)
        gathered.append(_combine_gather(ys, pos_groups))

    y_p = finish(n_chunks - 1, gathered[-1][0], y_p)
    y_s = _final(x1_s, gathered[-1][1], _route_rows(rw_s, 0, 2 * TOP_K).T, mod_s, final_g, tl=tl_s,
                 tiles_per_batch=1)
    unpad = lambda s, n: s[:, s.shape[1] - n:][None]
    npool_p = jnp.concatenate([m[4] for m in mixed], axis=0)
    nconv_p = jnp.concatenate([m[5] for m in mixed], axis=0)
    return (y_p.reshape(bp, lp, D_MODEL), y_s.reshape(bs, ls, D_MODEL),
            unpad(npool_p, POOL_PAD), unpad(nconv_p, CONV_PAD),
            unpad(npool_s, POOL_PAD), unpad(nconv_s, CONV_PAD))
```

```python
import functools

import jax
import jax.numpy as jnp
from jax import lax
from jax.experimental import pallas as pl
from jax.experimental.pallas import tpu as pltpu
from jax.experimental.pallas import tpu_sc as plsc

D_MODEL = 1024
D_POOL = 512
D_CONV = 512
POOL_WINDOWS = (2, 4, 8, 16)
POOL_GW = 128
POOL_OUT_GW = 256
POOL_PAD = 15
CONV_WIDTH = 31
CONV_PAD = 30
N_EXPERTS = 32
TOP_K = 4
SWIGLU_LIMIT = 7.0
SWIGLU_ALPHA = 1.702
EPS = 1e-6
PAST_LEN = 2048

POOL_HIST = 16
CONV_HIST = 32
VMEM_LIMIT_BYTES = 56 * 1024 * 1024

LANES = 128
SUBLANES = 8
D_WORDS = D_MODEL // 2
N_CHUNKS = D_WORDS // LANES
MOE_TILE = 512
PROMPT_TAIL_FRACTION = 4
SC_CORES = 2
SC_SUBCORES = 16
SC_WORKERS = SC_CORES * SC_SUBCORES
SC_ROWS = 128

F32 = jnp.float32
BF16 = jnp.bfloat16
I32 = jnp.int32
HI_MASK = -65536
NEG_LOG2E = -1.4426950408889634


def _sigmoid(v):
    return 1.0 / (1.0 + jnp.exp2(v * NEG_LOG2E))


def _pack_words(v):
    r = v.astype(BF16).astype(F32)
    hi = lax.bitcast_convert_type(r[:, :D_WORDS], I32)
    lo = lax.bitcast_convert_type(r[:, D_WORDS:], I32)
    return (hi & HI_MASK) | lax.shift_right_logical(lo, 16)


def _unpack_words(w):
    hi = lax.bitcast_convert_type(w & HI_MASK, F32)
    lo = lax.bitcast_convert_type(lax.shift_left(w, 16), F32)
    return hi, lo


def _mod_kernel(c_ref, w_ref, b_ref, o_ref):
    c = c_ref[...]
    s = (c * _sigmoid(c)).astype(BF16)
    o_ref[...] = jnp.dot(s, w_ref[...].astype(BF16), preferred_element_type=F32) + b_ref[...]


def _modulation(c, w_ada, b_ada):
    bsz = c.shape[0]
    out = pl.pallas_call(
        _mod_kernel,
        out_shape=jax.ShapeDtypeStruct((bsz, 6 * D_MODEL), F32),
        grid=(6,),
        in_specs=[
            pl.BlockSpec((bsz, D_MODEL), lambda j: (0, 0)),
            pl.BlockSpec((D_MODEL, D_MODEL), lambda j: (0, j)),
            pl.BlockSpec((1, D_MODEL), lambda j: (0, j)),
        ],
        out_specs=pl.BlockSpec((bsz, D_MODEL), lambda j: (0, j)),
        compiler_params=pltpu.CompilerParams(
            dimension_semantics=("arbitrary",), vmem_limit_bytes=VMEM_LIMIT_BYTES),
        name="adaln_mod",
    )(c, w_ada, b_ada.reshape(1, 6 * D_MODEL))
    out = out.reshape(bsz, 6, D_MODEL)
    return jnp.pad(out, ((0, 0), (0, 2), (0, 0)))


def _mix_kernel(x_ref, mod_ref, pinit_ref, cinit_ref, cntin_ref, tri_ref,
                n1_ref, n2_ref, win_ref, pw_ref, ps_ref, dw_ref, cb_ref, lg_ref, lb_ref,
                cwo_ref, gw_ref, gb_ref, wo_ref, rwt_ref, rb_ref,
                x1_ref, hp0_ref, hp1_ref, hp2_ref, hp3_ref, ri_ref, rw_ref,
                npool_ref, nconv_ref, cntout_ref,
                conv_sh, pool_in, hbuf, merged, ybuf, sy, cnt_run, *, tl, th, pos0, rc):
    b = pl.program_id(0)
    l = pl.program_id(1)

    conv_in = conv_sh.at[0]

    @pl.when(l == 0)
    def _():
        conv_in[0:CONV_HIST, :] = cinit_ref[0]
        pool_in[0:POOL_HIST, :] = pinit_ref[0]

    @pl.when((b == 0) & (l == 0))
    def _():
        cnt_run[...] = cntin_ref[...]

    mod = mod_ref[0]
    sh1, sc1, g1 = mod[0:1], mod[1:2], mod[2:3]
    sh2, sc2 = mod[3:4], mod[4:5]
    scale1 = n1_ref[...] * (1.0 + sc1)
    scale2 = n2_ref[...] * (1.0 + sc2)

    for r0 in range(0, tl, th):
        rows = slice(r0, r0 + th)
        x = x_ref[rows, :]
        ms = jnp.mean(x * x, axis=-1, keepdims=True)
        hbuf[rows, :] = ((x * lax.rsqrt(ms + EPS)) * scale1 + sh1).astype(BF16)

        proj = jnp.dot(hbuf[rows, :], win_ref[...], preferred_element_type=F32)
        glu = proj[:, D_POOL:D_POOL + D_CONV] * _sigmoid(proj[:, D_POOL + D_CONV:])
        pool_in[POOL_HIST + r0:POOL_HIST + r0 + th, :] = proj[:, :D_POOL]
        conv_in[CONV_HIST + r0:CONV_HIST + r0 + th, :] = glu

        pos = pos0 + l * tl + r0 + lax.broadcasted_iota(I32, (th, POOL_GW), 0)
        pooled = []
        for gi, w in enumerate(POOL_WINDOWS):
            lanes = slice(gi * POOL_GW, (gi + 1) * POOL_GW)
            s = pool_in[r0:r0 + POOL_HIST + th, lanes]
            d = 1
            while d < w:
                s = s + pltpu.roll(s, d, axis=0)
                d *= 2
            cur = pool_in[POOL_HIST + r0:POOL_HIST + r0 + th, lanes]
            cnt = jnp.minimum(pos + 1, w).astype(F32)
            pooled.append((s[POOL_HIST:] / cnt - cur).astype(BF16))

        sh_lo = 0 if r0 == 0 else r0 + CONV_HIST - SUBLANES
        sh_hi = r0 + th + CONV_HIST - SUBLANES
        for r in range(1, SUBLANES):
            conv_sh[r, sh_lo:sh_hi, :] = conv_in[sh_lo + r:sh_hi + r, :]

        for base in range(r0, r0 + th, rc):
            acc = jnp.broadcast_to(cb_ref[...], (rc // SUBLANES, SUBLANES, D_CONV))
            for k in range(CONV_WIDTH):
                q, r = divmod(k + CONV_HIST - CONV_PAD, SUBLANES)
                start = base + q * SUBLANES
                win = conv_sh[r, start:start + rc, :].reshape(rc // SUBLANES, SUBLANES, D_CONV)
                acc = acc + win * dw_ref[k]
            ybuf[base:base + rc, :] = acc.reshape(rc, D_CONV)

        yc = ybuf[rows, :]
        mu = jnp.mean(yc, axis=-1, keepdims=True)
        dev = yc - mu
        var = jnp.mean(dev * dev, axis=-1, keepdims=True)
        yn = dev * lax.rsqrt(var + EPS) * lg_ref[...] + lb_ref[...]
        sy[rows, :] = (yn * _sigmoid(yn)).astype(BF16)

        for j in range(len(POOL_WINDOWS)):
            cs = slice(j * POOL_OUT_GW, (j + 1) * POOL_OUT_GW)
            cs2 = slice(D_MODEL + j * POOL_OUT_GW, D_MODEL + (j + 1) * POOL_OUT_GW)
            ga = _sigmoid(jnp.dot(hbuf[rows, :], gw_ref[:, cs], preferred_element_type=F32) + gb_ref[:, cs])
            gb = _sigmoid(jnp.dot(hbuf[rows, :], gw_ref[:, cs2], preferred_element_type=F32) + gb_ref[:, cs2])
            a_j = jnp.dot(pooled[j], pw_ref[j], preferred_element_type=F32) * ps_ref[:, cs]
            b_j = jnp.dot(sy[rows, :], cwo_ref[:, cs], preferred_element_type=F32)
            merged[rows, cs] = (ga * a_j + gb * b_j).astype(BF16)

        x1 = x + g1 * jnp.dot(merged[rows, :], wo_ref[...], preferred_element_type=F32)
        x1_ref[rows, :] = x1

        ms2 = jnp.mean(x1 * x1, axis=-1, keepdims=True)
        h2 = (x1 * lax.rsqrt(ms2 + EPS)) * scale2 + sh2
        h2b = h2.astype(BF16)
        words = _pack_words(h2)
        for j, ref in enumerate((hp0_ref, hp1_ref, hp2_ref, hp3_ref)):
            ref[rows, :] = words[:, j * LANES:(j + 1) * LANES]

        logits = lax.dot_general(rwt_ref[...], h2b, (((1,), (1,)), ((), ())),
                                 preferred_element_type=F32) + rb_ref[:, 0:1]
        e_iota = lax.broadcasted_iota(I32, (N_EXPERTS, th), 0)
        v = logits
        ids, vals = [], []
        for _ in range(TOP_K):
            m = jnp.max(v, axis=0, keepdims=True)
            idx = jnp.min(jnp.where(v == m, e_iota, N_EXPERTS), axis=0, keepdims=True)
            ids.append(idx)
            vals.append(m)
            v = jnp.where(e_iota == idx, -jnp.inf, v)
        ex = [jnp.exp(vk - vals[0]) for vk in vals]
        den = ex[0] + ex[1] + ex[2] + ex[3]

        sel = [e_iota == idx for idx in ids]
        chosen = jnp.where(sel[0] | sel[1] | sel[2] | sel[3], 1.0, 0.0)
        before = jnp.dot(chosen.astype(BF16), tri_ref[...], preferred_element_type=F32)
        rank_all = cnt_run[:, 0:1] + before
        for k in range(TOP_K):
            ri_ref[0, k:k + 1, rows] = ids[k]
            rk = jnp.sum(jnp.where(sel[k], rank_all, 0.0), axis=0, keepdims=True)
            ri_ref[0, TOP_K + k:TOP_K + k + 1, rows] = rk.astype(I32)
            rw_ref[0, k:k + 1, rows] = ex[k] / den
            rw_ref[0, TOP_K + k:TOP_K + k + 1, rows] = jnp.zeros((1, th), F32)
        cnt_run[...] = cnt_run[...] + jnp.sum(chosen, axis=1, keepdims=True)

    npool_ref[0] = pool_in[tl:tl + POOL_HIST, :]
    nconv_ref[0] = conv_in[tl:tl + CONV_HIST, :]
    conv_in[0:CONV_HIST, :] = conv_in[tl:tl + CONV_HIST, :]
    pool_in[0:POOL_HIST, :] = pool_in[tl:tl + POOL_HIST, :]
    cntout_ref[...] = cnt_run[...]


def _token_mix(x, mod, pool_init, conv_init, cnt_in, wts, *, b0, tl, th, pos0, rc):
    _, seq, _ = x.shape
    bsz = pool_init.shape[0]
    assert seq % tl == 0 and tl % th == 0 and th % rc == 0 and th >= CONV_HIST
    assert th % LANES == 0 or th == tl
    rows_out = bsz * seq
    nl = seq // tl
    x2d = x.reshape(-1, D_MODEL)
    tri = jnp.triu(jnp.ones((th, th), BF16), k=1)
    full = lambda shape: pl.BlockSpec(shape, lambda b, l: (0,) * len(shape))
    per_b = lambda shape: pl.BlockSpec(shape, lambda b, l: (b,) + (0,) * (len(shape) - 1))
    tok = lambda width: pl.BlockSpec((tl, width), lambda b, l: (b * nl + l, 0))
    rt = lambda shape: pl.BlockSpec(shape, lambda b, l: (b * nl + l, 0, 0))
    in_specs = [
        pl.BlockSpec((tl, D_MODEL), lambda b, l: ((b0 + b) * nl + l, 0)),
        pl.BlockSpec((1, 8, D_MODEL), lambda b, l: (b0 + b, 0, 0)),
        per_b((1, POOL_HIST, D_POOL)),
        per_b((1, CONV_HIST, D_CONV)),
        full((N_EXPERTS, LANES)),
        full((th, th)),
        full((1, D_MODEL)), full((1, D_MODEL)),
        full((D_MODEL, D_POOL + 2 * D_CONV)),
        full((len(POOL_WINDOWS), POOL_GW, POOL_OUT_GW)),
        full((1, D_MODEL)),
        full((CONV_WIDTH, SUBLANES, D_CONV)),
        full((1, D_CONV)), full((1, D_CONV)), full((1, D_CONV)),
        full((D_CONV, D_MODEL)),
        full((D_MODEL, 2 * D_MODEL)),
        full((1, 2 * D_MODEL)),
        full((D_MODEL, D_MODEL)),
        full((N_EXPERTS, D_MODEL)),
        full((N_EXPERTS, LANES)),
    ]
    out_shape = (
        jax.ShapeDtypeStruct((rows_out, D_MODEL), F32),
        *[jax.ShapeDtypeStruct((rows_out, LANES), I32)] * N_CHUNKS,
        jax.ShapeDtypeStruct((bsz * nl, 2 * TOP_K, tl), I32),
        jax.ShapeDtypeStruct((bsz * nl, 2 * TOP_K, tl), F32),
        jax.ShapeDtypeStruct((bsz, POOL_HIST, D_POOL), F32),
        jax.ShapeDtypeStruct((bsz, CONV_HIST, D_CONV), F32),
        jax.ShapeDtypeStruct((N_EXPERTS, LANES), F32),
    )
    out_specs = (
        tok(D_MODEL),
        *[tok(LANES)] * N_CHUNKS,
        rt((1, 2 * TOP_K, tl)),
        rt((1, 2 * TOP_K, tl)),
        per_b((1, POOL_HIST, D_POOL)),
        per_b((1, CONV_HIST, D_CONV)),
        full((N_EXPERTS, LANES)),
    )
    scratch = [
        pltpu.VMEM((SUBLANES, tl + CONV_HIST, D_CONV), F32),
        pltpu.VMEM((tl + POOL_HIST, D_POOL), F32),
        pltpu.VMEM((tl, D_MODEL), BF16),
        pltpu.VMEM((tl, D_MODEL), BF16),
        pltpu.VMEM((tl, D_CONV), F32),
        pltpu.VMEM((tl, D_CONV), BF16),
        pltpu.VMEM((N_EXPERTS, LANES), F32),
    ]
    return pl.pallas_call(
        functools.partial(_mix_kernel, tl=tl, th=th, pos0=pos0, rc=rc),
        out_shape=out_shape,
        grid=(bsz, nl),
        in_specs=in_specs,
        out_specs=out_specs,
        scratch_shapes=scratch,
        compiler_params=pltpu.CompilerParams(
            dimension_semantics=("arbitrary", "arbitrary"), vmem_limit_bytes=VMEM_LIMIT_BYTES),
        name="token_mix",
    )(x2d, mod, pool_init, conv_init, cnt_in, tri, *wts)


def _sc_mesh():
    return plsc.VectorSubcoreMesh(core_axis_name="c", subcore_axis_name="s",
                                  num_cores=SC_CORES, num_subcores=SC_SUBCORES)


def _for_each_chunk(n_tok, fn):
    wid = lax.axis_index("s") * SC_CORES + lax.axis_index("c")
    if n_tok % (8 * SC_WORKERS) == 0 and n_tok // SC_WORKERS >= SC_ROWS:
        per_w = n_tok // SC_WORKERS

        @pl.loop(0, pl.cdiv(per_w, SC_ROWS))
        def _(c):
            fn(pl.multiple_of(wid * per_w + jnp.minimum(c * SC_ROWS, per_w - SC_ROWS), 8))
    else:
        assert n_tok % SC_ROWS == 0 and n_tok // SC_ROWS <= SC_WORKERS

        @pl.when(wid < n_tok // SC_ROWS)
        def _():
            fn(pl.multiple_of(wid * SC_ROWS, SC_ROWS))


_SC_SCRATCH = ([pltpu.VMEM((SC_ROWS, LANES), I32)] * N_CHUNKS
               + [pltpu.VMEM((SC_ROWS,), I32)] * TOP_K
               + [pltpu.SemaphoreType.DMA, pltpu.SemaphoreType.DMA])


def _dispatch(h_groups, pos_groups, n_rows):
    n_g = len(h_groups)

    def body(*refs):
        refs = list(refs)
        h = [[refs.pop(0) for _ in range(N_CHUNKS)] for _ in range(n_g)]
        p = [[refs.pop(0) for _ in range(TOP_K)] for _ in range(n_g)]
        o = [refs.pop(0) for _ in range(N_CHUNKS)]
        rows = [refs.pop(0) for _ in range(N_CHUNKS)]
        idx = [refs.pop(0) for _ in range(TOP_K)]
        sem_in, sem_out = refs

        for hg, pg in zip(h, p):
            def move(s, hg=hg, pg=pg):
                loads = [pltpu.async_copy(hg[j].at[pl.ds(s, SC_ROWS)], rows[j], sem_in)
                         for j in range(N_CHUNKS)]
                loads += [pltpu.async_copy(pg[k].at[pl.ds(s, SC_ROWS)], idx[k], sem_in)
                          for k in range(TOP_K)]
                for cp in loads:
                    cp.wait()
                stores = [pltpu.async_copy(rows[j], o[j].at[idx[k]], sem_out)
                          for j in range(N_CHUNKS) for k in range(TOP_K)]
                for cp in stores:
                    cp.wait()

            _for_each_chunk(hg[0].shape[0], move)

    call = pl.kernel(
        body,
        out_type=tuple(jax.ShapeDtypeStruct((n_rows, LANES), I32) for _ in range(N_CHUNKS)),
        mesh=_sc_mesh(), scratch_types=_SC_SCRATCH, name="sc_dispatch")
    flat = [a for g in h_groups for a in g] + [a for g in pos_groups for a in g]
    return call(*flat)


def _combine_gather(y_chunks, pos_groups):
    n_g = len(pos_groups)

    def body(*refs):
        refs = list(refs)
        y = [refs.pop(0) for _ in range(N_CHUNKS)]
        p = [[refs.pop(0) for _ in range(TOP_K)] for _ in range(n_g)]
        g = [refs.pop(0) for _ in range(n_g)]
        rows = [refs.pop(0) for _ in range(N_CHUNKS)]
        idx = [refs.pop(0) for _ in range(TOP_K)]
        sem_in, sem_out = refs

        for pg, gg in zip(p, g):
            def move(s, pg=pg, gg=gg):
                loads = [pltpu.async_copy(pg[k].at[pl.ds(s, SC_ROWS)], idx[k], sem_in)
                         for k in range(TOP_K)]
                for cp in loads:
                    cp.wait()
                for k in range(TOP_K):
                    gathers = [pltpu.async_copy(y[j].at[idx[k]], rows[j], sem_in)
                               for j in range(N_CHUNKS)]
                    for cp in gathers:
                        cp.wait()
                    stores = [pltpu.async_copy(rows[j], gg.at[k * N_CHUNKS + j, pl.ds(s, SC_ROWS)], sem_out)
                              for j in range(N_CHUNKS)]
                    for cp in stores:
                        cp.wait()

            _for_each_chunk(pg[0].shape[0], move)

    call = pl.kernel(
        body,
        out_type=tuple(jax.ShapeDtypeStruct((TOP_K * N_CHUNKS, pg[0].shape[0], LANES), I32)
                       for pg in pos_groups),
        mesh=_sc_mesh(), scratch_types=_SC_SCRATCH, name="sc_combine_gather")
    return call(*y_chunks, *[a for g in pos_groups for a in g])


def _expert_kernel(te_ref, tr_ref, x0_ref, x1_ref, x2_ref, x3_ref,
                   wg_ref, bg_ref, wu_ref, bu_ref, wd_ref, bd_ref,
                   y0_ref, y1_ref, y2_ref, y3_ref, wg_bf, wu_bf, wd_bf):
    i = pl.program_id(0)

    @pl.when((i == 0) | (te_ref[i] != te_ref[jnp.maximum(i - 1, 0)]))
    def _():
        wg_bf[...] = wg_ref[0].astype(BF16)
        wu_bf[...] = wu_ref[0].astype(BF16)
        wd_bf[...] = wd_ref[0].astype(BF16)

    @pl.when(tr_ref[i] > 0)
    def _():
        words = jnp.concatenate([x0_ref[...], x1_ref[...], x2_ref[...], x3_ref[...]], axis=1)
        hi, lo = _unpack_words(words)
        xt = jnp.concatenate([hi, lo], axis=1).astype(BF16)
        gt = jnp.minimum(jnp.dot(xt, wg_bf[...], preferred_element_type=F32) + bg_ref[0], SWIGLU_LIMIT)
        up = jnp.clip(jnp.dot(xt, wu_bf[...], preferred_element_type=F32) + bu_ref[0],
                      -SWIGLU_LIMIT, SWIGLU_LIMIT)
        act = gt * _sigmoid(SWIGLU_ALPHA * gt) * (up + 1.0)
        y = jnp.dot(act.astype(BF16), wd_bf[...], preferred_element_type=F32) + bd_ref[0]
        out = _pack_words(y)
        for j, ref in enumerate((y0_ref, y1_ref, y2_ref, y3_ref)):
            ref[...] = out[:, j * LANES:(j + 1) * LANES]


def _expert_ffn(xs_chunks, tile_expert, tile_rows, ew):
    n_rows = xs_chunks[0].shape[0]
    nt_max = n_rows // MOE_TILE
    wg, bg, wu, bu, wd, bd = ew
    rows = pl.BlockSpec((MOE_TILE, LANES), lambda i, te, nt: (i, 0))
    mat = pl.BlockSpec((1, D_MODEL, D_MODEL), lambda i, te, nt: (te[i], 0, 0))
    vec = pl.BlockSpec((1, 1, D_MODEL), lambda i, te, nt: (te[i], 0, 0))
    grid_spec = pltpu.PrefetchScalarGridSpec(
        num_scalar_prefetch=2,
        grid=(nt_max,),
        in_specs=[rows] * N_CHUNKS + [mat, vec, mat, vec, mat, vec],
        out_specs=[rows] * N_CHUNKS,
        scratch_shapes=[pltpu.VMEM((D_MODEL, D_MODEL), BF16)] * 3,
    )
    return pl.pallas_call(
        _expert_kernel,
        out_shape=[jax.ShapeDtypeStruct((n_rows, LANES), I32)] * N_CHUNKS,
        grid_spec=grid_spec,
        compiler_params=pltpu.CompilerParams(
            dimension_semantics=("arbitrary",), vmem_limit_bytes=VMEM_LIMIT_BYTES),
        name="expert_ffn",
    )(tile_expert, tile_rows, *xs_chunks, wg, bg, wu, bu, wd, bd)


def _final_kernel(x1_ref, g_ref, w_ref, mod_ref, fg_ref, *rest):
    y_ref = rest[-1]
    wt = w_ref[...]
    acc = None
    for k in range(TOP_K):
        words = jnp.concatenate([g_ref[k * N_CHUNKS + j] for j in range(N_CHUNKS)], axis=1)
        hi, lo = _unpack_words(words)
        term = wt[:, k:k + 1] * jnp.concatenate([hi, lo], axis=1)
        acc = term if acc is None else acc + term
    g2 = mod_ref[0][5:6]
    x2 = x1_ref[...] + g2 * acc
    ms = jnp.mean(x2 * x2, axis=-1, keepdims=True)
    y_ref[...] = x2 * lax.rsqrt(ms + EPS) * fg_ref[...]


def _final(x1, gathered, wt_tok, mod, final_g, *, tl, tiles_per_batch, b0=0, out_rows=None, y_prev=None):
    n_tok = x1.shape[0]
    out_rows = n_tok if out_rows is None else out_rows
    blk0 = b0 * tiles_per_batch
    in_specs = [
        pl.BlockSpec((tl, D_MODEL), lambda i: (i, 0)),
        pl.BlockSpec((TOP_K * N_CHUNKS, tl, LANES), lambda i: (0, i, 0)),
        pl.BlockSpec((tl, 2 * TOP_K), lambda i: (i, 0)),
        pl.BlockSpec((1, 8, D_MODEL), lambda i: (b0 + i // tiles_per_batch, 0, 0)),
        pl.BlockSpec((1, D_MODEL), lambda i: (0, 0)),
    ]
    args = [x1, gathered, wt_tok, mod, final_g.reshape(1, D_MODEL)]
    aliases = {}
    if y_prev is not None:
        in_specs.append(pl.BlockSpec(memory_space=pl.ANY))
        args.append(y_prev)
        aliases = {len(args) - 1: 0}
    return pl.pallas_call(
        _final_kernel,
        out_shape=jax.ShapeDtypeStruct((out_rows, D_MODEL), F32),
        grid=(n_tok // tl,),
        in_specs=in_specs,
        out_specs=pl.BlockSpec((tl, D_MODEL), lambda i: (blk0 + i, 0)),
        input_output_aliases=aliases,
        compiler_params=pltpu.CompilerParams(
            dimension_semantics=("arbitrary",), vmem_limit_bytes=VMEM_LIMIT_BYTES),
        name="combine_final",
    )(*args)


def _route_rows(r, lo, hi):
    return r[:, lo:hi, :].transpose(1, 0, 2).reshape(hi - lo, -1)


def kernel(x_prompt, x_sample, state_pool, state_conv, c_prompt, c_sample, norm1_g, norm2_g, final_g, w_ada, b_ada, w_in, pool_w, pool_scale, conv_dw, conv_b, conv_ln_g, conv_ln_b, conv_w_out, gate_w, gate_b, w_out, router_w, router_b, exp_w_gate, exp_b_gate, exp_w_up, exp_b_up, exp_w_down, exp_b_down):
    assert norm1_g.shape[0] == 1, "single-layer trunk"
    row = lambda v: v.reshape(1, -1)
    mix_w = (
        row(norm1_g[0]), row(norm2_g[0]),
        w_in[0].astype(BF16), pool_w[0].astype(BF16), row(pool_scale[0]),
        jnp.broadcast_to(conv_dw[0][:, None, :], (CONV_WIDTH, SUBLANES, D_CONV)),
        row(conv_b[0]), row(conv_ln_g[0]), row(conv_ln_b[0]),
        conv_w_out[0].astype(BF16), gate_w[0].astype(BF16), row(gate_b[0]),
        w_out[0].astype(BF16), router_w[0].T.astype(BF16),
        jnp.broadcast_to(router_b[0][:, None], (N_EXPERTS, LANES)),
    )
    ew = (
        exp_w_gate[0], exp_b_gate[0][:, None, :],
        exp_w_up[0], exp_b_up[0][:, None, :],
        exp_w_down[0], exp_b_down[0][:, None, :],
    )
    bp, lp, _ = x_prompt.shape
    bs, ls, _ = x_sample.shape
    t_p = bp * lp
    tl_p, tl_s = 512, ls
    nb_small = max(bp // PROMPT_TAIL_FRACTION, 1)
    chunk_nb = [nb_small, bp - nb_small] if bp > nb_small else [bp]
    chunk_b0 = [sum(chunk_nb[:c]) for c in range(len(chunk_nb))]
    n_chunks = len(chunk_nb)

    mod_p = _modulation(c_prompt, w_ada[0], b_ada[0])
    mod_s = _modulation(c_sample, w_ada[0], b_ada[0])
    pad_state = lambda s, hist: jnp.pad(s, ((0, 0), (hist - s.shape[1], 0), (0, 0)))
    split = lambda out: (out[0], out[1:1 + N_CHUNKS], *out[1 + N_CHUNKS:])

    zero_cnt = jnp.zeros((N_EXPERTS, LANES), F32)
    mixed = [split(_token_mix(
        x_prompt, mod_p, jnp.zeros((nb, POOL_HIST, D_POOL), F32), jnp.zeros((nb, CONV_HIST, D_CONV), F32),
        zero_cnt, mix_w, b0=b0, tl=tl_p, th=tl_p, pos0=0, rc=32))
        for b0, nb in zip(chunk_b0, chunk_nb)]
    x1_s, hp_s, ri_s, rw_s, npool_s, nconv_s, cnt_last = split(_token_mix(
        x_sample, mod_s, pad_state(state_pool[0], POOL_HIST), pad_state(state_conv[0], CONV_HIST),
        mixed[-1][6], mix_w, b0=0, tl=tl_s, th=tl_s, pos0=PAST_LEN, rc=32))

    def finish(c, g_c, y_prev):
        x1_c, rw_c = mixed[c][0], mixed[c][3]
        return _final(x1_c, g_c, _route_rows(rw_c, 0, 2 * TOP_K).T, mod_p, final_g, tl=tl_p,
                      tiles_per_batch=lp // tl_p, b0=chunk_b0[c], out_rows=t_p, y_prev=y_prev)

    y_p = None
    gathered = []
    for c, (x1_c, hp_c, ri_c, rw_c, _, _, cnt_c) in enumerate(mixed):
        last = c == n_chunks - 1
        n_tok = chunk_nb[c] * lp + (bs * ls if last else 0)
        n_rows = (TOP_K * n_tok // MOE_TILE + N_EXPERTS) * MOE_TILE
        counts = (cnt_last if last else cnt_c)[:, 0].astype(I32)
        tiles_e = (counts + MOE_TILE - 1) // MOE_TILE
        tile_end = jnp.cumsum(tiles_e)
        tile_start = tile_end - tiles_e
        row_off = tile_start * MOE_TILE
        tile_id = jnp.arange(n_rows // MOE_TILE, dtype=I32)
        tile_expert = jnp.minimum(
            jnp.sum(tile_id[:, None] >= tile_end[None, :], axis=1), N_EXPERTS - 1).astype(I32)
        is_e = tile_expert[:, None] == jnp.arange(N_EXPERTS, dtype=I32)[None, :]
        left = jnp.sum(jnp.where(is_e, (counts - (tile_id[:, None] - tile_start[None, :]) * MOE_TILE), 0), axis=1)
        tile_rows = jnp.clip(left, 0, MOE_TILE).astype(I32)

        def slots(ri, row_off=row_off):
            ids, ranks = _route_rows(ri, 0, TOP_K), _route_rows(ri, TOP_K, 2 * TOP_K)
            pos = ranks
            for e in range(N_EXPERTS):
                pos = pos + jnp.where(ids == e, row_off[e], 0)
            return [pos[k] for k in range(TOP_K)]

        h_groups = [hp_c] + ([hp_s] if last else [])
        pos_groups = [slots(ri_c)] + ([slots(ri_s)] if last else [])
        xs = _dispatch(h_groups, pos_groups, n_rows)
        ys = _expert_ffn(xs, tile_expert, tile_rows, ew)
        if c > 0:
            y_p = finish(c - 1, gathered[c - 1][0], y_p)
        gathered.append(_combine_gather(ys, pos_groups))

    y_p = finish(n_chunks - 1, gathered[-1][0], y_p)
    y_s = _final(x1_s, gathered[-1][1], _route_rows(rw_s, 0, 2 * TOP_K).T, mod_s, final_g, tl=tl_s,
                 tiles_per_batch=1)
    unpad = lambda s, n: s[:, s.shape[1] - n:][None]
    npool_p = jnp.concatenate([m[4] for m in mixed], axis=0)
    nconv_p = jnp.concatenate([m[5] for m in mixed], axis=0)
    return (y_p.reshape(bp, lp, D_MODEL), y_s.reshape(bs, ls, D_MODEL),
            unpad(npool_p, POOL_PAD), unpad(nconv_p, CONV_PAD),
            unpad(npool_s, POOL_PAD), unpad(nconv_s, CONV_PAD))
```

```python
import functools

import jax
import jax.numpy as jnp
from jax import lax
from jax.experimental import pallas as pl
from jax.experimental.pallas import tpu as pltpu
from jax.experimental.pallas import tpu_sc as plsc

D_MODEL = 1024
D_POOL = 512
D_CONV = 512
POOL_WINDOWS = (2, 4, 8, 16)
POOL_GW = 128
POOL_OUT_GW = 256
POOL_PAD = 15
CONV_WIDTH = 31
CONV_PAD = 30
N_EXPERTS = 32
TOP_K = 4
SWIGLU_LIMIT = 7.0
SWIGLU_ALPHA = 1.702
EPS = 1e-6
PAST_LEN = 2048

POOL_HIST = 16
CONV_HIST = 32
VMEM_LIMIT_BYTES = 56 * 1024 * 1024

LANES = 128
SUBLANES = 8
D_WORDS = D_MODEL // 2
N_CHUNKS = D_WORDS // LANES
MOE_TILE = 512
PROMPT_TAIL_FRACTION = 4
SC_CORES = 2
SC_SUBCORES = 16
SC_WORKERS = SC_CORES * SC_SUBCORES
SC_ROWS = 128

F32 = jnp.float32
BF16 = jnp.bfloat16
I32 = jnp.int32
HI_MASK = -65536
NEG_LOG2E = -1.4426950408889634


def _sigmoid(v):
    return 1.0 / (1.0 + jnp.exp2(v * NEG_LOG2E))


def _pack_words(v):
    r = v.astype(BF16).astype(F32)
    hi = lax.bitcast_convert_type(r[:, :D_WORDS], I32)
    lo = lax.bitcast_convert_type(r[:, D_WORDS:], I32)
    return (hi & HI_MASK) | lax.shift_right_logical(lo, 16)


def _unpack_words(w):
    hi = lax.bitcast_convert_type(w & HI_MASK, F32)
    lo = lax.bitcast_convert_type(lax.shift_left(w, 16), F32)
    return hi, lo


def _mod_kernel(c_ref, w_ref, b_ref, o_ref):
    c = c_ref[...]
    s = (c * _sigmoid(c)).astype(BF16)
    o_ref[...] = jnp.dot(s, w_ref[...].astype(BF16), preferred_element_type=F32) + b_ref[...]


def _modulation(c, w_ada, b_ada):
    bsz = c.shape[0]
    out = pl.pallas_call(
        _mod_kernel,
        out_shape=jax.ShapeDtypeStruct((bsz, 6 * D_MODEL), F32),
        grid=(6,),
        in_specs=[
            pl.BlockSpec((bsz, D_MODEL), lambda j: (0, 0)),
            pl.BlockSpec((D_MODEL, D_MODEL), lambda j: (0, j)),
            pl.BlockSpec((1, D_MODEL), lambda j: (0, j)),
        ],
        out_specs=pl.BlockSpec((bsz, D_MODEL), lambda j: (0, j)),
        compiler_params=pltpu.CompilerParams(
            dimension_semantics=("arbitrary",), vmem_limit_bytes=VMEM_LIMIT_BYTES),
        name="adaln_mod",
    )(c, w_ada, b_ada.reshape(1, 6 * D_MODEL))
    out = out.reshape(bsz, 6, D_MODEL)
    return jnp.pad(out, ((0, 0), (0, 2), (0, 0)))


def _mix_kernel(x_ref, mod_ref, pinit_ref, cinit_ref, cntin_ref, tri_ref,
                n1_ref, n2_ref, win_ref, pw_ref, ps_ref, dw_ref, cb_ref, lg_ref, lb_ref,
                cwo_ref, gw_ref, gb_ref, wo_ref, rwt_ref, rb_ref,
                x1_ref, hp0_ref, hp1_ref, hp2_ref, hp3_ref, ri_ref, rw_ref,
                npool_ref, nconv_ref, cntout_ref,
                conv_sh, pool_in, hbuf, merged, ybuf, sy, cnt_run, *, tl, th, pos0, rc):
    b = pl.program_id(0)
    l = pl.program_id(1)

    conv_in = conv_sh.at[0]

    @pl.when(l == 0)
    def _():
        conv_in[0:CONV_HIST, :] = cinit_ref[0]
        pool_in[0:POOL_HIST, :] = pinit_ref[0]

    @pl.when((b == 0) & (l == 0))
    def _():
        cnt_run[...] = cntin_ref[...]

    mod = mod_ref[0]
    sh1, sc1, g1 = mod[0:1], mod[1:2], mod[2:3]
    sh2, sc2 = mod[3:4], mod[4:5]
    scale1 = n1_ref[...] * (1.0 + sc1)
    scale2 = n2_ref[...] * (1.0 + sc2)

    for r0 in range(0, tl, th):
        rows = slice(r0, r0 + th)
        x = x_ref[rows, :]
        ms = jnp.mean(x * x, axis=-1, keepdims=True)
        hbuf[rows, :] = ((x * lax.rsqrt(ms + EPS)) * scale1 + sh1).astype(BF16)

        proj = jnp.dot(hbuf[rows, :], win_ref[...], preferred_element_type=F32)
        glu = proj[:, D_POOL:D_POOL + D_CONV] * _sigmoid(proj[:, D_POOL + D_CONV:])
        pool_in[POOL_HIST + r0:POOL_HIST + r0 + th, :] = proj[:, :D_POOL]
        conv_in[CONV_HIST + r0:CONV_HIST + r0 + th, :] = glu

        pos = pos0 + l * tl + r0 + lax.broadcasted_iota(I32, (th, POOL_GW), 0)
        pooled = []
        for gi, w in enumerate(POOL_WINDOWS):
            lanes = slice(gi * POOL_GW, (gi + 1) * POOL_GW)
            s = pool_in[r0:r0 + POOL_HIST + th, lanes]
            d = 1
            while d < w:
                s = s + pltpu.roll(s, d, axis=0)
                d *= 2
            cur = pool_in[POOL_HIST + r0:POOL_HIST + r0 + th, lanes]
            cnt = jnp.minimum(pos + 1, w).astype(F32)
            pooled.append((s[POOL_HIST:] / cnt - cur).astype(BF16))

        sh_lo = 0 if r0 == 0 else r0 + CONV_HIST - SUBLANES
        sh_hi = r0 + th + CONV_HIST - SUBLANES
        for r in range(1, SUBLANES):
            conv_sh[r, sh_lo:sh_hi, :] = conv_in[sh_lo + r:sh_hi + r, :]

        for base in range(r0, r0 + th, rc):
            acc = jnp.broadcast_to(cb_ref[...], (rc // SUBLANES, SUBLANES, D_CONV))
            for k in range(CONV_WIDTH):
                q, r = divmod(k + CONV_HIST - CONV_PAD, SUBLANES)
                start = base + q * SUBLANES
                win = conv_sh[r, start:start + rc, :].reshape(rc // SUBLANES, SUBLANES, D_CONV)
                acc = acc + win * dw_ref[k]
            ybuf[base:base + rc, :] = acc.reshape(rc, D_CONV)

        yc = ybuf[rows, :]
        mu = jnp.mean(yc, axis=-1, keepdims=True)
        dev = yc - mu
        var = jnp.mean(dev * dev, axis=-1, keepdims=True)
        yn = dev * lax.rsqrt(var + EPS) * lg_ref[...] + lb_ref[...]
        sy[rows, :] = (yn * _sigmoid(yn)).astype(BF16)

        for j in range(len(POOL_WINDOWS)):
            cs = slice(j * POOL_OUT_GW, (j + 1) * POOL_OUT_GW)
            cs2 = slice(D_MODEL + j * POOL_OUT_GW, D_MODEL + (j + 1) * POOL_OUT_GW)
            ga = _sigmoid(jnp.dot(hbuf[rows, :], gw_ref[:, cs], preferred_element_type=F32) + gb_ref[:, cs])
            gb = _sigmoid(jnp.dot(hbuf[rows, :], gw_ref[:, cs2], preferred_element_type=F32) + gb_ref[:, cs2])
            a_j = jnp.dot(pooled[j], pw_ref[j], preferred_element_type=F32) * ps_ref[:, cs]
            b_j = jnp.dot(sy[rows, :], cwo_ref[:, cs], preferred_element_type=F32)
            merged[rows, cs] = (ga * a_j + gb * b_j).astype(BF16)

        x1 = x + g1 * jnp.dot(merged[rows, :], wo_ref[...], preferred_element_type=F32)
        x1_ref[rows, :] = x1

        ms2 = jnp.mean(x1 * x1, axis=-1, keepdims=True)
        h2 = (x1 * lax.rsqrt(ms2 + EPS)) * scale2 + sh2
        h2b = h2.astype(BF16)
        words = _pack_words(h2)
        for j, ref in enumerate((hp0_ref, hp1_ref, hp2_ref, hp3_ref)):
            ref[rows, :] = words[:, j * LANES:(j + 1) * LANES]

        logits = lax.dot_general(rwt_ref[...], h2b, (((1,), (1,)), ((), ())),
                                 preferred_element_type=F32) + rb_ref[:, 0:1]
        e_iota = lax.broadcasted_iota(I32, (N_EXPERTS, th), 0)
        v = logits
        ids, vals = [], []
        for _ in range(TOP_K):
            m = jnp.max(v, axis=0, keepdims=True)
            idx = jnp.min(jnp.where(v == m, e_iota, N_EXPERTS), axis=0, keepdims=True)
            ids.append(idx)
            vals.append(m)
            v = jnp.where(e_iota == idx, -jnp.inf, v)
        ex = [jnp.exp(vk - vals[0]) for vk in vals]
        den = ex[0] + ex[1] + ex[2] + ex[3]

        sel = [e_iota == idx for idx in ids]
        chosen = jnp.where(sel[0] | sel[1] | sel[2] | sel[3], 1.0, 0.0)
        before = jnp.dot(chosen.astype(BF16), tri_ref[...], preferred_element_type=F32)
        rank_all = cnt_run[:, 0:1] + before
        for k in range(TOP_K):
            ri_ref[0, k:k + 1, rows] = ids[k]
            rk = jnp.sum(jnp.where(sel[k], rank_all, 0.0), axis=0, keepdims=True)
            ri_ref[0, TOP_K + k:TOP_K + k + 1, rows] = rk.astype(I32)
            rw_ref[0, k:k + 1, rows] = ex[k] / den
            rw_ref[0, TOP_K + k:TOP_K + k + 1, rows] = jnp.zeros((1, th), F32)
        cnt_run[...] = cnt_run[...] + jnp.sum(chosen, axis=1, keepdims=True)

    npool_ref[0] = pool_in[tl:tl + POOL_HIST, :]
    nconv_ref[0] = conv_in[tl:tl + CONV_HIST, :]
    conv_in[0:CONV_HIST, :] = conv_in[tl:tl + CONV_HIST, :]
    pool_in[0:POOL_HIST, :] = pool_in[tl:tl + POOL_HIST, :]
    cntout_ref[...] = cnt_run[...]


def _token_mix(x, mod, pool_init, conv_init, cnt_in, wts, *, b0, tl, th, pos0, rc):
    _, seq, _ = x.shape
    bsz = pool_init.shape[0]
    assert seq % tl == 0 and tl % th == 0 and th % rc == 0 and th >= CONV_HIST
    assert th % LANES == 0 or th == tl
    rows_out = bsz * seq
    nl = seq // tl
    x2d = x.reshape(-1, D_MODEL)
    tri = jnp.triu(jnp.ones((th, th), BF16), k=1)
    full = lambda shape: pl.BlockSpec(shape, lambda b, l: (0,) * len(shape))
    per_b = lambda shape: pl.BlockSpec(shape, lambda b, l: (b,) + (0,) * (len(shape) - 1))
    tok = lambda width: pl.BlockSpec((tl, width), lambda b, l: (b * nl + l, 0))
    rt = lambda shape: pl.BlockSpec(shape, lambda b, l: (b * nl + l, 0, 0))
    in_specs = [
        pl.BlockSpec((tl, D_MODEL), lambda b, l: ((b0 + b) * nl + l, 0)),
        pl.BlockSpec((1, 8, D_MODEL), lambda b, l: (b0 + b, 0, 0)),
        per_b((1, POOL_HIST, D_POOL)),
        per_b((1, CONV_HIST, D_CONV)),
        full((N_EXPERTS, LANES)),
        full((th, th)),
        full((1, D_MODEL)), full((1, D_MODEL)),
        full((D_MODEL, D_POOL + 2 * D_CONV)),
        full((len(POOL_WINDOWS), POOL_GW, POOL_OUT_GW)),
        full((1, D_MODEL)),
        full((CONV_WIDTH, SUBLANES, D_CONV)),
        full((1, D_CONV)), full((1, D_CONV)), full((1, D_CONV)),
        full((D_CONV, D_MODEL)),
        full((D_MODEL, 2 * D_MODEL)),
        full((1, 2 * D_MODEL)),
        full((D_MODEL, D_MODEL)),
        full((N_EXPERTS, D_MODEL)),
        full((N_EXPERTS, LANES)),
    ]
    out_shape = (
        jax.ShapeDtypeStruct((rows_out, D_MODEL), F32),
        *[jax.ShapeDtypeStruct((rows_out, LANES), I32)] * N_CHUNKS,
        jax.ShapeDtypeStruct((bsz * nl, 2 * TOP_K, tl), I32),
        jax.ShapeDtypeStruct((bsz * nl, 2 * TOP_K, tl), F32),
        jax.ShapeDtypeStruct((bsz, POOL_HIST, D_POOL), F32),
        jax.ShapeDtypeStruct((bsz, CONV_HIST, D_CONV), F32),
        jax.ShapeDtypeStruct((N_EXPERTS, LANES), F32),
    )
    out_specs = (
        tok(D_MODEL),
        *[tok(LANES)] * N_CHUNKS,
        rt((1, 2 * TOP_K, tl)),
        rt((1, 2 * TOP_K, tl)),
        per_b((1, POOL_HIST, D_POOL)),
        per_b((1, CONV_HIST, D_CONV)),
        full((N_EXPERTS, LANES)),
    )
    scratch = [
        pltpu.VMEM((SUBLANES, tl + CONV_HIST, D_CONV), F32),
        pltpu.VMEM((tl + POOL_HIST, D_POOL), F32),
        pltpu.VMEM((tl, D_MODEL), BF16),
        pltpu.VMEM((tl, D_MODEL), BF16),
        pltpu.VMEM((tl, D_CONV), F32),
        pltpu.VMEM((tl, D_CONV), BF16),
        pltpu.VMEM((N_EXPERTS, LANES), F32),
    ]
    return pl.pallas_call(
        functools.partial(_mix_kernel, tl=tl, th=th, pos0=pos0, rc=rc),
        out_shape=out_shape,
        grid=(bsz, nl),
        in_specs=in_specs,
        out_specs=out_specs,
        scratch_shapes=scratch,
        compiler_params=pltpu.CompilerParams(
            dimension_semantics=("arbitrary", "arbitrary"), vmem_limit_bytes=VMEM_LIMIT_BYTES),
        name="token_mix",
    )(x2d, mod, pool_init, conv_init, cnt_in, tri, *wts)


def _sc_mesh():
    return plsc.VectorSubcoreMesh(core_axis_name="c", subcore_axis_name="s",
                                  num_cores=SC_CORES, num_subcores=SC_SUBCORES)


def _for_each_chunk(n_tok, fn):
    wid = lax.axis_index("s") * SC_CORES + lax.axis_index("c")
    if n_tok % (8 * SC_WORKERS) == 0 and n_tok // SC_WORKERS >= SC_ROWS:
        per_w = n_tok // SC_WORKERS

        @pl.loop(0, pl.cdiv(per_w, SC_ROWS))
        def _(c):
            fn(pl.multiple_of(wid * per_w + jnp.minimum(c * SC_ROWS, per_w - SC_ROWS), 8))
    else:
        assert n_tok % SC_ROWS == 0 and n_tok // SC_ROWS <= SC_WORKERS

        @pl.when(wid < n_tok // SC_ROWS)
        def _():
            fn(pl.multiple_of(wid * SC_ROWS, SC_ROWS))


_SC_SCRATCH = ([pltpu.VMEM((SC_ROWS, LANES), I32)] * N_CHUNKS
               + [pltpu.VMEM((SC_ROWS,), I32)] * TOP_K
               + [pltpu.SemaphoreType.DMA, pltpu.SemaphoreType.DMA])


def _dispatch(h_groups, pos_groups, n_rows):
    n_g = len(h_groups)

    def body(*refs):
        refs = list(refs)
        h = [[refs.pop(0) for _ in range(N_CHUNKS)] for _ in range(n_g)]
        p = [[refs.pop(0) for _ in range(TOP_K)] for _ in range(n_g)]
        o = [refs.pop(0) for _ in range(N_CHUNKS)]
        rows = [refs.pop(0) for _ in range(N_CHUNKS)]
        idx = [refs.pop(0) for _ in range(TOP_K)]
        sem_in, sem_out = refs

        for hg, pg in zip(h, p):
            def move(s, hg=hg, pg=pg):
                loads = [pltpu.async_copy(hg[j].at[pl.ds(s, SC_ROWS)], rows[j], sem_in)
                         for j in range(N_CHUNKS)]
                loads += [pltpu.async_copy(pg[k].at[pl.ds(s, SC_ROWS)], idx[k], sem_in)
                          for k in range(TOP_K)]
                for cp in loads:
                    cp.wait()
                stores = [pltpu.async_copy(rows[j], o[j].at[idx[k]], sem_out)
                          for j in range(N_CHUNKS) for k in range(TOP_K)]
                for cp in stores:
                    cp.wait()

            _for_each_chunk(hg[0].shape[0], move)

    call = pl.kernel(
        body,
        out_type=tuple(jax.ShapeDtypeStruct((n_rows, LANES), I32) for _ in range(N_CHUNKS)),
        mesh=_sc_mesh(), scratch_types=_SC_SCRATCH, name="sc_dispatch")
    flat = [a for g in h_groups for a in g] + [a for g in pos_groups for a in g]
    return call(*flat)


def _combine_gather(y_chunks, pos_groups):
    n_g = len(pos_groups)

    def body(*refs):
        refs = list(refs)
        y = [refs.pop(0) for _ in range(N_CHUNKS)]
        p = [[refs.pop(0) for _ in range(TOP_K)] for _ in range(n_g)]
        g = [refs.pop(0) for _ in range(n_g)]
        rows = [refs.pop(0) for _ in range(N_CHUNKS)]
        idx = [refs.pop(0) for _ in range(TOP_K)]
        sem_in, sem_out = refs

        for pg, gg in zip(p, g):
            def move(s, pg=pg, gg=gg):
                loads = [pltpu.async_copy(pg[k].at[pl.ds(s, SC_ROWS)], idx[k], sem_in)
                         for k in range(TOP_K)]
                for cp in loads:
                    cp.wait()
                for k in range(TOP_K):
                    gathers = [pltpu.async_copy(y[j].at[idx[k]], rows[j], sem_in)
                               for j in range(N_CHUNKS)]
                    for cp in gathers:
                        cp.wait()
                    stores = [pltpu.async_copy(rows[j], gg.at[k * N_CHUNKS + j, pl.ds(s, SC_ROWS)], sem_out)
                              for j in range(N_CHUNKS)]
                    for cp in stores:
                        cp.wait()

            _for_each_chunk(pg[0].shape[0], move)

    call = pl.kernel(
        body,
        out_type=tuple(jax.ShapeDtypeStruct((TOP_K * N_CHUNKS, pg[0].shape[0], LANES), I32)
                       for pg in pos_groups),
        mesh=_sc_mesh(), scratch_types=_SC_SCRATCH, name="sc_combine_gather")
    return call(*y_chunks, *[a for g in pos_groups for a in g])


def _expert_kernel(te_ref, tr_ref, nx_ref, x0_ref, x1_ref, x2_ref, x3_ref,
                   wg_hbm, bg_ref, wu_hbm, bu_ref, wd_hbm, bd_ref,
                   y0_ref, y1_ref, y2_ref, y3_ref, stage, w_bf, sem):
    i = pl.program_id(0)

    def weight_copies(e):
        return [pltpu.make_async_copy(w.at[e], stage.at[m], sem.at[m])
                for m, w in enumerate((wg_hbm, wu_hbm, wd_hbm))]

    @pl.when(i == 0)
    def _():
        for cp in weight_copies(te_ref[0]):
            cp.start()

    @pl.when((i == 0) | (te_ref[i] != te_ref[jnp.maximum(i - 1, 0)]))
    def _():
        for cp in weight_copies(te_ref[i]):
            cp.wait()
        for m in range(3):
            w_bf[m] = stage[m].astype(BF16)

        @pl.when(nx_ref[i] >= 0)
        def _():
            for cp in weight_copies(nx_ref[i]):
                cp.start()

    @pl.when(tr_ref[i] > 0)
    def _():
        words = jnp.concatenate([x0_ref[...], x1_ref[...], x2_ref[...], x3_ref[...]], axis=1)
        hi, lo = _unpack_words(words)
        xt = jnp.concatenate([hi, lo], axis=1).astype(BF16)
        gt = jnp.minimum(jnp.dot(xt, w_bf[0], preferred_element_type=F32) + bg_ref[0], SWIGLU_LIMIT)
        up = jnp.clip(jnp.dot(xt, w_bf[1], preferred_element_type=F32) + bu_ref[0],
                      -SWIGLU_LIMIT, SWIGLU_LIMIT)
        act = gt * _sigmoid(SWIGLU_ALPHA * gt) * (up + 1.0)
        y = jnp.dot(act.astype(BF16), w_bf[2], preferred_element_type=F32) + bd_ref[0]
        out = _pack_words(y)
        for j, ref in enumerate((y0_ref, y1_ref, y2_ref, y3_ref)):
            ref[...] = out[:, j * LANES:(j + 1) * LANES]


def _expert_ffn(xs_chunks, tile_expert, tile_rows, next_expert, ew):
    n_rows = xs_chunks[0].shape[0]
    nt_max = n_rows // MOE_TILE
    wg, bg, wu, bu, wd, bd = ew
    rows = pl.BlockSpec((MOE_TILE, LANES), lambda i, te, tr, nx: (i, 0))
    mat = pl.BlockSpec(memory_space=pl.ANY)
    vec = pl.BlockSpec((1, 1, D_MODEL), lambda i, te, tr, nx: (te[i], 0, 0))
    grid_spec = pltpu.PrefetchScalarGridSpec(
        num_scalar_prefetch=3,
        grid=(nt_max,),
        in_specs=[rows] * N_CHUNKS + [mat, vec, mat, vec, mat, vec],
        out_specs=[rows] * N_CHUNKS,
        scratch_shapes=[pltpu.VMEM((3, D_MODEL, D_MODEL), F32),
                        pltpu.VMEM((3, D_MODEL, D_MODEL), BF16),
                        pltpu.SemaphoreType.DMA((3,))],
    )
    return pl.pallas_call(
        _expert_kernel,
        out_shape=[jax.ShapeDtypeStruct((n_rows, LANES), I32)] * N_CHUNKS,
        grid_spec=grid_spec,
        compiler_params=pltpu.CompilerParams(
            dimension_semantics=("arbitrary",), vmem_limit_bytes=VMEM_LIMIT_BYTES),
        name="expert_ffn",
    )(tile_expert, tile_rows, next_expert, *xs_chunks, wg, bg, wu, bu, wd, bd)


def _final_kernel(x1_ref, g_ref, w_ref, mod_ref, fg_ref, *rest):
    y_ref = rest[-1]
    wt = w_ref[...]
    acc = None
    for k in range(TOP_K):
        words = jnp.concatenate([g_ref[k * N_CHUNKS + j] for j in range(N_CHUNKS)], axis=1)
        hi, lo = _unpack_words(words)
        term = wt[:, k:k + 1] * jnp.concatenate([hi, lo], axis=1)
        acc = term if acc is None else acc + term
    g2 = mod_ref[0][5:6]
    x2 = x1_ref[...] + g2 * acc
    ms = jnp.mean(x2 * x2, axis=-1, keepdims=True)
    y_ref[...] = x2 * lax.rsqrt(ms + EPS) * fg_ref[...]


def _final(x1, gathered, wt_tok, mod, final_g, *, tl, tiles_per_batch, b0=0, out_rows=None, y_prev=None):
    n_tok = x1.shape[0]
    out_rows = n_tok if out_rows is None else out_rows
    blk0 = b0 * tiles_per_batch
    in_specs = [
        pl.BlockSpec((tl, D_MODEL), lambda i: (i, 0)),
        pl.BlockSpec((TOP_K * N_CHUNKS, tl, LANES), lambda i: (0, i, 0)),
        pl.BlockSpec((tl, 2 * TOP_K), lambda i: (i, 0)),
        pl.BlockSpec((1, 8, D_MODEL), lambda i: (b0 + i // tiles_per_batch, 0, 0)),
        pl.BlockSpec((1, D_MODEL), lambda i: (0, 0)),
    ]
    args = [x1, gathered, wt_tok, mod, final_g.reshape(1, D_MODEL)]
    aliases = {}
    if y_prev is not None:
        in_specs.append(pl.BlockSpec(memory_space=pl.ANY))
        args.append(y_prev)
        aliases = {len(args) - 1: 0}
    return pl.pallas_call(
        _final_kernel,
        out_shape=jax.ShapeDtypeStruct((out_rows, D_MODEL), F32),
        grid=(n_tok // tl,),
        in_specs=in_specs,
        out_specs=pl.BlockSpec((tl, D_MODEL), lambda i: (blk0 + i, 0)),
        input_output_aliases=aliases,
        compiler_params=pltpu.CompilerParams(
            dimension_semantics=("arbitrary",), vmem_limit_bytes=VMEM_LIMIT_BYTES),
        name="combine_final",
    )(*args)


def _route_rows(r, lo, hi):
    return r[:, lo:hi, :].transpose(1, 0, 2).reshape(hi - lo, -1)


def kernel(x_prompt, x_sample, state_pool, state_conv, c_prompt, c_sample, norm1_g, norm2_g, final_g, w_ada, b_ada, w_in, pool_w, pool_scale, conv_dw, conv_b, conv_ln_g, conv_ln_b, conv_w_out, gate_w, gate_b, w_out, router_w, router_b, exp_w_gate, exp_b_gate, exp_w_up, exp_b_up, exp_w_down, exp_b_down):
    assert norm1_g.shape[0] == 1, "single-layer trunk"
    row = lambda v: v.reshape(1, -1)
    mix_w = (
        row(norm1_g[0]), row(norm2_g[0]),
        w_in[0].astype(BF16), pool_w[0].astype(BF16), row(pool_scale[0]),
        jnp.broadcast_to(conv_dw[0][:, None, :], (CONV_WIDTH, SUBLANES, D_CONV)),
        row(conv_b[0]), row(conv_ln_g[0]), row(conv_ln_b[0]),
        conv_w_out[0].astype(BF16), gate_w[0].astype(BF16), row(gate_b[0]),
        w_out[0].astype(BF16), router_w[0].T.astype(BF16),
        jnp.broadcast_to(router_b[0][:, None], (N_EXPERTS, LANES)),
    )
    ew = (
        exp_w_gate[0], exp_b_gate[0][:, None, :],
        exp_w_up[0], exp_b_up[0][:, None, :],
        exp_w_down[0], exp_b_down[0][:, None, :],
    )
    bp, lp, _ = x_prompt.shape
    bs, ls, _ = x_sample.shape
    t_p = bp * lp
    tl_p, tl_s = 512, ls
    nb_small = max(bp // PROMPT_TAIL_FRACTION, 1)
    chunk_nb = [nb_small, bp - nb_small] if bp > nb_small else [bp]
    chunk_b0 = [sum(chunk_nb[:c]) for c in range(len(chunk_nb))]
    n_chunks = len(chunk_nb)

    mod_p = _modulation(c_prompt, w_ada[0], b_ada[0])
    mod_s = _modulation(c_sample, w_ada[0], b_ada[0])
    pad_state = lambda s, hist: jnp.pad(s, ((0, 0), (hist - s.shape[1], 0), (0, 0)))
    split = lambda out: (out[0], out[1:1 + N_CHUNKS], *out[1 + N_CHUNKS:])

    zero_cnt = jnp.zeros((N_EXPERTS, LANES), F32)
    mixed = [split(_token_mix(
        x_prompt, mod_p, jnp.zeros((nb, POOL_HIST, D_POOL), F32), jnp.zeros((nb, CONV_HIST, D_CONV), F32),
        zero_cnt, mix_w, b0=b0, tl=tl_p, th=tl_p, pos0=0, rc=32))
        for b0, nb in zip(chunk_b0, chunk_nb)]
    x1_s, hp_s, ri_s, rw_s, npool_s, nconv_s, cnt_last = split(_token_mix(
        x_sample, mod_s, pad_state(state_pool[0], POOL_HIST), pad_state(state_conv[0], CONV_HIST),
        mixed[-1][6], mix_w, b0=0, tl=tl_s, th=tl_s, pos0=PAST_LEN, rc=32))

    def finish(c, g_c, y_prev):
        x1_c, rw_c = mixed[c][0], mixed[c][3]
        return _final(x1_c, g_c, _route_rows(rw_c, 0, 2 * TOP_K).T, mod_p, final_g, tl=tl_p,
                      tiles_per_batch=lp // tl_p, b0=chunk_b0[c], out_rows=t_p, y_prev=y_prev)

    y_p = None
    gathered = []
    for c, (x1_c, hp_c, ri_c, rw_c, _, _, cnt_c) in enumerate(mixed):
        last = c == n_chunks - 1
        n_tok = chunk_nb[c] * lp + (bs * ls if last else 0)
        n_rows = (TOP_K * n_tok // MOE_TILE + N_EXPERTS) * MOE_TILE
        counts = (cnt_last if last else cnt_c)[:, 0].astype(I32)
        tiles_e = (counts + MOE_TILE - 1) // MOE_TILE
        tile_end = jnp.cumsum(tiles_e)
        tile_start = tile_end - tiles_e
        row_off = tile_start * MOE_TILE
        e_ids = jnp.arange(N_EXPERTS, dtype=I32)
        tile_id = jnp.arange(n_rows // MOE_TILE, dtype=I32)
        e_last = jnp.max(jnp.where(tiles_e > 0, e_ids, 0))
        tile_expert = jnp.minimum(
            jnp.sum(tile_id[:, None] >= tile_end[None, :], axis=1), e_last).astype(I32)
        is_e = tile_expert[:, None] == e_ids[None, :]
        left = jnp.sum(jnp.where(is_e, (counts - (tile_id[:, None] - tile_start[None, :]) * MOE_TILE), 0), axis=1)
        tile_rows = jnp.clip(left, 0, MOE_TILE).astype(I32)
        later = (e_ids[None, :] > e_ids[:, None]) & (tiles_e[None, :] > 0)
        next_of = jnp.min(jnp.where(later, e_ids[None, :], N_EXPERTS), axis=1)
        next_of = jnp.where(next_of == N_EXPERTS, -1, next_of)
        next_expert = jnp.sum(jnp.where(is_e, next_of[None, :], 0), axis=1).astype(I32)

        def slots(ri, row_off=row_off):
            ids, ranks = _route_rows(ri, 0, TOP_K), _route_rows(ri, TOP_K, 2 * TOP_K)
            pos = ranks
            for e in range(N_EXPERTS):
                pos = pos + jnp.where(ids == e, row_off[e], 0)
            return [pos[k] for k in range(TOP_K)]

        h_groups = [hp_c] + ([hp_s] if last else [])
        pos_groups = [slots(ri_c)] + ([slots(ri_s)] if last else [])
        xs = _dispatch(h_groups, pos_groups, n_rows)
        ys = _expert_ffn(xs, tile_expert, tile_rows, next_expert, ew)
        if c > 0:
            y_p = finish(c - 1, gathered[c - 1][0], y_p)
        gathered.append(_combine_gather(ys, pos_groups))

    y_p = finish(n_chunks - 1, gathered[-1][0], y_p)
    y_s = _final(x1_s, gathered[-1][1], _route_rows(rw_s, 0, 2 * TOP_K).T, mod_s, final_g, tl=tl_s,
                 tiles_per_batch=1)
    unpad = lambda s, n: s[:, s.shape[1] - n:][None]
    npool_p = jnp.concatenate([m[4] for m in mixed], axis=0)
    nconv_p = jnp.concatenate([m[5] for m in mixed], axis=0)
    return (y_p.reshape(bp, lp, D_MODEL), y_s.reshape(bs, ls, D_MODEL),
            unpad(npool_p, POOL_PAD), unpad(nconv_p, CONV_PAD),
            unpad(npool_s, POOL_PAD), unpad(nconv_s, CONV_PAD))
```

```python
import functools

import jax
import jax.numpy as jnp
from jax import lax
from jax.experimental import pallas as pl
from jax.experimental.pallas import tpu as pltpu
from jax.experimental.pallas import tpu_sc as plsc

D_MODEL = 1024
D_POOL = 512
D_CONV = 512
POOL_WINDOWS = (2, 4, 8, 16)
POOL_GW = 128
POOL_OUT_GW = 256
POOL_PAD = 15
CONV_WIDTH = 31
CONV_PAD = 30
N_EXPERTS = 32
TOP_K = 4
SWIGLU_LIMIT = 7.0
SWIGLU_ALPHA = 1.702
EPS = 1e-6
PAST_LEN = 2048

POOL_HIST = 16
CONV_HIST = 32
VMEM_LIMIT_BYTES = 56 * 1024 * 1024

LANES = 128
SUBLANES = 8
D_WORDS = D_MODEL // 2
N_CHUNKS = D_WORDS // LANES
MOE_TILE = 512
PROMPT_TAIL_FRACTION = 4
SC_CORES = 2
SC_SUBCORES = 16
SC_WORKERS = SC_CORES * SC_SUBCORES
SC_ROWS = 128

F32 = jnp.float32
BF16 = jnp.bfloat16
I32 = jnp.int32
HI_MASK = -65536
NEG_LOG2E = -1.4426950408889634


def _sigmoid(v):
    return 1.0 / (1.0 + jnp.exp2(v * NEG_LOG2E))


def _pack_words(v):
    r = v.astype(BF16).astype(F32)
    hi = lax.bitcast_convert_type(r[:, :D_WORDS], I32)
    lo = lax.bitcast_convert_type(r[:, D_WORDS:], I32)
    return (hi & HI_MASK) | lax.shift_right_logical(lo, 16)


def _unpack_words(w):
    hi = lax.bitcast_convert_type(w & HI_MASK, F32)
    lo = lax.bitcast_convert_type(lax.shift_left(w, 16), F32)
    return hi, lo


def _mod_kernel(c_ref, w_ref, b_ref, o_ref):
    c = c_ref[...]
    s = (c * _sigmoid(c)).astype(BF16)
    o_ref[...] = jnp.dot(s, w_ref[...].astype(BF16), preferred_element_type=F32) + b_ref[...]


def _modulation(c, w_ada, b_ada):
    bsz = c.shape[0]
    out = pl.pallas_call(
        _mod_kernel,
        out_shape=jax.ShapeDtypeStruct((bsz, 6 * D_MODEL), F32),
        grid=(6,),
        in_specs=[
            pl.BlockSpec((bsz, D_MODEL), lambda j: (0, 0)),
            pl.BlockSpec((D_MODEL, D_MODEL), lambda j: (0, j)),
            pl.BlockSpec((1, D_MODEL), lambda j: (0, j)),
        ],
        out_specs=pl.BlockSpec((bsz, D_MODEL), lambda j: (0, j)),
        compiler_params=pltpu.CompilerParams(
            dimension_semantics=("arbitrary",), vmem_limit_bytes=VMEM_LIMIT_BYTES),
        name="adaln_mod",
    )(c, w_ada, b_ada.reshape(1, 6 * D_MODEL))
    out = out.reshape(bsz, 6, D_MODEL)
    return jnp.pad(out, ((0, 0), (0, 2), (0, 0)))


def _mix_kernel(x_ref, mod_ref, pinit_ref, cinit_ref, cntin_ref, tri_ref,
                n1_ref, n2_ref, win_ref, pw_ref, ps_ref, dw_ref, cb_ref, lg_ref, lb_ref,
                cwo_ref, gw_ref, gb_ref, wo_ref, rwt_ref, rb_ref,
                x1_ref, hp0_ref, hp1_ref, hp2_ref, hp3_ref, ri_ref, rw_ref,
                npool_ref, nconv_ref, cntout_ref,
                conv_sh, pool_in, hbuf, merged, ybuf, sy, cnt_run, *, tl, th, pos0, rc):
    b = pl.program_id(0)
    l = pl.program_id(1)

    conv_in = conv_sh.at[0]

    @pl.when(l == 0)
    def _():
        conv_in[0:CONV_HIST, :] = cinit_ref[0]
        pool_in[0:POOL_HIST, :] = pinit_ref[0]

    @pl.when((b == 0) & (l == 0))
    def _():
        cnt_run[...] = cntin_ref[...]

    mod = mod_ref[0]
    sh1, sc1, g1 = mod[0:1], mod[1:2], mod[2:3]
    sh2, sc2 = mod[3:4], mod[4:5]
    scale1 = n1_ref[...] * (1.0 + sc1)
    scale2 = n2_ref[...] * (1.0 + sc2)

    for r0 in range(0, tl, th):
        rows = slice(r0, r0 + th)
        x = x_ref[rows, :]
        ms = jnp.mean(x * x, axis=-1, keepdims=True)
        hbuf[rows, :] = ((x * lax.rsqrt(ms + EPS)) * scale1 + sh1).astype(BF16)

        proj = jnp.dot(hbuf[rows, :], win_ref[...], preferred_element_type=F32)
        glu = proj[:, D_POOL:D_POOL + D_CONV] * _sigmoid(proj[:, D_POOL + D_CONV:])
        pool_in[POOL_HIST + r0:POOL_HIST + r0 + th, :] = proj[:, :D_POOL]
        conv_in[CONV_HIST + r0:CONV_HIST + r0 + th, :] = glu

        pos = pos0 + l * tl + r0 + lax.broadcasted_iota(I32, (th, POOL_GW), 0)
        pooled = []
        for gi, w in enumerate(POOL_WINDOWS):
            lanes = slice(gi * POOL_GW, (gi + 1) * POOL_GW)
            s = pool_in[r0:r0 + POOL_HIST + th, lanes]
            d = 1
            while d < w:
                s = s + pltpu.roll(s, d, axis=0)
                d *= 2
            cur = pool_in[POOL_HIST + r0:POOL_HIST + r0 + th, lanes]
            cnt = jnp.minimum(pos + 1, w).astype(F32)
            pooled.append((s[POOL_HIST:] / cnt - cur).astype(BF16))

        sh_lo = 0 if r0 == 0 else r0 + CONV_HIST - SUBLANES
        sh_hi = r0 + th + CONV_HIST - SUBLANES
        for r in range(1, SUBLANES):
            conv_sh[r, sh_lo:sh_hi, :] = conv_in[sh_lo + r:sh_hi + r, :]

        for base in range(r0, r0 + th, rc):
            acc = jnp.broadcast_to(cb_ref[...], (rc // SUBLANES, SUBLANES, D_CONV))
            for k in range(CONV_WIDTH):
                q, r = divmod(k + CONV_HIST - CONV_PAD, SUBLANES)
                start = base + q * SUBLANES
                win = conv_sh[r, start:start + rc, :].reshape(rc // SUBLANES, SUBLANES, D_CONV)
                acc = acc + win * dw_ref[k]
            ybuf[base:base + rc, :] = acc.reshape(rc, D_CONV)

        yc = ybuf[rows, :]
        mu = jnp.mean(yc, axis=-1, keepdims=True)
        dev = yc - mu
        var = jnp.mean(dev * dev, axis=-1, keepdims=True)
        yn = dev * lax.rsqrt(var + EPS) * lg_ref[...] + lb_ref[...]
        sy[rows, :] = (yn * _sigmoid(yn)).astype(BF16)

        for j in range(len(POOL_WINDOWS)):
            cs = slice(j * POOL_OUT_GW, (j + 1) * POOL_OUT_GW)
            cs2 = slice(D_MODEL + j * POOL_OUT_GW, D_MODEL + (j + 1) * POOL_OUT_GW)
            ga = _sigmoid(jnp.dot(hbuf[rows, :], gw_ref[:, cs], preferred_element_type=F32) + gb_ref[:, cs])
            gb = _sigmoid(jnp.dot(hbuf[rows, :], gw_ref[:, cs2], preferred_element_type=F32) + gb_ref[:, cs2])
            a_j = jnp.dot(pooled[j], pw_ref[j], preferred_element_type=F32) * ps_ref[:, cs]
            b_j = jnp.dot(sy[rows, :], cwo_ref[:, cs], preferred_element_type=F32)
            merged[rows, cs] = (ga * a_j + gb * b_j).astype(BF16)

        x1 = x + g1 * jnp.dot(merged[rows, :], wo_ref[...], preferred_element_type=F32)
        x1_ref[rows, :] = x1

        ms2 = jnp.mean(x1 * x1, axis=-1, keepdims=True)
        h2 = (x1 * lax.rsqrt(ms2 + EPS)) * scale2 + sh2
        h2b = h2.astype(BF16)
        words = _pack_words(h2)
        for j, ref in enumerate((hp0_ref, hp1_ref, hp2_ref, hp3_ref)):
            ref[rows, :] = words[:, j * LANES:(j + 1) * LANES]

        logits = lax.dot_general(rwt_ref[...], h2b, (((1,), (1,)), ((), ())),
                                 preferred_element_type=F32) + rb_ref[:, 0:1]
        e_iota = lax.broadcasted_iota(I32, (N_EXPERTS, th), 0)
        v = logits
        ids, vals = [], []
        for _ in range(TOP_K):
            m = jnp.max(v, axis=0, keepdims=True)
            idx = jnp.min(jnp.where(v == m, e_iota, N_EXPERTS), axis=0, keepdims=True)
            ids.append(idx)
            vals.append(m)
            v = jnp.where(e_iota == idx, -jnp.inf, v)
        ex = [jnp.exp(vk - vals[0]) for vk in vals]
        den = ex[0] + ex[1] + ex[2] + ex[3]

        sel = [e_iota == idx for idx in ids]
        chosen = jnp.where(sel[0] | sel[1] | sel[2] | sel[3], 1.0, 0.0)
        before = jnp.dot(chosen.astype(BF16), tri_ref[...], preferred_element_type=F32)
        rank_all = cnt_run[:, 0:1] + before
        for k in range(TOP_K):
            ri_ref[0, k:k + 1, rows] = ids[k]
            rk = jnp.sum(jnp.where(sel[k], rank_all, 0.0), axis=0, keepdims=True)
            ri_ref[0, TOP_K + k:TOP_K + k + 1, rows] = rk.astype(I32)
            rw_ref[0, k:k + 1, rows] = ex[k] / den
            rw_ref[0, TOP_K + k:TOP_K + k + 1, rows] = jnp.zeros((1, th), F32)
        cnt_run[...] = cnt_run[...] + jnp.sum(chosen, axis=1, keepdims=True)

    npool_ref[0] = pool_in[tl:tl + POOL_HIST, :]
    nconv_ref[0] = conv_in[tl:tl + CONV_HIST, :]
    conv_in[0:CONV_HIST, :] = conv_in[tl:tl + CONV_HIST, :]
    pool_in[0:POOL_HIST, :] = pool_in[tl:tl + POOL_HIST, :]
    cntout_ref[...] = cnt_run[...]


def _token_mix(x, mod, pool_init, conv_init, cnt_in, wts, *, b0, tl, th, pos0, rc):
    _, seq, _ = x.shape
    bsz = pool_init.shape[0]
    assert seq % tl == 0 and tl % th == 0 and th % rc == 0 and th >= CONV_HIST
    assert th % LANES == 0 or th == tl
    rows_out = bsz * seq
    nl = seq // tl
    x2d = x.reshape(-1, D_MODEL)
    tri = jnp.triu(jnp.ones((th, th), BF16), k=1)
    full = lambda shape: pl.BlockSpec(shape, lambda b, l: (0,) * len(shape))
    per_b = lambda shape: pl.BlockSpec(shape, lambda b, l: (b,) + (0,) * (len(shape) - 1))
    tok = lambda width: pl.BlockSpec((tl, width), lambda b, l: (b * nl + l, 0))
    rt = lambda shape: pl.BlockSpec(shape, lambda b, l: (b * nl + l, 0, 0))
    in_specs = [
        pl.BlockSpec((tl, D_MODEL), lambda b, l: ((b0 + b) * nl + l, 0)),
        pl.BlockSpec((1, 8, D_MODEL), lambda b, l: (b0 + b, 0, 0)),
        per_b((1, POOL_HIST, D_POOL)),
        per_b((1, CONV_HIST, D_CONV)),
        full((N_EXPERTS, LANES)),
        full((th, th)),
        full((1, D_MODEL)), full((1, D_MODEL)),
        full((D_MODEL, D_POOL + 2 * D_CONV)),
        full((len(POOL_WINDOWS), POOL_GW, POOL_OUT_GW)),
        full((1, D_MODEL)),
        full((CONV_WIDTH, SUBLANES, D_CONV)),
        full((1, D_CONV)), full((1, D_CONV)), full((1, D_CONV)),
        full((D_CONV, D_MODEL)),
        full((D_MODEL, 2 * D_MODEL)),
        full((1, 2 * D_MODEL)),
        full((D_MODEL, D_MODEL)),
        full((N_EXPERTS, D_MODEL)),
        full((N_EXPERTS, LANES)),
    ]
    out_shape = (
        jax.ShapeDtypeStruct((rows_out, D_MODEL), F32),
        *[jax.ShapeDtypeStruct((rows_out, LANES), I32)] * N_CHUNKS,
        jax.ShapeDtypeStruct((bsz * nl, 2 * TOP_K, tl), I32),
        jax.ShapeDtypeStruct((bsz * nl, 2 * TOP_K, tl), F32),
        jax.ShapeDtypeStruct((bsz, POOL_HIST, D_POOL), F32),
        jax.ShapeDtypeStruct((bsz, CONV_HIST, D_CONV), F32),
        jax.ShapeDtypeStruct((N_EXPERTS, LANES), F32),
    )
    out_specs = (
        tok(D_MODEL),
        *[tok(LANES)] * N_CHUNKS,
        rt((1, 2 * TOP_K, tl)),
        rt((1, 2 * TOP_K, tl)),
        per_b((1, POOL_HIST, D_POOL)),
        per_b((1, CONV_HIST, D_CONV)),
        full((N_EXPERTS, LANES)),
    )
    scratch = [
        pltpu.VMEM((SUBLANES, tl + CONV_HIST, D_CONV), F32),
        pltpu.VMEM((tl + POOL_HIST, D_POOL), F32),
        pltpu.VMEM((tl, D_MODEL), BF16),
        pltpu.VMEM((tl, D_MODEL), BF16),
        pltpu.VMEM((tl, D_CONV), F32),
        pltpu.VMEM((tl, D_CONV), BF16),
        pltpu.VMEM((N_EXPERTS, LANES), F32),
    ]
    return pl.pallas_call(
        functools.partial(_mix_kernel, tl=tl, th=th, pos0=pos0, rc=rc),
        out_shape=out_shape,
        grid=(bsz, nl),
        in_specs=in_specs,
        out_specs=out_specs,
        scratch_shapes=scratch,
        compiler_params=pltpu.CompilerParams(
            dimension_semantics=("arbitrary", "arbitrary"), vmem_limit_bytes=VMEM_LIMIT_BYTES),
        name="token_mix",
    )(x2d, mod, pool_init, conv_init, cnt_in, tri, *wts)


def _skew_kernel(xh_ref, xt_ref, modh_ref, modt_ref, pinit_ref, cinit_ref, cntin_ref, tri_ref,
                 n1_ref, n2_ref, win_ref, pw_ref, ps_ref, dw_ref, cb_ref, lg_ref, lb_ref,
                 cwo_ref, gw_ref, gb_ref, wo_ref, rwt_ref, rb_ref,
                 x1_ref, hp0_ref, hp1_ref, hp2_ref, hp3_ref, ri_ref, rw_ref,
                 npool_ref, nconv_ref, cntout_ref,
                 conv_sh, pool_in, hbuf, ybuf, h_prev, sy_prev, pooled_prev, merged, cnt_run,
                 *, tl, nl, n_tiles, pos0, rc):
    g = pl.program_id(0)
    gh = jnp.minimum(g, n_tiles - 1)
    l = lax.rem(gh, nl)
    head_live = g < n_tiles
    tail_live = g >= 1
    conv_in = conv_sh.at[0]

    @pl.when(g == 0)
    def _():
        cnt_run[...] = cntin_ref[...]
        h_prev[...] = jnp.zeros_like(h_prev)
        sy_prev[...] = jnp.zeros_like(sy_prev)
        pooled_prev[...] = jnp.zeros_like(pooled_prev)

    @pl.when(l == 0)
    def _():
        conv_in[0:CONV_HIST, :] = cinit_ref[0]
        pool_in[0:POOL_HIST, :] = pinit_ref[0]

    modh = modh_ref[0]
    sh1, sc1 = modh[0:1], modh[1:2]
    x = xh_ref[...]
    ms = jnp.mean(x * x, axis=-1, keepdims=True)
    hbuf[...] = ((x * lax.rsqrt(ms + EPS)) * (n1_ref[...] * (1.0 + sc1)) + sh1).astype(BF16)

    proj = jnp.dot(hbuf[...], win_ref[...], preferred_element_type=F32)
    glu = proj[:, D_POOL:D_POOL + D_CONV] * _sigmoid(proj[:, D_POOL + D_CONV:])
    pool_in[POOL_HIST:POOL_HIST + tl, :] = proj[:, :D_POOL]
    conv_in[CONV_HIST:CONV_HIST + tl, :] = glu

    modt = modt_ref[0]
    g1, sh2, sc2 = modt[2:3], modt[3:4], modt[4:5]
    for j in range(len(POOL_WINDOWS)):
        cs = slice(j * POOL_OUT_GW, (j + 1) * POOL_OUT_GW)
        cs2 = slice(D_MODEL + j * POOL_OUT_GW, D_MODEL + (j + 1) * POOL_OUT_GW)
        ga = _sigmoid(jnp.dot(h_prev[...], gw_ref[:, cs], preferred_element_type=F32) + gb_ref[:, cs])
        gb = _sigmoid(jnp.dot(h_prev[...], gw_ref[:, cs2], preferred_element_type=F32) + gb_ref[:, cs2])
        a_j = jnp.dot(pooled_prev[:, j * POOL_GW:(j + 1) * POOL_GW], pw_ref[j],
                      preferred_element_type=F32) * ps_ref[:, cs]
        b_j = jnp.dot(sy_prev[...], cwo_ref[:, cs], preferred_element_type=F32)
        merged[:, cs] = (ga * a_j + gb * b_j).astype(BF16)

    x1 = xt_ref[...] + g1 * jnp.dot(merged[...], wo_ref[...], preferred_element_type=F32)
    x1_ref[...] = x1
    ms2 = jnp.mean(x1 * x1, axis=-1, keepdims=True)
    h2 = (x1 * lax.rsqrt(ms2 + EPS)) * (n2_ref[...] * (1.0 + sc2)) + sh2
    h2b = h2.astype(BF16)
    words = _pack_words(h2)
    for j, ref in enumerate((hp0_ref, hp1_ref, hp2_ref, hp3_ref)):
        ref[...] = words[:, j * LANES:(j + 1) * LANES]

    logits = lax.dot_general(rwt_ref[...], h2b, (((1,), (1,)), ((), ())),
                             preferred_element_type=F32) + rb_ref[:, 0:1]
    e_iota = lax.broadcasted_iota(I32, (N_EXPERTS, tl), 0)
    v = logits
    ids, vals = [], []
    for _ in range(TOP_K):
        m = jnp.max(v, axis=0, keepdims=True)
        idx = jnp.min(jnp.where(v == m, e_iota, N_EXPERTS), axis=0, keepdims=True)
        ids.append(idx)
        vals.append(m)
        v = jnp.where(e_iota == idx, -jnp.inf, v)
    ex = [jnp.exp(vk - vals[0]) for vk in vals]
    den = ex[0] + ex[1] + ex[2] + ex[3]
    sel = [e_iota == idx for idx in ids]
    chosen = jnp.where((sel[0] | sel[1] | sel[2] | sel[3]) & tail_live, 1.0, 0.0)
    before = jnp.dot(chosen.astype(BF16), tri_ref[...], preferred_element_type=F32)
    rank_all = cnt_run[:, 0:1] + before
    for k in range(TOP_K):
        ri_ref[0, k:k + 1, :] = ids[k]
        rk = jnp.sum(jnp.where(sel[k], rank_all, 0.0), axis=0, keepdims=True)
        ri_ref[0, TOP_K + k:TOP_K + k + 1, :] = rk.astype(I32)
        rw_ref[0, k:k + 1, :] = ex[k] / den
        rw_ref[0, TOP_K + k:TOP_K + k + 1, :] = jnp.zeros((1, tl), F32)
    cnt_run[...] = cnt_run[...] + jnp.sum(chosen, axis=1, keepdims=True)
    cntout_ref[...] = cnt_run[...]

    pos = pos0 + l * tl + lax.broadcasted_iota(I32, (tl, POOL_GW), 0)
    pooled = []
    for gi, w in enumerate(POOL_WINDOWS):
        lanes = slice(gi * POOL_GW, (gi + 1) * POOL_GW)
        s = pool_in[:, lanes]
        d = 1
        while d < w:
            s = s + pltpu.roll(s, d, axis=0)
            d *= 2
        cur = pool_in[POOL_HIST:POOL_HIST + tl, lanes]
        cnt = jnp.minimum(pos + 1, w).astype(F32)
        pooled.append((s[POOL_HIST:] / cnt - cur).astype(BF16))

    n_sh = tl + CONV_HIST - SUBLANES
    for r in range(1, SUBLANES):
        conv_sh[r, 0:n_sh, :] = conv_in[r:r + n_sh, :]

    for base in range(0, tl, rc):
        acc = jnp.broadcast_to(cb_ref[...], (rc // SUBLANES, SUBLANES, D_CONV))
        for k in range(CONV_WIDTH):
            q, r = divmod(k + CONV_HIST - CONV_PAD, SUBLANES)
            start = base + q * SUBLANES
            win = conv_sh[r, start:start + rc, :].reshape(rc // SUBLANES, SUBLANES, D_CONV)
            acc = acc + win * dw_ref[k]
        ybuf[base:base + rc, :] = acc.reshape(rc, D_CONV)

    yc = ybuf[...]
    mu = jnp.mean(yc, axis=-1, keepdims=True)
    dev = yc - mu
    var = jnp.mean(dev * dev, axis=-1, keepdims=True)
    yn = dev * lax.rsqrt(var + EPS) * lg_ref[...] + lb_ref[...]

    h_prev[...] = hbuf[...]
    sy_prev[...] = (yn * _sigmoid(yn)).astype(BF16)
    for gi in range(len(POOL_WINDOWS)):
        pooled_prev[:, gi * POOL_GW:(gi + 1) * POOL_GW] = pooled[gi]

    @pl.when(head_live)
    def _():
        npool_ref[0] = pool_in[tl:tl + POOL_HIST, :]
        nconv_ref[0] = conv_in[tl:tl + CONV_HIST, :]

    conv_in[0:CONV_HIST, :] = conv_in[tl:tl + CONV_HIST, :]
    pool_in[0:POOL_HIST, :] = pool_in[tl:tl + POOL_HIST, :]


def _token_mix_skew(x, mod, pool_init, conv_init, cnt_in, wts, *, b0, tl, pos0, rc):
    _, seq, _ = x.shape
    bsz = pool_init.shape[0]
    assert seq % tl == 0 and tl % rc == 0 and tl >= CONV_HIST
    nl = seq // tl
    n_tiles = bsz * nl
    rows_out = bsz * seq
    x2d = x.reshape(-1, D_MODEL)
    tri = jnp.triu(jnp.ones((tl, tl), BF16), k=1)
    head = lambda g: jnp.minimum(g, n_tiles - 1)
    tail = lambda g: jnp.maximum(g - 1, 0)
    full = lambda shape: pl.BlockSpec(shape, lambda g: (0,) * len(shape))
    per_b = lambda shape: pl.BlockSpec(shape, lambda g: (head(g) // nl,) + (0,) * (len(shape) - 1))
    tok_t = lambda width: pl.BlockSpec((tl, width), lambda g: (tail(g), 0))
    rt = lambda shape: pl.BlockSpec(shape, lambda g: (tail(g), 0, 0))
    in_specs = [
        pl.BlockSpec((tl, D_MODEL), lambda g: (b0 * nl + head(g), 0)),
        pl.BlockSpec((tl, D_MODEL), lambda g: (b0 * nl + tail(g), 0)),
        pl.BlockSpec((1, 8, D_MODEL), lambda g: (b0 + head(g) // nl, 0, 0)),
        pl.BlockSpec((1, 8, D_MODEL), lambda g: (b0 + tail(g) // nl, 0, 0)),
        per_b((1, POOL_HIST, D_POOL)),
        per_b((1, CONV_HIST, D_CONV)),
        full((N_EXPERTS, LANES)),
        full((tl, tl)),
        full((1, D_MODEL)), full((1, D_MODEL)),
        full((D_MODEL, D_POOL + 2 * D_CONV)),
        full((len(POOL_WINDOWS), POOL_GW, POOL_OUT_GW)),
        full((1, D_MODEL)),
        full((CONV_WIDTH, SUBLANES, D_CONV)),
        full((1, D_CONV)), full((1, D_CONV)), full((1, D_CONV)),
        full((D_CONV, D_MODEL)),
        full((D_MODEL, 2 * D_MODEL)),
        full((1, 2 * D_MODEL)),
        full((D_MODEL, D_MODEL)),
        full((N_EXPERTS, D_MODEL)),
        full((N_EXPERTS, LANES)),
    ]
    out_shape = (
        jax.ShapeDtypeStruct((rows_out, D_MODEL), F32),
        *[jax.ShapeDtypeStruct((rows_out, LANES), I32)] * N_CHUNKS,
        jax.ShapeDtypeStruct((n_tiles, 2 * TOP_K, tl), I32),
        jax.ShapeDtypeStruct((n_tiles, 2 * TOP_K, tl), F32),
        jax.ShapeDtypeStruct((bsz, POOL_HIST, D_POOL), F32),
        jax.ShapeDtypeStruct((bsz, CONV_HIST, D_CONV), F32),
        jax.ShapeDtypeStruct((N_EXPERTS, LANES), F32),
    )
    out_specs = (
        tok_t(D_MODEL),
        *[tok_t(LANES)] * N_CHUNKS,
        rt((1, 2 * TOP_K, tl)),
        rt((1, 2 * TOP_K, tl)),
        per_b((1, POOL_HIST, D_POOL)),
        per_b((1, CONV_HIST, D_CONV)),
        full((N_EXPERTS, LANES)),
    )
    scratch = [
        pltpu.VMEM((SUBLANES, tl + CONV_HIST, D_CONV), F32),
        pltpu.VMEM((tl + POOL_HIST, D_POOL), F32),
        pltpu.VMEM((tl, D_MODEL), BF16),
        pltpu.VMEM((tl, D_CONV), F32),
        pltpu.VMEM((tl, D_MODEL), BF16),
        pltpu.VMEM((tl, D_CONV), BF16),
        pltpu.VMEM((tl, D_POOL), BF16),
        pltpu.VMEM((tl, D_MODEL), BF16),
        pltpu.VMEM((N_EXPERTS, LANES), F32),
    ]
    return pl.pallas_call(
        functools.partial(_skew_kernel, tl=tl, nl=nl, n_tiles=n_tiles, pos0=pos0, rc=rc),
        out_shape=out_shape,
        grid=(n_tiles + 1,),
        in_specs=in_specs,
        out_specs=out_specs,
        scratch_shapes=scratch,
        compiler_params=pltpu.CompilerParams(
            dimension_semantics=("arbitrary",), vmem_limit_bytes=VMEM_LIMIT_BYTES),
        name="token_mix",
    )(x2d, x2d, mod, mod, pool_init, conv_init, cnt_in, tri, *wts)


def _sc_mesh():
    return plsc.VectorSubcoreMesh(core_axis_name="c", subcore_axis_name="s",
                                  num_cores=SC_CORES, num_subcores=SC_SUBCORES)


def _for_each_chunk(n_tok, fn):
    wid = lax.axis_index("s") * SC_CORES + lax.axis_index("c")
    if n_tok % (8 * SC_WORKERS) == 0 and n_tok // SC_WORKERS >= SC_ROWS:
        per_w = n_tok // SC_WORKERS

        @pl.loop(0, pl.cdiv(per_w, SC_ROWS))
        def _(c):
            fn(pl.multiple_of(wid * per_w + jnp.minimum(c * SC_ROWS, per_w - SC_ROWS), 8))
    else:
        assert n_tok % SC_ROWS == 0 and n_tok // SC_ROWS <= SC_WORKERS

        @pl.when(wid < n_tok // SC_ROWS)
        def _():
            fn(pl.multiple_of(wid * SC_ROWS, SC_ROWS))


_SC_SCRATCH = ([pltpu.VMEM((SC_ROWS, LANES), I32)] * N_CHUNKS
               + [pltpu.VMEM((SC_ROWS,), I32)] * TOP_K
               + [pltpu.SemaphoreType.DMA, pltpu.SemaphoreType.DMA])


def _dispatch(h_groups, pos_groups, n_rows):
    n_g = len(h_groups)

    def body(*refs):
        refs = list(refs)
        h = [[refs.pop(0) for _ in range(N_CHUNKS)] for _ in range(n_g)]
        p = [[refs.pop(0) for _ in range(TOP_K)] for _ in range(n_g)]
        o = [refs.pop(0) for _ in range(N_CHUNKS)]
        rows = [refs.pop(0) for _ in range(N_CHUNKS)]
        idx = [refs.pop(0) for _ in range(TOP_K)]
        sem_in, sem_out = refs

        for hg, pg in zip(h, p):
            def move(s, hg=hg, pg=pg):
                loads = [pltpu.async_copy(hg[j].at[pl.ds(s, SC_ROWS)], rows[j], sem_in)
                         for j in range(N_CHUNKS)]
                loads += [pltpu.async_copy(pg[k].at[pl.ds(s, SC_ROWS)], idx[k], sem_in)
                          for k in range(TOP_K)]
                for cp in loads:
                    cp.wait()
                stores = [pltpu.async_copy(rows[j], o[j].at[idx[k]], sem_out)
                          for j in range(N_CHUNKS) for k in range(TOP_K)]
                for cp in stores:
                    cp.wait()

            _for_each_chunk(hg[0].shape[0], move)

    call = pl.kernel(
        body,
        out_type=tuple(jax.ShapeDtypeStruct((n_rows, LANES), I32) for _ in range(N_CHUNKS)),
        mesh=_sc_mesh(), scratch_types=_SC_SCRATCH, name="sc_dispatch")
    flat = [a for g in h_groups for a in g] + [a for g in pos_groups for a in g]
    return call(*flat)


def _combine_gather(y_chunks, pos_groups):
    n_g = len(pos_groups)

    def body(*refs):
        refs = list(refs)
        y = [refs.pop(0) for _ in range(N_CHUNKS)]
        p = [[refs.pop(0) for _ in range(TOP_K)] for _ in range(n_g)]
        g = [refs.pop(0) for _ in range(n_g)]
        rows = [refs.pop(0) for _ in range(N_CHUNKS)]
        idx = [refs.pop(0) for _ in range(TOP_K)]
        sem_in, sem_out = refs

        for pg, gg in zip(p, g):
            def move(s, pg=pg, gg=gg):
                loads = [pltpu.async_copy(pg[k].at[pl.ds(s, SC_ROWS)], idx[k], sem_in)
                         for k in range(TOP_K)]
                for cp in loads:
                    cp.wait()
                for k in range(TOP_K):
                    gathers = [pltpu.async_copy(y[j].at[idx[k]], rows[j], sem_in)
                               for j in range(N_CHUNKS)]
                    for cp in gathers:
                        cp.wait()
                    stores = [pltpu.async_copy(rows[j], gg.at[k * N_CHUNKS + j, pl.ds(s, SC_ROWS)], sem_out)
                              for j in range(N_CHUNKS)]
                    for cp in stores:
                        cp.wait()

            _for_each_chunk(pg[0].shape[0], move)

    call = pl.kernel(
        body,
        out_type=tuple(jax.ShapeDtypeStruct((TOP_K * N_CHUNKS, pg[0].shape[0], LANES), I32)
                       for pg in pos_groups),
        mesh=_sc_mesh(), scratch_types=_SC_SCRATCH, name="sc_combine_gather")
    return call(*y_chunks, *[a for g in pos_groups for a in g])


def _expert_kernel(te_ref, tr_ref, nx_ref, x0_ref, x1_ref, x2_ref, x3_ref,
                   wg_hbm, bg_ref, wu_hbm, bu_ref, wd_hbm, bd_ref,
                   y0_ref, y1_ref, y2_ref, y3_ref, stage, w_bf, sem):
    i = pl.program_id(0)

    def weight_copies(e):
        return [pltpu.make_async_copy(w.at[e], stage.at[m], sem.at[m])
                for m, w in enumerate((wg_hbm, wu_hbm, wd_hbm))]

    @pl.when(i == 0)
    def _():
        for cp in weight_copies(te_ref[0]):
            cp.start()

    @pl.when((i == 0) | (te_ref[i] != te_ref[jnp.maximum(i - 1, 0)]))
    def _():
        for cp in weight_copies(te_ref[i]):
            cp.wait()
        for m in range(3):
            w_bf[m] = stage[m].astype(BF16)

        @pl.when(nx_ref[i] >= 0)
        def _():
            for cp in weight_copies(nx_ref[i]):
                cp.start()

    @pl.when(tr_ref[i] > 0)
    def _():
        words = jnp.concatenate([x0_ref[...], x1_ref[...], x2_ref[...], x3_ref[...]], axis=1)
        hi, lo = _unpack_words(words)
        xt = jnp.concatenate([hi, lo], axis=1).astype(BF16)
        gt = jnp.minimum(jnp.dot(xt, w_bf[0], preferred_element_type=F32) + bg_ref[0], SWIGLU_LIMIT)
        up = jnp.clip(jnp.dot(xt, w_bf[1], preferred_element_type=F32) + bu_ref[0],
                      -SWIGLU_LIMIT, SWIGLU_LIMIT)
        act = gt * _sigmoid(SWIGLU_ALPHA * gt) * (up + 1.0)
        y = jnp.dot(act.astype(BF16), w_bf[2], preferred_element_type=F32) + bd_ref[0]
        out = _pack_words(y)
        for j, ref in enumerate((y0_ref, y1_ref, y2_ref, y3_ref)):
            ref[...] = out[:, j * LANES:(j + 1) * LANES]


def _expert_ffn(xs_chunks, tile_expert, tile_rows, next_expert, ew):
    n_rows = xs_chunks[0].shape[0]
    nt_max = n_rows // MOE_TILE
    wg, bg, wu, bu, wd, bd = ew
    rows = pl.BlockSpec((MOE_TILE, LANES), lambda i, te, tr, nx: (i, 0))
    mat = pl.BlockSpec(memory_space=pl.ANY)
    vec = pl.BlockSpec((1, 1, D_MODEL), lambda i, te, tr, nx: (te[i], 0, 0))
    grid_spec = pltpu.PrefetchScalarGridSpec(
        num_scalar_prefetch=3,
        grid=(nt_max,),
        in_specs=[rows] * N_CHUNKS + [mat, vec, mat, vec, mat, vec],
        out_specs=[rows] * N_CHUNKS,
        scratch_shapes=[pltpu.VMEM((3, D_MODEL, D_MODEL), F32),
                        pltpu.VMEM((3, D_MODEL, D_MODEL), BF16),
                        pltpu.SemaphoreType.DMA((3,))],
    )
    return pl.pallas_call(
        _expert_kernel,
        out_shape=[jax.ShapeDtypeStruct((n_rows, LANES), I32)] * N_CHUNKS,
        grid_spec=grid_spec,
        compiler_params=pltpu.CompilerParams(
            dimension_semantics=("arbitrary",), vmem_limit_bytes=VMEM_LIMIT_BYTES),
        name="expert_ffn",
    )(tile_expert, tile_rows, next_expert, *xs_chunks, wg, bg, wu, bu, wd, bd)


def _final_kernel(x1_ref, g_ref, w_ref, mod_ref, fg_ref, *rest):
    y_ref = rest[-1]
    wt = w_ref[...]
    acc = None
    for k in range(TOP_K):
        words = jnp.concatenate([g_ref[k * N_CHUNKS + j] for j in range(N_CHUNKS)], axis=1)
        hi, lo = _unpack_words(words)
        term = wt[:, k:k + 1] * jnp.concatenate([hi, lo], axis=1)
        acc = term if acc is None else acc + term
    g2 = mod_ref[0][5:6]
    x2 = x1_ref[...] + g2 * acc
    ms = jnp.mean(x2 * x2, axis=-1, keepdims=True)
    y_ref[...] = x2 * lax.rsqrt(ms + EPS) * fg_ref[...]


def _final(x1, gathered, wt_tok, mod, final_g, *, tl, tiles_per_batch, b0=0, out_rows=None, y_prev=None):
    n_tok = x1.shape[0]
    out_rows = n_tok if out_rows is None else out_rows
    blk0 = b0 * tiles_per_batch
    in_specs = [
        pl.BlockSpec((tl, D_MODEL), lambda i: (i, 0)),
        pl.BlockSpec((TOP_K * N_CHUNKS, tl, LANES), lambda i: (0, i, 0)),
        pl.BlockSpec((tl, 2 * TOP_K), lambda i: (i, 0)),
        pl.BlockSpec((1, 8, D_MODEL), lambda i: (b0 + i // tiles_per_batch, 0, 0)),
        pl.BlockSpec((1, D_MODEL), lambda i: (0, 0)),
    ]
    args = [x1, gathered, wt_tok, mod, final_g.reshape(1, D_MODEL)]
    aliases = {}
    if y_prev is not None:
        in_specs.append(pl.BlockSpec(memory_space=pl.ANY))
        args.append(y_prev)
        aliases = {len(args) - 1: 0}
    return pl.pallas_call(
        _final_kernel,
        out_shape=jax.ShapeDtypeStruct((out_rows, D_MODEL), F32),
        grid=(n_tok // tl,),
        in_specs=in_specs,
        out_specs=pl.BlockSpec((tl, D_MODEL), lambda i: (blk0 + i, 0)),
        input_output_aliases=aliases,
        compiler_params=pltpu.CompilerParams(
            dimension_semantics=("arbitrary",), vmem_limit_bytes=VMEM_LIMIT_BYTES),
        name="combine_final",
    )(*args)


def _route_rows(r, lo, hi):
    return r[:, lo:hi, :].transpose(1, 0, 2).reshape(hi - lo, -1)


def kernel(x_prompt, x_sample, state_pool, state_conv, c_prompt, c_sample, norm1_g, norm2_g, final_g, w_ada, b_ada, w_in, pool_w, pool_scale, conv_dw, conv_b, conv_ln_g, conv_ln_b, conv_w_out, gate_w, gate_b, w_out, router_w, router_b, exp_w_gate, exp_b_gate, exp_w_up, exp_b_up, exp_w_down, exp_b_down):
    assert norm1_g.shape[0] == 1, "single-layer trunk"
    row = lambda v: v.reshape(1, -1)
    mix_w = (
        row(norm1_g[0]), row(norm2_g[0]),
        w_in[0].astype(BF16), pool_w[0].astype(BF16), row(pool_scale[0]),
        jnp.broadcast_to(conv_dw[0][:, None, :], (CONV_WIDTH, SUBLANES, D_CONV)),
        row(conv_b[0]), row(conv_ln_g[0]), row(conv_ln_b[0]),
        conv_w_out[0].astype(BF16), gate_w[0].astype(BF16), row(gate_b[0]),
        w_out[0].astype(BF16), router_w[0].T.astype(BF16),
        jnp.broadcast_to(router_b[0][:, None], (N_EXPERTS, LANES)),
    )
    ew = (
        exp_w_gate[0], exp_b_gate[0][:, None, :],
        exp_w_up[0], exp_b_up[0][:, None, :],
        exp_w_down[0], exp_b_down[0][:, None, :],
    )
    bp, lp, _ = x_prompt.shape
    bs, ls, _ = x_sample.shape
    t_p = bp * lp
    tl_p, tl_s = 512, ls
    nb_small = max(bp // PROMPT_TAIL_FRACTION, 1)
    chunk_nb = [nb_small, bp - nb_small] if bp > nb_small else [bp]
    chunk_b0 = [sum(chunk_nb[:c]) for c in range(len(chunk_nb))]
    n_chunks = len(chunk_nb)

    mod_p = _modulation(c_prompt, w_ada[0], b_ada[0])
    mod_s = _modulation(c_sample, w_ada[0], b_ada[0])
    pad_state = lambda s, hist: jnp.pad(s, ((0, 0), (hist - s.shape[1], 0), (0, 0)))
    split = lambda out: (out[0], out[1:1 + N_CHUNKS], *out[1 + N_CHUNKS:])

    zero_cnt = jnp.zeros((N_EXPERTS, LANES), F32)
    mixed = [split(_token_mix_skew(
        x_prompt, mod_p, jnp.zeros((nb, POOL_HIST, D_POOL), F32), jnp.zeros((nb, CONV_HIST, D_CONV), F32),
        zero_cnt, mix_w, b0=b0, tl=tl_p, pos0=0, rc=32))
        for b0, nb in zip(chunk_b0, chunk_nb)]
    x1_s, hp_s, ri_s, rw_s, npool_s, nconv_s, cnt_last = split(_token_mix_skew(
        x_sample, mod_s, pad_state(state_pool[0], POOL_HIST), pad_state(state_conv[0], CONV_HIST),
        mixed[-1][6], mix_w, b0=0, tl=tl_s, pos0=PAST_LEN, rc=32))

    def finish(c, g_c, y_prev):
        x1_c, rw_c = mixed[c][0], mixed[c][3]
        return _final(x1_c, g_c, _route_rows(rw_c, 0, 2 * TOP_K).T, mod_p, final_g, tl=tl_p,
                      tiles_per_batch=lp // tl_p, b0=chunk_b0[c], out_rows=t_p, y_prev=y_prev)

    y_p = None
    gathered = []
    for c, (x1_c, hp_c, ri_c, rw_c, _, _, cnt_c) in enumerate(mixed):
        last = c == n_chunks - 1
        n_tok = chunk_nb[c] * lp + (bs * ls if last else 0)
        n_rows = (TOP_K * n_tok // MOE_TILE + N_EXPERTS) * MOE_TILE
        counts = (cnt_last if last else cnt_c)[:, 0].astype(I32)
        tiles_e = (counts + MOE_TILE - 1) // MOE_TILE
        tile_end = jnp.cumsum(tiles_e)
        tile_start = tile_end - tiles_e
        row_off = tile_start * MOE_TILE
        e_ids = jnp.arange(N_EXPERTS, dtype=I32)
        tile_id = jnp.arange(n_rows // MOE_TILE, dtype=I32)
        e_last = jnp.max(jnp.where(tiles_e > 0, e_ids, 0))
        tile_expert = jnp.minimum(
            jnp.sum(tile_id[:, None] >= tile_end[None, :], axis=1), e_last).astype(I32)
        is_e = tile_expert[:, None] == e_ids[None, :]
        left = jnp.sum(jnp.where(is_e, (counts - (tile_id[:, None] - tile_start[None, :]) * MOE_TILE), 0), axis=1)
        tile_rows = jnp.clip(left, 0, MOE_TILE).astype(I32)
        later = (e_ids[None, :] > e_ids[:, None]) & (tiles_e[None, :] > 0)
        next_of = jnp.min(jnp.where(later, e_ids[None, :], N_EXPERTS), axis=1)
        next_of = jnp.where(next_of == N_EXPERTS, -1, next_of)
        next_expert = jnp.sum(jnp.where(is_e, next_of[None, :], 0), axis=1).astype(I32)

        def slots(ri, row_off=row_off):
            ids, ranks = _route_rows(ri, 0, TOP_K), _route_rows(ri, TOP_K, 2 * TOP_K)
            pos = ranks
            for e in range(N_EXPERTS):
                pos = pos + jnp.where(ids == e, row_off[e], 0)
            return [pos[k] for k in range(TOP_K)]

        h_groups = [hp_c] + ([hp_s] if last else [])
        pos_groups = [slots(ri_c)] + ([slots(ri_s)] if last else [])
        xs = _dispatch(h_groups, pos_groups, n_rows)
        ys = _expert_ffn(xs, tile_expert, tile_rows, next_expert, ew)
        if c > 0:
            y_p = finish(c - 1, gathered[c - 1][0], y_p)
        gathered.append(_combine_gather(ys, pos_groups))

    y_p = finish(n_chunks - 1, gathered[-1][0], y_p)
    y_s = _final(x1_s, gathered[-1][1], _route_rows(rw_s, 0, 2 * TOP_K).T, mod_s, final_g, tl=tl_s,
                 tiles_per_batch=1)
    unpad = lambda s, n: s[:, s.shape[1] - n:][None]
    npool_p = jnp.concatenate([m[4] for m in mixed], axis=0)
    nconv_p = jnp.concatenate([m[5] for m in mixed], axis=0)
    return (y_p.reshape(bp, lp, D_MODEL), y_s.reshape(bs, ls, D_MODEL),
            unpad(npool_p, POOL_PAD), unpad(nconv_p, CONV_PAD),
            unpad(npool_s, POOL_PAD), unpad(nconv_s, CONV_PAD))
```

```python
import functools

import jax
import jax.numpy as jnp
from jax import lax
from jax.experimental import pallas as pl
from jax.experimental.pallas import tpu as pltpu
from jax.experimental.pallas import tpu_sc as plsc

D_MODEL = 1024
D_POOL = 512
D_CONV = 512
POOL_WINDOWS = (2, 4, 8, 16)
POOL_GW = 128
POOL_OUT_GW = 256
POOL_PAD = 15
CONV_WIDTH = 31
CONV_PAD = 30
N_EXPERTS = 32
TOP_K = 4
SWIGLU_LIMIT = 7.0
SWIGLU_ALPHA = 1.702
EPS = 1e-6
PAST_LEN = 2048

POOL_HIST = 16
CONV_HIST = 32
VMEM_LIMIT_BYTES = 56 * 1024 * 1024

LANES = 128
SUBLANES = 8
D_WORDS = D_MODEL // 2
N_CHUNKS = D_WORDS // LANES
MOE_TILE = 512
PROMPT_TAIL_FRACTION = 4
SC_CORES = 2
SC_SUBCORES = 16
SC_WORKERS = SC_CORES * SC_SUBCORES
SC_ROWS = 128

F32 = jnp.float32
BF16 = jnp.bfloat16
I32 = jnp.int32
HI_MASK = -65536
NEG_LOG2E = -1.4426950408889634


def _sigmoid(v):
    return 1.0 / (1.0 + jnp.exp2(v * NEG_LOG2E))


def _pack_words(v):
    r = v.astype(BF16).astype(F32)
    hi = lax.bitcast_convert_type(r[:, :D_WORDS], I32)
    lo = lax.bitcast_convert_type(r[:, D_WORDS:], I32)
    return (hi & HI_MASK) | lax.shift_right_logical(lo, 16)


def _unpack_words(w):
    hi = lax.bitcast_convert_type(w & HI_MASK, F32)
    lo = lax.bitcast_convert_type(lax.shift_left(w, 16), F32)
    return hi, lo


def _mod_kernel(c_ref, w_ref, b_ref, o_ref):
    c = c_ref[...]
    s = (c * _sigmoid(c)).astype(BF16)
    o_ref[...] = jnp.dot(s, w_ref[...].astype(BF16), preferred_element_type=F32) + b_ref[...]


def _modulation(c, w_ada, b_ada):
    bsz = c.shape[0]
    out = pl.pallas_call(
        _mod_kernel,
        out_shape=jax.ShapeDtypeStruct((bsz, 6 * D_MODEL), F32),
        grid=(6,),
        in_specs=[
            pl.BlockSpec((bsz, D_MODEL), lambda j: (0, 0)),
            pl.BlockSpec((D_MODEL, D_MODEL), lambda j: (0, j)),
            pl.BlockSpec((1, D_MODEL), lambda j: (0, j)),
        ],
        out_specs=pl.BlockSpec((bsz, D_MODEL), lambda j: (0, j)),
        compiler_params=pltpu.CompilerParams(
            dimension_semantics=("arbitrary",), vmem_limit_bytes=VMEM_LIMIT_BYTES),
        name="adaln_mod",
    )(c, w_ada, b_ada.reshape(1, 6 * D_MODEL))
    out = out.reshape(bsz, 6, D_MODEL)
    return jnp.pad(out, ((0, 0), (0, 2), (0, 0)))


def _mix_kernel(xh_ref, xt_ref, modh_ref, modt_ref, pinit_ref, cinit_ref, cntin_ref, tri_ref,
                 n1_ref, n2_ref, win_ref, pw_ref, ps_ref, dw_ref, cb_ref, lg_ref, lb_ref,
                 cwo_ref, gw_ref, gb_ref, wo_ref, rwt_ref, rb_ref,
                 x1_ref, hp0_ref, hp1_ref, hp2_ref, hp3_ref, ri_ref, rw_ref,
                 npool_ref, nconv_ref, cntout_ref,
                 conv_sh, pool_in, hbuf, ybuf, h_prev, sy_prev, pooled_prev, merged, cnt_run,
                 *, tl, nl, n_tiles, pos0, rc):
    g = pl.program_id(0)
    gh = jnp.minimum(g, n_tiles - 1)
    l = lax.rem(gh, nl)
    head_live = g < n_tiles
    tail_live = g >= 1
    conv_in = conv_sh.at[0]

    @pl.when(g == 0)
    def _():
        cnt_run[...] = cntin_ref[...]
        h_prev[...] = jnp.zeros_like(h_prev)
        sy_prev[...] = jnp.zeros_like(sy_prev)
        pooled_prev[...] = jnp.zeros_like(pooled_prev)

    @pl.when(l == 0)
    def _():
        conv_in[0:CONV_HIST, :] = cinit_ref[0]
        pool_in[0:POOL_HIST, :] = pinit_ref[0]

    modh = modh_ref[0]
    sh1, sc1 = modh[0:1], modh[1:2]
    x = xh_ref[...]
    ms = jnp.mean(x * x, axis=-1, keepdims=True)
    hbuf[...] = ((x * lax.rsqrt(ms + EPS)) * (n1_ref[...] * (1.0 + sc1)) + sh1).astype(BF16)

    proj = jnp.dot(hbuf[...], win_ref[...], preferred_element_type=F32)
    glu = proj[:, D_POOL:D_POOL + D_CONV] * _sigmoid(proj[:, D_POOL + D_CONV:])
    pool_in[POOL_HIST:POOL_HIST + tl, :] = proj[:, :D_POOL]
    conv_in[CONV_HIST:CONV_HIST + tl, :] = glu

    modt = modt_ref[0]
    g1, sh2, sc2 = modt[2:3], modt[3:4], modt[4:5]
    for j in range(len(POOL_WINDOWS)):
        cs = slice(j * POOL_OUT_GW, (j + 1) * POOL_OUT_GW)
        cs2 = slice(D_MODEL + j * POOL_OUT_GW, D_MODEL + (j + 1) * POOL_OUT_GW)
        ga = _sigmoid(jnp.dot(h_prev[...], gw_ref[:, cs], preferred_element_type=F32) + gb_ref[:, cs])
        gb = _sigmoid(jnp.dot(h_prev[...], gw_ref[:, cs2], preferred_element_type=F32) + gb_ref[:, cs2])
        a_j = jnp.dot(pooled_prev[:, j * POOL_GW:(j + 1) * POOL_GW], pw_ref[j],
                      preferred_element_type=F32) * ps_ref[:, cs]
        b_j = jnp.dot(sy_prev[...], cwo_ref[:, cs], preferred_element_type=F32)
        merged[:, cs] = (ga * a_j + gb * b_j).astype(BF16)

    x1 = xt_ref[...] + g1 * jnp.dot(merged[...], wo_ref[...], preferred_element_type=F32)
    x1_ref[...] = x1
    ms2 = jnp.mean(x1 * x1, axis=-1, keepdims=True)
    h2 = (x1 * lax.rsqrt(ms2 + EPS)) * (n2_ref[...] * (1.0 + sc2)) + sh2
    h2b = h2.astype(BF16)
    words = _pack_words(h2)
    for j, ref in enumerate((hp0_ref, hp1_ref, hp2_ref, hp3_ref)):
        ref[...] = words[:, j * LANES:(j + 1) * LANES]

    logits = lax.dot_general(rwt_ref[...], h2b, (((1,), (1,)), ((), ())),
                             preferred_element_type=F32) + rb_ref[:, 0:1]
    e_iota = lax.broadcasted_iota(I32, (N_EXPERTS, tl), 0)
    v = logits
    ids, vals = [], []
    for _ in range(TOP_K):
        m = jnp.max(v, axis=0, keepdims=True)
        idx = jnp.min(jnp.where(v == m, e_iota, N_EXPERTS), axis=0, keepdims=True)
        ids.append(idx)
        vals.append(m)
        v = jnp.where(e_iota == idx, -jnp.inf, v)
    ex = [jnp.exp(vk - vals[0]) for vk in vals]
    den = ex[0] + ex[1] + ex[2] + ex[3]
    sel = [e_iota == idx for idx in ids]
    chosen = jnp.where((sel[0] | sel[1] | sel[2] | sel[3]) & tail_live, 1.0, 0.0)
    before = jnp.dot(chosen.astype(BF16), tri_ref[...], preferred_element_type=F32)
    rank_all = cnt_run[:, 0:1] + before
    for k in range(TOP_K):
        ri_ref[0, k:k + 1, :] = ids[k]
        rk = jnp.sum(jnp.where(sel[k], rank_all, 0.0), axis=0, keepdims=True)
        ri_ref[0, TOP_K + k:TOP_K + k + 1, :] = rk.astype(I32)
        rw_ref[0, k:k + 1, :] = ex[k] / den
        rw_ref[0, TOP_K + k:TOP_K + k + 1, :] = jnp.zeros((1, tl), F32)
    cnt_run[...] = cnt_run[...] + jnp.sum(chosen, axis=1, keepdims=True)
    cntout_ref[...] = cnt_run[...]

    pos = pos0 + l * tl + lax.broadcasted_iota(I32, (tl, POOL_GW), 0)
    pooled = []
    for gi, w in enumerate(POOL_WINDOWS):
        lanes = slice(gi * POOL_GW, (gi + 1) * POOL_GW)
        s = pool_in[:, lanes]
        d = 1
        while d < w:
            s = s + pltpu.roll(s, d, axis=0)
            d *= 2
        cur = pool_in[POOL_HIST:POOL_HIST + tl, lanes]
        cnt = jnp.minimum(pos + 1, w).astype(F32)
        pooled.append((s[POOL_HIST:] / cnt - cur).astype(BF16))

    n_sh = tl + CONV_HIST - SUBLANES
    for r in range(1, SUBLANES):
        conv_sh[r, 0:n_sh, :] = conv_in[r:r + n_sh, :]

    for base in range(0, tl, rc):
        acc = jnp.broadcast_to(cb_ref[...], (rc // SUBLANES, SUBLANES, D_CONV))
        for k in range(CONV_WIDTH):
            q, r = divmod(k + CONV_HIST - CONV_PAD, SUBLANES)
            start = base + q * SUBLANES
            win = conv_sh[r, start:start + rc, :].reshape(rc // SUBLANES, SUBLANES, D_CONV)
            acc = acc + win * dw_ref[k]
        ybuf[base:base + rc, :] = acc.reshape(rc, D_CONV)

    yc = ybuf[...]
    mu = jnp.mean(yc, axis=-1, keepdims=True)
    dev = yc - mu
    var = jnp.mean(dev * dev, axis=-1, keepdims=True)
    yn = dev * lax.rsqrt(var + EPS) * lg_ref[...] + lb_ref[...]

    h_prev[...] = hbuf[...]
    sy_prev[...] = (yn * _sigmoid(yn)).astype(BF16)
    for gi in range(len(POOL_WINDOWS)):
        pooled_prev[:, gi * POOL_GW:(gi + 1) * POOL_GW] = pooled[gi]

    @pl.when(head_live)
    def _():
        npool_ref[0] = pool_in[tl:tl + POOL_HIST, :]
        nconv_ref[0] = conv_in[tl:tl + CONV_HIST, :]

    conv_in[0:CONV_HIST, :] = conv_in[tl:tl + CONV_HIST, :]
    pool_in[0:POOL_HIST, :] = pool_in[tl:tl + POOL_HIST, :]


def _token_mix(x, mod, pool_init, conv_init, cnt_in, wts, *, b0, tl, pos0, rc):
    _, seq, _ = x.shape
    bsz = pool_init.shape[0]
    assert seq % tl == 0 and tl % rc == 0 and tl >= CONV_HIST
    nl = seq // tl
    n_tiles = bsz * nl
    rows_out = bsz * seq
    x2d = x.reshape(-1, D_MODEL)
    tri = jnp.triu(jnp.ones((tl, tl), BF16), k=1)
    head = lambda g: jnp.minimum(g, n_tiles - 1)
    tail = lambda g: jnp.maximum(g - 1, 0)
    full = lambda shape: pl.BlockSpec(shape, lambda g: (0,) * len(shape))
    per_b = lambda shape: pl.BlockSpec(shape, lambda g: (head(g) // nl,) + (0,) * (len(shape) - 1))
    tok_t = lambda width: pl.BlockSpec((tl, width), lambda g: (tail(g), 0))
    rt = lambda shape: pl.BlockSpec(shape, lambda g: (tail(g), 0, 0))
    in_specs = [
        pl.BlockSpec((tl, D_MODEL), lambda g: (b0 * nl + head(g), 0)),
        pl.BlockSpec((tl, D_MODEL), lambda g: (b0 * nl + tail(g), 0)),
        pl.BlockSpec((1, 8, D_MODEL), lambda g: (b0 + head(g) // nl, 0, 0)),
        pl.BlockSpec((1, 8, D_MODEL), lambda g: (b0 + tail(g) // nl, 0, 0)),
        per_b((1, POOL_HIST, D_POOL)),
        per_b((1, CONV_HIST, D_CONV)),
        full((N_EXPERTS, LANES)),
        full((tl, tl)),
        full((1, D_MODEL)), full((1, D_MODEL)),
        full((D_MODEL, D_POOL + 2 * D_CONV)),
        full((len(POOL_WINDOWS), POOL_GW, POOL_OUT_GW)),
        full((1, D_MODEL)),
        full((CONV_WIDTH, SUBLANES, D_CONV)),
        full((1, D_CONV)), full((1, D_CONV)), full((1, D_CONV)),
        full((D_CONV, D_MODEL)),
        full((D_MODEL, 2 * D_MODEL)),
        full((1, 2 * D_MODEL)),
        full((D_MODEL, D_MODEL)),
        full((N_EXPERTS, D_MODEL)),
        full((N_EXPERTS, LANES)),
    ]
    out_shape = (
        jax.ShapeDtypeStruct((rows_out, D_MODEL), F32),
        *[jax.ShapeDtypeStruct((rows_out, LANES), I32)] * N_CHUNKS,
        jax.ShapeDtypeStruct((n_tiles, 2 * TOP_K, tl), I32),
        jax.ShapeDtypeStruct((n_tiles, 2 * TOP_K, tl), F32),
        jax.ShapeDtypeStruct((bsz, POOL_HIST, D_POOL), F32),
        jax.ShapeDtypeStruct((bsz, CONV_HIST, D_CONV), F32),
        jax.ShapeDtypeStruct((N_EXPERTS, LANES), F32),
    )
    out_specs = (
        tok_t(D_MODEL),
        *[tok_t(LANES)] * N_CHUNKS,
        rt((1, 2 * TOP_K, tl)),
        rt((1, 2 * TOP_K, tl)),
        per_b((1, POOL_HIST, D_POOL)),
        per_b((1, CONV_HIST, D_CONV)),
        full((N_EXPERTS, LANES)),
    )
    scratch = [
        pltpu.VMEM((SUBLANES, tl + CONV_HIST, D_CONV), F32),
        pltpu.VMEM((tl + POOL_HIST, D_POOL), F32),
        pltpu.VMEM((tl, D_MODEL), BF16),
        pltpu.VMEM((tl, D_CONV), F32),
        pltpu.VMEM((tl, D_MODEL), BF16),
        pltpu.VMEM((tl, D_CONV), BF16),
        pltpu.VMEM((tl, D_POOL), BF16),
        pltpu.VMEM((tl, D_MODEL), BF16),
        pltpu.VMEM((N_EXPERTS, LANES), F32),
    ]
    return pl.pallas_call(
        functools.partial(_mix_kernel, tl=tl, nl=nl, n_tiles=n_tiles, pos0=pos0, rc=rc),
        out_shape=out_shape,
        grid=(n_tiles + 1,),
        in_specs=in_specs,
        out_specs=out_specs,
        scratch_shapes=scratch,
        compiler_params=pltpu.CompilerParams(
            dimension_semantics=("arbitrary",), vmem_limit_bytes=VMEM_LIMIT_BYTES),
        name="token_mix",
    )(x2d, x2d, mod, mod, pool_init, conv_init, cnt_in, tri, *wts)


def _sc_mesh():
    return plsc.VectorSubcoreMesh(core_axis_name="c", subcore_axis_name="s",
                                  num_cores=SC_CORES, num_subcores=SC_SUBCORES)


def _for_each_chunk(n_tok, fn):
    wid = lax.axis_index("s") * SC_CORES + lax.axis_index("c")
    if n_tok % (8 * SC_WORKERS) == 0 and n_tok // SC_WORKERS >= SC_ROWS:
        per_w = n_tok // SC_WORKERS

        @pl.loop(0, pl.cdiv(per_w, SC_ROWS))
        def _(c):
            fn(pl.multiple_of(wid * per_w + jnp.minimum(c * SC_ROWS, per_w - SC_ROWS), 8))
    else:
        assert n_tok % SC_ROWS == 0 and n_tok // SC_ROWS <= SC_WORKERS

        @pl.when(wid < n_tok // SC_ROWS)
        def _():
            fn(pl.multiple_of(wid * SC_ROWS, SC_ROWS))


_SC_SCRATCH = ([pltpu.VMEM((SC_ROWS, LANES), I32)] * N_CHUNKS
               + [pltpu.VMEM((SC_ROWS,), I32)] * TOP_K
               + [pltpu.SemaphoreType.DMA, pltpu.SemaphoreType.DMA])


def _dispatch(h_groups, pos_groups, n_rows):
    n_g = len(h_groups)

    def body(*refs):
        refs = list(refs)
        h = [[refs.pop(0) for _ in range(N_CHUNKS)] for _ in range(n_g)]
        p = [[refs.pop(0) for _ in range(TOP_K)] for _ in range(n_g)]
        o = [refs.pop(0) for _ in range(N_CHUNKS)]
        rows = [refs.pop(0) for _ in range(N_CHUNKS)]
        idx = [refs.pop(0) for _ in range(TOP_K)]
        sem_in, sem_out = refs

        for hg, pg in zip(h, p):
            def move(s, hg=hg, pg=pg):
                loads = [pltpu.async_copy(hg[j].at[pl.ds(s, SC_ROWS)], rows[j], sem_in)
                         for j in range(N_CHUNKS)]
                loads += [pltpu.async_copy(pg[k].at[pl.ds(s, SC_ROWS)], idx[k], sem_in)
                          for k in range(TOP_K)]
                for cp in loads:
                    cp.wait()
                stores = [pltpu.async_copy(rows[j], o[j].at[idx[k]], sem_out)
                          for j in range(N_CHUNKS) for k in range(TOP_K)]
                for cp in stores:
                    cp.wait()

            _for_each_chunk(hg[0].shape[0], move)

    call = pl.kernel(
        body,
        out_type=tuple(jax.ShapeDtypeStruct((n_rows, LANES), I32) for _ in range(N_CHUNKS)),
        mesh=_sc_mesh(), scratch_types=_SC_SCRATCH, name="sc_dispatch")
    flat = [a for g in h_groups for a in g] + [a for g in pos_groups for a in g]
    return call(*flat)


def _combine_gather(y_chunks, pos_groups):
    n_g = len(pos_groups)

    def body(*refs):
        refs = list(refs)
        y = [refs.pop(0) for _ in range(N_CHUNKS)]
        p = [[refs.pop(0) for _ in range(TOP_K)] for _ in range(n_g)]
        g = [refs.pop(0) for _ in range(n_g)]
        rows = [refs.pop(0) for _ in range(N_CHUNKS)]
        idx = [refs.pop(0) for _ in range(TOP_K)]
        sem_in, sem_out = refs

        for pg, gg in zip(p, g):
            def move(s, pg=pg, gg=gg):
                loads = [pltpu.async_copy(pg[k].at[pl.ds(s, SC_ROWS)], idx[k], sem_in)
                         for k in range(TOP_K)]
                for cp in loads:
                    cp.wait()
                for k in range(TOP_K):
                    gathers = [pltpu.async_copy(y[j].at[idx[k]], rows[j], sem_in)
                               for j in range(N_CHUNKS)]
                    for cp in gathers:
                        cp.wait()
                    stores = [pltpu.async_copy(rows[j], gg.at[k * N_CHUNKS + j, pl.ds(s, SC_ROWS)], sem_out)
                              for j in range(N_CHUNKS)]
                    for cp in stores:
                        cp.wait()

            _for_each_chunk(pg[0].shape[0], move)

    call = pl.kernel(
        body,
        out_type=tuple(jax.ShapeDtypeStruct((TOP_K * N_CHUNKS, pg[0].shape[0], LANES), I32)
                       for pg in pos_groups),
        mesh=_sc_mesh(), scratch_types=_SC_SCRATCH, name="sc_combine_gather")
    return call(*y_chunks, *[a for g in pos_groups for a in g])


def _expert_kernel(te_ref, tr_ref, nx_ref, x0_ref, x1_ref, x2_ref, x3_ref,
                   wg_hbm, bg_ref, wu_hbm, bu_ref, wd_hbm, bd_ref,
                   y0_ref, y1_ref, y2_ref, y3_ref, stage, w_bf, sem):
    i = pl.program_id(0)

    def weight_copies(e):
        return [pltpu.make_async_copy(w.at[e], stage.at[m], sem.at[m])
                for m, w in enumerate((wg_hbm, wu_hbm, wd_hbm))]

    @pl.when(i == 0)
    def _():
        for cp in weight_copies(te_ref[0]):
            cp.start()

    @pl.when((i == 0) | (te_ref[i] != te_ref[jnp.maximum(i - 1, 0)]))
    def _():
        for cp in weight_copies(te_ref[i]):
            cp.wait()
        for m in range(3):
            w_bf[m] = stage[m].astype(BF16)

        @pl.when(nx_ref[i] >= 0)
        def _():
            for cp in weight_copies(nx_ref[i]):
                cp.start()

    @pl.when(tr_ref[i] > 0)
    def _():
        words = jnp.concatenate([x0_ref[...], x1_ref[...], x2_ref[...], x3_ref[...]], axis=1)
        hi, lo = _unpack_words(words)
        xt = jnp.concatenate([hi, lo], axis=1).astype(BF16)
        gt = jnp.minimum(jnp.dot(xt, w_bf[0], preferred_element_type=F32) + bg_ref[0], SWIGLU_LIMIT)
        up = jnp.clip(jnp.dot(xt, w_bf[1], preferred_element_type=F32) + bu_ref[0],
                      -SWIGLU_LIMIT, SWIGLU_LIMIT)
        act = gt * _sigmoid(SWIGLU_ALPHA * gt) * (up + 1.0)
        y = jnp.dot(act.astype(BF16), w_bf[2], preferred_element_type=F32) + bd_ref[0]
        out = _pack_words(y)
        for j, ref in enumerate((y0_ref, y1_ref, y2_ref, y3_ref)):
            ref[...] = out[:, j * LANES:(j + 1) * LANES]


def _expert_ffn(xs_chunks, tile_expert, tile_rows, next_expert, ew):
    n_rows = xs_chunks[0].shape[0]
    nt_max = n_rows // MOE_TILE
    wg, bg, wu, bu, wd, bd = ew
    rows = pl.BlockSpec((MOE_TILE, LANES), lambda i, te, tr, nx: (i, 0))
    mat = pl.BlockSpec(memory_space=pl.ANY)
    vec = pl.BlockSpec((1, 1, D_MODEL), lambda i, te, tr, nx: (te[i], 0, 0))
    grid_spec = pltpu.PrefetchScalarGridSpec(
        num_scalar_prefetch=3,
        grid=(nt_max,),
        in_specs=[rows] * N_CHUNKS + [mat, vec, mat, vec, mat, vec],
        out_specs=[rows] * N_CHUNKS,
        scratch_shapes=[pltpu.VMEM((3, D_MODEL, D_MODEL), F32),
                        pltpu.VMEM((3, D_MODEL, D_MODEL), BF16),
                        pltpu.SemaphoreType.DMA((3,))],
    )
    return pl.pallas_call(
        _expert_kernel,
        out_shape=[jax.ShapeDtypeStruct((n_rows, LANES), I32)] * N_CHUNKS,
        grid_spec=grid_spec,
        compiler_params=pltpu.CompilerParams(
            dimension_semantics=("arbitrary",), vmem_limit_bytes=VMEM_LIMIT_BYTES),
        name="expert_ffn",
    )(tile_expert, tile_rows, next_expert, *xs_chunks, wg, bg, wu, bu, wd, bd)


def _final_kernel(x1_ref, g_ref, w_ref, mod_ref, fg_ref, *rest):
    y_ref = rest[-1]
    wt = w_ref[...]
    acc = None
    for k in range(TOP_K):
        words = jnp.concatenate([g_ref[k * N_CHUNKS + j] for j in range(N_CHUNKS)], axis=1)
        hi, lo = _unpack_words(words)
        term = wt[:, k:k + 1] * jnp.concatenate([hi, lo], axis=1)
        acc = term if acc is None else acc + term
    g2 = mod_ref[0][5:6]
    x2 = x1_ref[...] + g2 * acc
    ms = jnp.mean(x2 * x2, axis=-1, keepdims=True)
    y_ref[...] = x2 * lax.rsqrt(ms + EPS) * fg_ref[...]


def _final(x1, gathered, wt_tok, mod, final_g, *, tl, tiles_per_batch, b0=0, out_rows=None, y_prev=None):
    n_tok = x1.shape[0]
    out_rows = n_tok if out_rows is None else out_rows
    blk0 = b0 * tiles_per_batch
    in_specs = [
        pl.BlockSpec((tl, D_MODEL), lambda i: (i, 0)),
        pl.BlockSpec((TOP_K * N_CHUNKS, tl, LANES), lambda i: (0, i, 0)),
        pl.BlockSpec((tl, 2 * TOP_K), lambda i: (i, 0)),
        pl.BlockSpec((1, 8, D_MODEL), lambda i: (b0 + i // tiles_per_batch, 0, 0)),
        pl.BlockSpec((1, D_MODEL), lambda i: (0, 0)),
    ]
    args = [x1, gathered, wt_tok, mod, final_g.reshape(1, D_MODEL)]
    aliases = {}
    if y_prev is not None:
        in_specs.append(pl.BlockSpec(memory_space=pl.ANY))
        args.append(y_prev)
        aliases = {len(args) - 1: 0}
    return pl.pallas_call(
        _final_kernel,
        out_shape=jax.ShapeDtypeStruct((out_rows, D_MODEL), F32),
        grid=(n_tok // tl,),
        in_specs=in_specs,
        out_specs=pl.BlockSpec((tl, D_MODEL), lambda i: (blk0 + i, 0)),
        input_output_aliases=aliases,
        compiler_params=pltpu.CompilerParams(
            dimension_semantics=("arbitrary",), vmem_limit_bytes=VMEM_LIMIT_BYTES),
        name="combine_final",
    )(*args)


def _route_rows(r, lo, hi):
    return r[:, lo:hi, :].transpose(1, 0, 2).reshape(hi - lo, -1)


def kernel(x_prompt, x_sample, state_pool, state_conv, c_prompt, c_sample, norm1_g, norm2_g, final_g, w_ada, b_ada, w_in, pool_w, pool_scale, conv_dw, conv_b, conv_ln_g, conv_ln_b, conv_w_out, gate_w, gate_b, w_out, router_w, router_b, exp_w_gate, exp_b_gate, exp_w_up, exp_b_up, exp_w_down, exp_b_down):
    assert norm1_g.shape[0] == 1, "single-layer trunk"
    row = lambda v: v.reshape(1, -1)
    mix_w = (
        row(norm1_g[0]), row(norm2_g[0]),
        w_in[0].astype(BF16), pool_w[0].astype(BF16), row(pool_scale[0]),
        jnp.broadcast_to(conv_dw[0][:, None, :], (CONV_WIDTH, SUBLANES, D_CONV)),
        row(conv_b[0]), row(conv_ln_g[0]), row(conv_ln_b[0]),
        conv_w_out[0].astype(BF16), gate_w[0].astype(BF16), row(gate_b[0]),
        w_out[0].astype(BF16), router_w[0].T.astype(BF16),
        jnp.broadcast_to(router_b[0][:, None], (N_EXPERTS, LANES)),
    )
    ew = (
        exp_w_gate[0], exp_b_gate[0][:, None, :],
        exp_w_up[0], exp_b_up[0][:, None, :],
        exp_w_down[0], exp_b_down[0][:, None, :],
    )
    bp, lp, _ = x_prompt.shape
    bs, ls, _ = x_sample.shape
    t_p = bp * lp
    tl_p, tl_s = 512, ls
    nb_small = max(bp // PROMPT_TAIL_FRACTION, 1)
    chunk_nb = [nb_small, bp - nb_small] if bp > nb_small else [bp]
    chunk_b0 = [sum(chunk_nb[:c]) for c in range(len(chunk_nb))]
    n_chunks = len(chunk_nb)

    mod_p = _modulation(c_prompt, w_ada[0], b_ada[0])
    mod_s = _modulation(c_sample, w_ada[0], b_ada[0])
    pad_state = lambda s, hist: jnp.pad(s, ((0, 0), (hist - s.shape[1], 0), (0, 0)))
    split = lambda out: (out[0], out[1:1 + N_CHUNKS], *out[1 + N_CHUNKS:])

    zero_cnt = jnp.zeros((N_EXPERTS, LANES), F32)
    mixed = [split(_token_mix(
        x_prompt, mod_p, jnp.zeros((nb, POOL_HIST, D_POOL), F32), jnp.zeros((nb, CONV_HIST, D_CONV), F32),
        zero_cnt, mix_w, b0=b0, tl=tl_p, pos0=0, rc=32))
        for b0, nb in zip(chunk_b0, chunk_nb)]
    x1_s, hp_s, ri_s, rw_s, npool_s, nconv_s, cnt_last = split(_token_mix(
        x_sample, mod_s, pad_state(state_pool[0], POOL_HIST), pad_state(state_conv[0], CONV_HIST),
        mixed[-1][6], mix_w, b0=0, tl=tl_s, pos0=PAST_LEN, rc=32))

    def finish(c, g_c, y_prev):
        x1_c, rw_c = mixed[c][0], mixed[c][3]
        return _final(x1_c, g_c, _route_rows(rw_c, 0, 2 * TOP_K).T, mod_p, final_g, tl=tl_p,
                      tiles_per_batch=lp // tl_p, b0=chunk_b0[c], out_rows=t_p, y_prev=y_prev)

    gathered = []
    for c, (x1_c, hp_c, ri_c, rw_c, _, _, cnt_c) in enumerate(mixed):
        last = c == n_chunks - 1
        n_tok = chunk_nb[c] * lp + (bs * ls if last else 0)
        n_rows = (TOP_K * n_tok // MOE_TILE + N_EXPERTS) * MOE_TILE
        counts = (cnt_last if last else cnt_c)[:, 0].astype(I32)
        tiles_e = (counts + MOE_TILE - 1) // MOE_TILE
        tile_end = jnp.cumsum(tiles_e)
        tile_start = tile_end - tiles_e
        row_off = tile_start * MOE_TILE
        e_ids = jnp.arange(N_EXPERTS, dtype=I32)
        tile_id = jnp.arange(n_rows // MOE_TILE, dtype=I32)
        e_last = jnp.max(jnp.where(tiles_e > 0, e_ids, 0))
        tile_expert = jnp.minimum(
            jnp.sum(tile_id[:, None] >= tile_end[None, :], axis=1), e_last).astype(I32)
        is_e = tile_expert[:, None] == e_ids[None, :]
        left = jnp.sum(jnp.where(is_e, (counts - (tile_id[:, None] - tile_start[None, :]) * MOE_TILE), 0), axis=1)
        tile_rows = jnp.clip(left, 0, MOE_TILE).astype(I32)
        later = (e_ids[None, :] > e_ids[:, None]) & (tiles_e[None, :] > 0)
        next_of = jnp.min(jnp.where(later, e_ids[None, :], N_EXPERTS), axis=1)
        next_of = jnp.where(next_of == N_EXPERTS, -1, next_of)
        next_expert = jnp.sum(jnp.where(is_e, next_of[None, :], 0), axis=1).astype(I32)

        def slots(ri, row_off=row_off):
            ids, ranks = _route_rows(ri, 0, TOP_K), _route_rows(ri, TOP_K, 2 * TOP_K)
            pos = ranks
            for e in range(N_EXPERTS):
                pos = pos + jnp.where(ids == e, row_off[e], 0)
            return [pos[k] for k in range(TOP_K)]

        h_groups = [hp_c] + ([hp_s] if last else [])
        pos_groups = [slots(ri_c)] + ([slots(ri_s)] if last else [])
        xs = _dispatch(h_groups, pos_groups, n_rows)
        ys = _expert_ffn(xs, tile_expert, tile_rows, next_expert, ew)
        gathered.append(_combine_gather(ys, pos_groups))

    y_p = None
    for c in reversed(range(n_chunks)):
        y_p = finish(c, gathered[c][0], y_p)
    y_s = _final(x1_s, gathered[-1][1], _route_rows(rw_s, 0, 2 * TOP_K).T, mod_s, final_g, tl=tl_s,
                 tiles_per_batch=1)
    unpad = lambda s, n: s[:, s.shape[1] - n:][None]
    npool_p = jnp.concatenate([m[4] for m in mixed], axis=0)
    nconv_p = jnp.concatenate([m[5] for m in mixed], axis=0)
    return (y_p.reshape(bp, lp, D_MODEL), y_s.reshape(bs, ls, D_MODEL),
            unpad(npool_p, POOL_PAD), unpad(nconv_p, CONV_PAD),
            unpad(npool_s, POOL_PAD), unpad(nconv_s, CONV_PAD))
```

```python
import functools

import jax
import jax.numpy as jnp
from jax import lax
from jax.experimental import pallas as pl
from jax.experimental.pallas import tpu as pltpu
from jax.experimental.pallas import tpu_sc as plsc

D_MODEL = 1024
D_POOL = 512
D_CONV = 512
POOL_WINDOWS = (2, 4, 8, 16)
POOL_GW = 128
POOL_OUT_GW = 256
POOL_PAD = 15
CONV_WIDTH = 31
CONV_PAD = 30
N_EXPERTS = 32
TOP_K = 4
SWIGLU_LIMIT = 7.0
SWIGLU_ALPHA = 1.702
EPS = 1e-6
PAST_LEN = 2048

POOL_HIST = 16
CONV_HIST = 32
VMEM_LIMIT_BYTES = 56 * 1024 * 1024

LANES = 128
SUBLANES = 8
D_WORDS = D_MODEL // 2
N_CHUNKS = D_WORDS // LANES
MOE_TILE = 512
PROMPT_TAIL_FRACTION = 4
SC_CORES = 2
SC_SUBCORES = 16
SC_WORKERS = SC_CORES * SC_SUBCORES
SC_ROWS = 128

F32 = jnp.float32
BF16 = jnp.bfloat16
I32 = jnp.int32
HI_MASK = -65536
NEG_LOG2E = -1.4426950408889634


def _sigmoid(v):
    return 1.0 / (1.0 + jnp.exp2(v * NEG_LOG2E))


def _pack_words(v):
    r = v.astype(BF16).astype(F32)
    hi = lax.bitcast_convert_type(r[:, :D_WORDS], I32)
    lo = lax.bitcast_convert_type(r[:, D_WORDS:], I32)
    return (hi & HI_MASK) | lax.shift_right_logical(lo, 16)


def _unpack_words(w):
    hi = lax.bitcast_convert_type(w & HI_MASK, F32)
    lo = lax.bitcast_convert_type(lax.shift_left(w, 16), F32)
    return hi, lo


def _mod_kernel(c_ref, w_ref, b_ref, o_ref):
    c = c_ref[...]
    s = (c * _sigmoid(c)).astype(BF16)
    o_ref[...] = jnp.dot(s, w_ref[...].astype(BF16), preferred_element_type=F32) + b_ref[...]


def _modulation(c, w_ada, b_ada):
    bsz = c.shape[0]
    out = pl.pallas_call(
        _mod_kernel,
        out_shape=jax.ShapeDtypeStruct((bsz, 6 * D_MODEL), F32),
        grid=(6,),
        in_specs=[
            pl.BlockSpec((bsz, D_MODEL), lambda j: (0, 0)),
            pl.BlockSpec((D_MODEL, D_MODEL), lambda j: (0, j)),
            pl.BlockSpec((1, D_MODEL), lambda j: (0, j)),
        ],
        out_specs=pl.BlockSpec((bsz, D_MODEL), lambda j: (0, j)),
        compiler_params=pltpu.CompilerParams(
            dimension_semantics=("arbitrary",), vmem_limit_bytes=VMEM_LIMIT_BYTES),
        name="adaln_mod",
    )(c, w_ada, b_ada.reshape(1, 6 * D_MODEL))
    out = out.reshape(bsz, 6, D_MODEL)
    return jnp.pad(out, ((0, 0), (0, 2), (0, 0)))


def _mix_kernel(xh_ref, xt_ref, modh_ref, modt_ref, pinit_ref, cinit_ref, cntin_ref, tri_ref,
                 n1_ref, n2_ref, win_ref, pw_ref, ps_ref, dw_ref, cb_ref, lg_ref, lb_ref,
                 cwo_ref, gw_ref, gb_ref, wo_ref, rwt_ref, rb_ref,
                 x1_ref, hp0_ref, hp1_ref, hp2_ref, hp3_ref, ri_ref, rw_ref,
                 npool_ref, nconv_ref, cntout_ref,
                 conv_sh, pool_in, hbuf, ybuf, h_prev, sy_prev, pooled_prev, merged, cnt_run,
                 *, tl, nl, n_tiles, pos0, rc):
    g = pl.program_id(0)
    gh = jnp.minimum(g, n_tiles - 1)
    l = lax.rem(gh, nl)
    head_live = g < n_tiles
    tail_live = g >= 1
    conv_in = conv_sh.at[0]

    @pl.when(g == 0)
    def _():
        cnt_run[...] = cntin_ref[...]
        h_prev[...] = jnp.zeros_like(h_prev)
        sy_prev[...] = jnp.zeros_like(sy_prev)
        pooled_prev[...] = jnp.zeros_like(pooled_prev)

    @pl.when(l == 0)
    def _():
        conv_in[0:CONV_HIST, :] = cinit_ref[0]
        pool_in[0:POOL_HIST, :] = pinit_ref[0]

    modh = modh_ref[0]
    sh1, sc1 = modh[0:1], modh[1:2]
    x = xh_ref[...]
    ms = jnp.mean(x * x, axis=-1, keepdims=True)
    hbuf[...] = ((x * lax.rsqrt(ms + EPS)) * (n1_ref[...] * (1.0 + sc1)) + sh1).astype(BF16)

    proj = jnp.dot(hbuf[...], win_ref[...], preferred_element_type=F32)
    glu = proj[:, D_POOL:D_POOL + D_CONV] * _sigmoid(proj[:, D_POOL + D_CONV:])
    pool_in[POOL_HIST:POOL_HIST + tl, :] = proj[:, :D_POOL]
    conv_in[CONV_HIST:CONV_HIST + tl, :] = glu

    modt = modt_ref[0]
    g1, sh2, sc2 = modt[2:3], modt[3:4], modt[4:5]
    for j in range(len(POOL_WINDOWS)):
        cs = slice(j * POOL_OUT_GW, (j + 1) * POOL_OUT_GW)
        cs2 = slice(D_MODEL + j * POOL_OUT_GW, D_MODEL + (j + 1) * POOL_OUT_GW)
        ga = _sigmoid(jnp.dot(h_prev[...], gw_ref[:, cs], preferred_element_type=F32) + gb_ref[:, cs])
        gb = _sigmoid(jnp.dot(h_prev[...], gw_ref[:, cs2], preferred_element_type=F32) + gb_ref[:, cs2])
        a_j = jnp.dot(pooled_prev[:, j * POOL_GW:(j + 1) * POOL_GW], pw_ref[j],
                      preferred_element_type=F32) * ps_ref[:, cs]
        b_j = jnp.dot(sy_prev[...], cwo_ref[:, cs], preferred_element_type=F32)
        merged[:, cs] = (ga * a_j + gb * b_j).astype(BF16)

    x1 = xt_ref[...] + g1 * jnp.dot(merged[...], wo_ref[...], preferred_element_type=F32)
    x1_ref[...] = x1
    ms2 = jnp.mean(x1 * x1, axis=-1, keepdims=True)
    h2 = (x1 * lax.rsqrt(ms2 + EPS)) * (n2_ref[...] * (1.0 + sc2)) + sh2
    h2b = h2.astype(BF16)
    words = _pack_words(h2)
    for j, ref in enumerate((hp0_ref, hp1_ref, hp2_ref, hp3_ref)):
        ref[...] = words[:, j * LANES:(j + 1) * LANES]

    logits = lax.dot_general(rwt_ref[...], h2b, (((1,), (1,)), ((), ())),
                             preferred_element_type=F32) + rb_ref[:, 0:1]
    e_iota = lax.broadcasted_iota(I32, (N_EXPERTS, tl), 0)
    v = logits
    ids, vals = [], []
    for _ in range(TOP_K):
        m = jnp.max(v, axis=0, keepdims=True)
        idx = jnp.min(jnp.where(v == m, e_iota, N_EXPERTS), axis=0, keepdims=True)
        ids.append(idx)
        vals.append(m)
        v = jnp.where(e_iota == idx, -jnp.inf, v)
    ex = [jnp.exp(vk - vals[0]) for vk in vals]
    den = ex[0] + ex[1] + ex[2] + ex[3]
    sel = [e_iota == idx for idx in ids]
    chosen = jnp.where((sel[0] | sel[1] | sel[2] | sel[3]) & tail_live, 1.0, 0.0)
    before = jnp.dot(chosen.astype(BF16), tri_ref[...], preferred_element_type=F32)
    rank_all = cnt_run[:, 0:1] + before
    ri = ri_ref if len(ri_ref.shape) == 2 else ri_ref.at[0]
    rw = rw_ref if len(rw_ref.shape) == 2 else rw_ref.at[0]
    for k in range(TOP_K):
        ri[k:k + 1, :] = ids[k]
        rk = jnp.sum(jnp.where(sel[k], rank_all, 0.0), axis=0, keepdims=True)
        ri[TOP_K + k:TOP_K + k + 1, :] = rk.astype(I32)
        rw[k:k + 1, :] = ex[k] / den
        rw[TOP_K + k:TOP_K + k + 1, :] = jnp.zeros((1, tl), F32)
    cnt_run[...] = cnt_run[...] + jnp.sum(chosen, axis=1, keepdims=True)
    cntout_ref[...] = cnt_run[...]

    pos = pos0 + l * tl + lax.broadcasted_iota(I32, (tl, POOL_GW), 0)
    pooled = []
    for gi, w in enumerate(POOL_WINDOWS):
        lanes = slice(gi * POOL_GW, (gi + 1) * POOL_GW)
        s = pool_in[:, lanes]
        d = 1
        while d < w:
            s = s + pltpu.roll(s, d, axis=0)
            d *= 2
        cur = pool_in[POOL_HIST:POOL_HIST + tl, lanes]
        cnt = jnp.minimum(pos + 1, w).astype(F32)
        pooled.append((s[POOL_HIST:] / cnt - cur).astype(BF16))

    n_sh = tl + CONV_HIST - SUBLANES
    for r in range(1, SUBLANES):
        conv_sh[r, 0:n_sh, :] = conv_in[r:r + n_sh, :]

    for base in range(0, tl, rc):
        acc = jnp.broadcast_to(cb_ref[...], (rc // SUBLANES, SUBLANES, D_CONV))
        for k in range(CONV_WIDTH):
            q, r = divmod(k + CONV_HIST - CONV_PAD, SUBLANES)
            start = base + q * SUBLANES
            win = conv_sh[r, start:start + rc, :].reshape(rc // SUBLANES, SUBLANES, D_CONV)
            acc = acc + win * dw_ref[k]
        ybuf[base:base + rc, :] = acc.reshape(rc, D_CONV)

    yc = ybuf[...]
    mu = jnp.mean(yc, axis=-1, keepdims=True)
    dev = yc - mu
    var = jnp.mean(dev * dev, axis=-1, keepdims=True)
    yn = dev * lax.rsqrt(var + EPS) * lg_ref[...] + lb_ref[...]

    h_prev[...] = hbuf[...]
    sy_prev[...] = (yn * _sigmoid(yn)).astype(BF16)
    for gi in range(len(POOL_WINDOWS)):
        pooled_prev[:, gi * POOL_GW:(gi + 1) * POOL_GW] = pooled[gi]

    @pl.when(head_live)
    def _():
        npool_ref[0] = pool_in[tl:tl + POOL_HIST, :]
        nconv_ref[0] = conv_in[tl:tl + CONV_HIST, :]

    conv_in[0:CONV_HIST, :] = conv_in[tl:tl + CONV_HIST, :]
    pool_in[0:POOL_HIST, :] = pool_in[tl:tl + POOL_HIST, :]


def _token_mix(x, mod, pool_init, conv_init, cnt_in, wts, *, b0, tl, pos0, rc):
    _, seq, _ = x.shape
    bsz = pool_init.shape[0]
    assert seq % tl == 0 and tl % rc == 0 and tl >= CONV_HIST
    nl = seq // tl
    n_tiles = bsz * nl
    rows_out = bsz * seq
    x2d = x.reshape(-1, D_MODEL)
    tri = jnp.triu(jnp.ones((tl, tl), BF16), k=1)
    head = lambda g: jnp.minimum(g, n_tiles - 1)
    tail = lambda g: jnp.maximum(g - 1, 0)
    full = lambda shape: pl.BlockSpec(shape, lambda g: (0,) * len(shape))
    per_b = lambda shape: pl.BlockSpec(shape, lambda g: (head(g) // nl,) + (0,) * (len(shape) - 1))
    tok_t = lambda width: pl.BlockSpec((tl, width), lambda g: (tail(g), 0))
    if tl % LANES == 0:
        rt_shape = (2 * TOP_K, n_tiles * tl)
        rt_spec = pl.BlockSpec((2 * TOP_K, tl), lambda g: (0, tail(g)))
    else:
        rt_shape = (n_tiles, 2 * TOP_K, tl)
        rt_spec = pl.BlockSpec((1, 2 * TOP_K, tl), lambda g: (tail(g), 0, 0))
    in_specs = [
        pl.BlockSpec((tl, D_MODEL), lambda g: (b0 * nl + head(g), 0)),
        pl.BlockSpec((tl, D_MODEL), lambda g: (b0 * nl + tail(g), 0)),
        pl.BlockSpec((1, 8, D_MODEL), lambda g: (b0 + head(g) // nl, 0, 0)),
        pl.BlockSpec((1, 8, D_MODEL), lambda g: (b0 + tail(g) // nl, 0, 0)),
        per_b((1, POOL_HIST, D_POOL)),
        per_b((1, CONV_HIST, D_CONV)),
        full((N_EXPERTS, LANES)),
        full((tl, tl)),
        full((1, D_MODEL)), full((1, D_MODEL)),
        full((D_MODEL, D_POOL + 2 * D_CONV)),
        full((len(POOL_WINDOWS), POOL_GW, POOL_OUT_GW)),
        full((1, D_MODEL)),
        full((CONV_WIDTH, SUBLANES, D_CONV)),
        full((1, D_CONV)), full((1, D_CONV)), full((1, D_CONV)),
        full((D_CONV, D_MODEL)),
        full((D_MODEL, 2 * D_MODEL)),
        full((1, 2 * D_MODEL)),
        full((D_MODEL, D_MODEL)),
        full((N_EXPERTS, D_MODEL)),
        full((N_EXPERTS, LANES)),
    ]
    out_shape = (
        jax.ShapeDtypeStruct((rows_out, D_MODEL), F32),
        *[jax.ShapeDtypeStruct((rows_out, LANES), I32)] * N_CHUNKS,
        jax.ShapeDtypeStruct(rt_shape, I32),
        jax.ShapeDtypeStruct(rt_shape, F32),
        jax.ShapeDtypeStruct((bsz, POOL_HIST, D_POOL), F32),
        jax.ShapeDtypeStruct((bsz, CONV_HIST, D_CONV), F32),
        jax.ShapeDtypeStruct((N_EXPERTS, LANES), F32),
    )
    out_specs = (
        tok_t(D_MODEL),
        *[tok_t(LANES)] * N_CHUNKS,
        rt_spec,
        rt_spec,
        per_b((1, POOL_HIST, D_POOL)),
        per_b((1, CONV_HIST, D_CONV)),
        full((N_EXPERTS, LANES)),
    )
    scratch = [
        pltpu.VMEM((SUBLANES, tl + CONV_HIST, D_CONV), F32),
        pltpu.VMEM((tl + POOL_HIST, D_POOL), F32),
        pltpu.VMEM((tl, D_MODEL), BF16),
        pltpu.VMEM((tl, D_CONV), F32),
        pltpu.VMEM((tl, D_MODEL), BF16),
        pltpu.VMEM((tl, D_CONV), BF16),
        pltpu.VMEM((tl, D_POOL), BF16),
        pltpu.VMEM((tl, D_MODEL), BF16),
        pltpu.VMEM((N_EXPERTS, LANES), F32),
    ]
    return pl.pallas_call(
        functools.partial(_mix_kernel, tl=tl, nl=nl, n_tiles=n_tiles, pos0=pos0, rc=rc),
        out_shape=out_shape,
        grid=(n_tiles + 1,),
        in_specs=in_specs,
        out_specs=out_specs,
        scratch_shapes=scratch,
        compiler_params=pltpu.CompilerParams(
            dimension_semantics=("arbitrary",), vmem_limit_bytes=VMEM_LIMIT_BYTES),
        name="token_mix",
    )(x2d, x2d, mod, mod, pool_init, conv_init, cnt_in, tri, *wts)


def _sc_mesh():
    return plsc.VectorSubcoreMesh(core_axis_name="c", subcore_axis_name="s",
                                  num_cores=SC_CORES, num_subcores=SC_SUBCORES)


def _for_each_chunk(n_tok, fn):
    wid = lax.axis_index("s") * SC_CORES + lax.axis_index("c")
    if n_tok % (8 * SC_WORKERS) == 0 and n_tok // SC_WORKERS >= SC_ROWS:
        per_w = n_tok // SC_WORKERS

        @pl.loop(0, pl.cdiv(per_w, SC_ROWS))
        def _(c):
            fn(pl.multiple_of(wid * per_w + jnp.minimum(c * SC_ROWS, per_w - SC_ROWS), 8))
    else:
        assert n_tok % SC_ROWS == 0 and n_tok // SC_ROWS <= SC_WORKERS

        @pl.when(wid < n_tok // SC_ROWS)
        def _():
            fn(pl.multiple_of(wid * SC_ROWS, SC_ROWS))


_SC_SCRATCH = ([pltpu.VMEM((SC_ROWS, LANES), I32)] * N_CHUNKS
               + [pltpu.VMEM((SC_ROWS,), I32)] * TOP_K
               + [pltpu.SemaphoreType.DMA, pltpu.SemaphoreType.DMA])


def _dispatch(h_groups, pos_groups, n_rows):
    n_g = len(h_groups)

    def body(*refs):
        refs = list(refs)
        h = [[refs.pop(0) for _ in range(N_CHUNKS)] for _ in range(n_g)]
        p = [[refs.pop(0) for _ in range(TOP_K)] for _ in range(n_g)]
        o = [refs.pop(0) for _ in range(N_CHUNKS)]
        rows = [refs.pop(0) for _ in range(N_CHUNKS)]
        idx = [refs.pop(0) for _ in range(TOP_K)]
        sem_in, sem_out = refs

        for hg, pg in zip(h, p):
            def move(s, hg=hg, pg=pg):
                loads = [pltpu.async_copy(hg[j].at[pl.ds(s, SC_ROWS)], rows[j], sem_in)
                         for j in range(N_CHUNKS)]
                loads += [pltpu.async_copy(pg[k].at[pl.ds(s, SC_ROWS)], idx[k], sem_in)
                          for k in range(TOP_K)]
                for cp in loads:
                    cp.wait()
                stores = [pltpu.async_copy(rows[j], o[j].at[idx[k]], sem_out)
                          for j in range(N_CHUNKS) for k in range(TOP_K)]
                for cp in stores:
                    cp.wait()

            _for_each_chunk(hg[0].shape[0], move)

    call = pl.kernel(
        body,
        out_type=tuple(jax.ShapeDtypeStruct((n_rows, LANES), I32) for _ in range(N_CHUNKS)),
        mesh=_sc_mesh(), scratch_types=_SC_SCRATCH, name="sc_dispatch")
    flat = [a for g in h_groups for a in g] + [a for g in pos_groups for a in g]
    return call(*flat)


def _combine_gather(y_chunks, pos_groups):
    n_g = len(pos_groups)

    def body(*refs):
        refs = list(refs)
        y = [refs.pop(0) for _ in range(N_CHUNKS)]
        p = [[refs.pop(0) for _ in range(TOP_K)] for _ in range(n_g)]
        g = [refs.pop(0) for _ in range(n_g)]
        rows = [refs.pop(0) for _ in range(N_CHUNKS)]
        idx = [refs.pop(0) for _ in range(TOP_K)]
        sem_in, sem_out = refs

        for pg, gg in zip(p, g):
            def move(s, pg=pg, gg=gg):
                loads = [pltpu.async_copy(pg[k].at[pl.ds(s, SC_ROWS)], idx[k], sem_in)
                         for k in range(TOP_K)]
                for cp in loads:
                    cp.wait()
                for k in range(TOP_K):
                    gathers = [pltpu.async_copy(y[j].at[idx[k]], rows[j], sem_in)
                               for j in range(N_CHUNKS)]
                    for cp in gathers:
                        cp.wait()
                    stores = [pltpu.async_copy(rows[j], gg.at[k * N_CHUNKS + j, pl.ds(s, SC_ROWS)], sem_out)
                              for j in range(N_CHUNKS)]
                    for cp in stores:
                        cp.wait()

            _for_each_chunk(pg[0].shape[0], move)

    call = pl.kernel(
        body,
        out_type=tuple(jax.ShapeDtypeStruct((TOP_K * N_CHUNKS, pg[0].shape[0], LANES), I32)
                       for pg in pos_groups),
        mesh=_sc_mesh(), scratch_types=_SC_SCRATCH, name="sc_combine_gather")
    return call(*y_chunks, *[a for g in pos_groups for a in g])


def _expert_kernel(te_ref, tr_ref, nx_ref, x0_ref, x1_ref, x2_ref, x3_ref,
                   wg_hbm, bg_ref, wu_hbm, bu_ref, wd_hbm, bd_ref,
                   y0_ref, y1_ref, y2_ref, y3_ref, stage, w_bf, sem):
    i = pl.program_id(0)

    def weight_copies(e):
        return [pltpu.make_async_copy(w.at[e], stage.at[m], sem.at[m])
                for m, w in enumerate((wg_hbm, wu_hbm, wd_hbm))]

    @pl.when(i == 0)
    def _():
        for cp in weight_copies(te_ref[0]):
            cp.start()

    @pl.when((i == 0) | (te_ref[i] != te_ref[jnp.maximum(i - 1, 0)]))
    def _():
        for cp in weight_copies(te_ref[i]):
            cp.wait()
        for m in range(3):
            w_bf[m] = stage[m].astype(BF16)

        @pl.when(nx_ref[i] >= 0)
        def _():
            for cp in weight_copies(nx_ref[i]):
                cp.start()

    @pl.when(tr_ref[i] > 0)
    def _():
        words = jnp.concatenate([x0_ref[...], x1_ref[...], x2_ref[...], x3_ref[...]], axis=1)
        hi, lo = _unpack_words(words)
        xt = jnp.concatenate([hi, lo], axis=1).astype(BF16)
        gt = jnp.minimum(jnp.dot(xt, w_bf[0], preferred_element_type=F32) + bg_ref[0], SWIGLU_LIMIT)
        up = jnp.clip(jnp.dot(xt, w_bf[1], preferred_element_type=F32) + bu_ref[0],
                      -SWIGLU_LIMIT, SWIGLU_LIMIT)
        act = gt * _sigmoid(SWIGLU_ALPHA * gt) * (up + 1.0)
        y = jnp.dot(act.astype(BF16), w_bf[2], preferred_element_type=F32) + bd_ref[0]
        out = _pack_words(y)
        for j, ref in enumerate((y0_ref, y1_ref, y2_ref, y3_ref)):
            ref[...] = out[:, j * LANES:(j + 1) * LANES]


def _expert_ffn(xs_chunks, tile_expert, tile_rows, next_expert, ew):
    n_rows = xs_chunks[0].shape[0]
    nt_max = n_rows // MOE_TILE
    wg, bg, wu, bu, wd, bd = ew
    rows = pl.BlockSpec((MOE_TILE, LANES), lambda i, te, tr, nx: (i, 0))
    mat = pl.BlockSpec(memory_space=pl.ANY)
    vec = pl.BlockSpec((1, 1, D_MODEL), lambda i, te, tr, nx: (te[i], 0, 0))
    grid_spec = pltpu.PrefetchScalarGridSpec(
        num_scalar_prefetch=3,
        grid=(nt_max,),
        in_specs=[rows] * N_CHUNKS + [mat, vec, mat, vec, mat, vec],
        out_specs=[rows] * N_CHUNKS,
        scratch_shapes=[pltpu.VMEM((3, D_MODEL, D_MODEL), F32),
                        pltpu.VMEM((3, D_MODEL, D_MODEL), BF16),
                        pltpu.SemaphoreType.DMA((3,))],
    )
    return pl.pallas_call(
        _expert_kernel,
        out_shape=[jax.ShapeDtypeStruct((n_rows, LANES), I32)] * N_CHUNKS,
        grid_spec=grid_spec,
        compiler_params=pltpu.CompilerParams(
            dimension_semantics=("arbitrary",), vmem_limit_bytes=VMEM_LIMIT_BYTES),
        name="expert_ffn",
    )(tile_expert, tile_rows, next_expert, *xs_chunks, wg, bg, wu, bu, wd, bd)


def _final_kernel(x1_ref, g_ref, w_ref, mod_ref, fg_ref, *rest):
    y_ref = rest[-1]
    wt = w_ref[...]
    acc = None
    for k in range(TOP_K):
        words = jnp.concatenate([g_ref[k * N_CHUNKS + j] for j in range(N_CHUNKS)], axis=1)
        hi, lo = _unpack_words(words)
        term = wt[:, k:k + 1] * jnp.concatenate([hi, lo], axis=1)
        acc = term if acc is None else acc + term
    g2 = mod_ref[0][5:6]
    x2 = x1_ref[...] + g2 * acc
    ms = jnp.mean(x2 * x2, axis=-1, keepdims=True)
    y_ref[...] = x2 * lax.rsqrt(ms + EPS) * fg_ref[...]


def _final(x1, gathered, wt_tok, mod, final_g, *, tl, tiles_per_batch, b0=0, out_rows=None, y_prev=None):
    n_tok = x1.shape[0]
    out_rows = n_tok if out_rows is None else out_rows
    blk0 = b0 * tiles_per_batch
    in_specs = [
        pl.BlockSpec((tl, D_MODEL), lambda i: (i, 0)),
        pl.BlockSpec((TOP_K * N_CHUNKS, tl, LANES), lambda i: (0, i, 0)),
        pl.BlockSpec((tl, 2 * TOP_K), lambda i: (i, 0)),
        pl.BlockSpec((1, 8, D_MODEL), lambda i: (b0 + i // tiles_per_batch, 0, 0)),
        pl.BlockSpec((1, D_MODEL), lambda i: (0, 0)),
    ]
    args = [x1, gathered, wt_tok, mod, final_g.reshape(1, D_MODEL)]
    aliases = {}
    if y_prev is not None:
        in_specs.append(pl.BlockSpec(memory_space=pl.ANY))
        args.append(y_prev)
        aliases = {len(args) - 1: 0}
    return pl.pallas_call(
        _final_kernel,
        out_shape=jax.ShapeDtypeStruct((out_rows, D_MODEL), F32),
        grid=(n_tok // tl,),
        in_specs=in_specs,
        out_specs=pl.BlockSpec((tl, D_MODEL), lambda i: (blk0 + i, 0)),
        input_output_aliases=aliases,
        compiler_params=pltpu.CompilerParams(
            dimension_semantics=("arbitrary",), vmem_limit_bytes=VMEM_LIMIT_BYTES),
        name="combine_final",
    )(*args)


def _route_rows(r, lo, hi):
    if r.ndim == 2:
        return r[lo:hi]
    return r[:, lo:hi, :].transpose(1, 0, 2).reshape(hi - lo, -1)


def kernel(x_prompt, x_sample, state_pool, state_conv, c_prompt, c_sample, norm1_g, norm2_g, final_g, w_ada, b_ada, w_in, pool_w, pool_scale, conv_dw, conv_b, conv_ln_g, conv_ln_b, conv_w_out, gate_w, gate_b, w_out, router_w, router_b, exp_w_gate, exp_b_gate, exp_w_up, exp_b_up, exp_w_down, exp_b_down):
    assert norm1_g.shape[0] == 1, "single-layer trunk"
    row = lambda v: v.reshape(1, -1)
    mix_w = (
        row(norm1_g[0]), row(norm2_g[0]),
        w_in[0].astype(BF16), pool_w[0].astype(BF16), row(pool_scale[0]),
        jnp.broadcast_to(conv_dw[0][:, None, :], (CONV_WIDTH, SUBLANES, D_CONV)),
        row(conv_b[0]), row(conv_ln_g[0]), row(conv_ln_b[0]),
        conv_w_out[0].astype(BF16), gate_w[0].astype(BF16), row(gate_b[0]),
        w_out[0].astype(BF16), router_w[0].T.astype(BF16),
        jnp.broadcast_to(router_b[0][:, None], (N_EXPERTS, LANES)),
    )
    ew = (
        exp_w_gate[0], exp_b_gate[0][:, None, :],
        exp_w_up[0], exp_b_up[0][:, None, :],
        exp_w_down[0], exp_b_down[0][:, None, :],
    )
    bp, lp, _ = x_prompt.shape
    bs, ls, _ = x_sample.shape
    t_p = bp * lp
    tl_p, tl_s = 512, ls
    nb_small = max(bp // PROMPT_TAIL_FRACTION, 1)
    chunk_nb = [nb_small, bp - nb_small] if bp > nb_small else [bp]
    chunk_b0 = [sum(chunk_nb[:c]) for c in range(len(chunk_nb))]
    n_chunks = len(chunk_nb)

    mod_p = _modulation(c_prompt, w_ada[0], b_ada[0])
    mod_s = _modulation(c_sample, w_ada[0], b_ada[0])
    pad_state = lambda s, hist: jnp.pad(s, ((0, 0), (hist - s.shape[1], 0), (0, 0)))
    split = lambda out: (out[0], out[1:1 + N_CHUNKS], *out[1 + N_CHUNKS:])

    zero_cnt = jnp.zeros((N_EXPERTS, LANES), F32)
    mixed = [split(_token_mix(
        x_prompt, mod_p, jnp.zeros((nb, POOL_HIST, D_POOL), F32), jnp.zeros((nb, CONV_HIST, D_CONV), F32),
        zero_cnt, mix_w, b0=b0, tl=tl_p, pos0=0, rc=32))
        for b0, nb in zip(chunk_b0, chunk_nb)]
    x1_s, hp_s, ri_s, rw_s, npool_s, nconv_s, cnt_last = split(_token_mix(
        x_sample, mod_s, pad_state(state_pool[0], POOL_HIST), pad_state(state_conv[0], CONV_HIST),
        mixed[-1][6], mix_w, b0=0, tl=tl_s, pos0=PAST_LEN, rc=32))

    def finish(c, g_c, y_prev):
        x1_c, rw_c = mixed[c][0], mixed[c][3]
        return _final(x1_c, g_c, _route_rows(rw_c, 0, 2 * TOP_K).T, mod_p, final_g, tl=tl_p,
                      tiles_per_batch=lp // tl_p, b0=chunk_b0[c], out_rows=t_p, y_prev=y_prev)

    gathered = []
    for c, (x1_c, hp_c, ri_c, rw_c, _, _, cnt_c) in enumerate(mixed):
        last = c == n_chunks - 1
        n_tok = chunk_nb[c] * lp + (bs * ls if last else 0)
        n_rows = (TOP_K * n_tok // MOE_TILE + N_EXPERTS) * MOE_TILE
        counts = (cnt_last if last else cnt_c)[:, 0].astype(I32)
        tiles_e = (counts + MOE_TILE - 1) // MOE_TILE
        tile_end = jnp.cumsum(tiles_e)
        tile_start = tile_end - tiles_e
        row_off = tile_start * MOE_TILE
        e_ids = jnp.arange(N_EXPERTS, dtype=I32)
        tile_id = jnp.arange(n_rows // MOE_TILE, dtype=I32)
        e_last = jnp.max(jnp.where(tiles_e > 0, e_ids, 0))
        tile_expert = jnp.minimum(
            jnp.sum(tile_id[:, None] >= tile_end[None, :], axis=1), e_last).astype(I32)
        is_e = tile_expert[:, None] == e_ids[None, :]
        left = jnp.sum(jnp.where(is_e, (counts - (tile_id[:, None] - tile_start[None, :]) * MOE_TILE), 0), axis=1)
        tile_rows = jnp.clip(left, 0, MOE_TILE).astype(I32)
        later = (e_ids[None, :] > e_ids[:, None]) & (tiles_e[None, :] > 0)
        next_of = jnp.min(jnp.where(later, e_ids[None, :], N_EXPERTS), axis=1)
        next_of = jnp.where(next_of == N_EXPERTS, -1, next_of)
        next_expert = jnp.sum(jnp.where(is_e, next_of[None, :], 0), axis=1).astype(I32)

        def slots(ri, row_off=row_off):
            ids, ranks = _route_rows(ri, 0, TOP_K), _route_rows(ri, TOP_K, 2 * TOP_K)
            pos = ranks
            for e in range(N_EXPERTS):
                pos = pos + jnp.where(ids == e, row_off[e], 0)
            return [pos[k] for k in range(TOP_K)]

        h_groups = [hp_c] + ([hp_s] if last else [])
        pos_groups = [slots(ri_c)] + ([slots(ri_s)] if last else [])
        xs = _dispatch(h_groups, pos_groups, n_rows)
        ys = _expert_ffn(xs, tile_expert, tile_rows, next_expert, ew)
        gathered.append(_combine_gather(ys, pos_groups))

    y_p = None
    for c in reversed(range(n_chunks)):
        y_p = finish(c, gathered[c][0], y_p)
    y_s = _final(x1_s, gathered[-1][1], _route_rows(rw_s, 0, 2 * TOP_K).T, mod_s, final_g, tl=tl_s,
                 tiles_per_batch=1)
    unpad = lambda s, n: s[:, s.shape[1] - n:][None]
    npool_p = jnp.concatenate([m[4] for m in mixed], axis=0)
    nconv_p = jnp.concatenate([m[5] for m in mixed], axis=0)
    return (y_p.reshape(bp, lp, D_MODEL), y_s.reshape(bs, ls, D_MODEL),
            unpad(npool_p, POOL_PAD), unpad(nconv_p, CONV_PAD),
            unpad(npool_s, POOL_PAD), unpad(nconv_s, CONV_PAD))
```

```python
import functools

import jax
import jax.numpy as jnp
from jax import lax
from jax.experimental import pallas as pl
from jax.experimental.pallas import tpu as pltpu
from jax.experimental.pallas import tpu_sc as plsc

D_MODEL = 1024
D_POOL = 512
D_CONV = 512
POOL_WINDOWS = (2, 4, 8, 16)
POOL_GW = 128
POOL_OUT_GW = 256
POOL_PAD = 15
CONV_WIDTH = 31
CONV_PAD = 30
N_EXPERTS = 32
TOP_K = 4
SWIGLU_LIMIT = 7.0
SWIGLU_ALPHA = 1.702
EPS = 1e-6
PAST_LEN = 2048

POOL_HIST = 16
CONV_HIST = 32
VMEM_LIMIT_BYTES = 56 * 1024 * 1024

LANES = 128
SUBLANES = 8
D_WORDS = D_MODEL // 2
N_CHUNKS = D_WORDS // LANES
MIX_TILE = 512
CONV_ROWS = 32
MOE_TILE = 512
PROMPT_TAIL_FRACTION = 4
SC_CORES = 2
SC_SUBCORES = 16
SC_WORKERS = SC_CORES * SC_SUBCORES
SC_ROWS = 128

F32 = jnp.float32
BF16 = jnp.bfloat16
I32 = jnp.int32
HI_MASK = -65536
NEG_LOG2E = -1.4426950408889634


def _sigmoid(v):
    return 1.0 / (1.0 + jnp.exp2(v * NEG_LOG2E))


def _pack_words(v):
    r = v.astype(BF16).astype(F32)
    hi = lax.bitcast_convert_type(r[:, :D_WORDS], I32)
    lo = lax.bitcast_convert_type(r[:, D_WORDS:], I32)
    return (hi & HI_MASK) | lax.shift_right_logical(lo, 16)


def _unpack_words(w):
    hi = lax.bitcast_convert_type(w & HI_MASK, F32)
    lo = lax.bitcast_convert_type(lax.shift_left(w, 16), F32)
    return hi, lo


def _mod_kernel(c_ref, w_ref, b_ref, o_ref):
    c = c_ref[...]
    s = (c * _sigmoid(c)).astype(BF16)
    o_ref[...] = jnp.dot(s, w_ref[...].astype(BF16), preferred_element_type=F32) + b_ref[...]


def _modulation(c, w_ada, b_ada):
    bsz = c.shape[0]
    out = pl.pallas_call(
        _mod_kernel,
        out_shape=jax.ShapeDtypeStruct((bsz, 6 * D_MODEL), F32),
        grid=(6,),
        in_specs=[
            pl.BlockSpec((bsz, D_MODEL), lambda j: (0, 0)),
            pl.BlockSpec((D_MODEL, D_MODEL), lambda j: (0, j)),
            pl.BlockSpec((1, D_MODEL), lambda j: (0, j)),
        ],
        out_specs=pl.BlockSpec((bsz, D_MODEL), lambda j: (0, j)),
        compiler_params=pltpu.CompilerParams(
            dimension_semantics=("arbitrary",), vmem_limit_bytes=VMEM_LIMIT_BYTES),
        name="adaln_mod",
    )(c, w_ada, b_ada.reshape(1, 6 * D_MODEL))
    out = out.reshape(bsz, 6, D_MODEL)
    return jnp.pad(out, ((0, 0), (0, 2), (0, 0)))


def _mix_kernel(xh_ref, xt_ref, modh_ref, modt_ref, pinit_ref, cinit_ref, cntin_ref, tri_ref,
                 n1_ref, n2_ref, win_ref, pw_ref, ps_ref, dw_ref, cb_ref, lg_ref, lb_ref,
                 cwo_ref, gw_ref, gb_ref, wo_ref, rwt_ref, rb_ref,
                 x1_ref, hp0_ref, hp1_ref, hp2_ref, hp3_ref, ri_ref, rw_ref,
                 npool_ref, nconv_ref, cntout_ref,
                 conv_sh, pool_in, hbuf, ybuf, h_prev, sy_prev, pooled_prev, merged, cnt_run,
                 *, tl, nl, n_tiles, pos0, rc):
    g = pl.program_id(0)
    gh = jnp.minimum(g, n_tiles - 1)
    l = lax.rem(gh, nl)
    head_live = g < n_tiles
    tail_live = g >= 1
    conv_in = conv_sh.at[0]

    @pl.when(g == 0)
    def _():
        cnt_run[...] = cntin_ref[...]
        h_prev[...] = jnp.zeros_like(h_prev)
        sy_prev[...] = jnp.zeros_like(sy_prev)
        pooled_prev[...] = jnp.zeros_like(pooled_prev)

    @pl.when(l == 0)
    def _():
        conv_in[0:CONV_HIST, :] = cinit_ref[0]
        pool_in[0:POOL_HIST, :] = pinit_ref[0]

    modh = modh_ref[0]
    sh1, sc1 = modh[0:1], modh[1:2]
    x = xh_ref[...]
    ms = jnp.mean(x * x, axis=-1, keepdims=True)
    hbuf[...] = ((x * lax.rsqrt(ms + EPS)) * (n1_ref[...] * (1.0 + sc1)) + sh1).astype(BF16)

    proj = jnp.dot(hbuf[...], win_ref[...], preferred_element_type=F32)
    glu = proj[:, D_POOL:D_POOL + D_CONV] * _sigmoid(proj[:, D_POOL + D_CONV:])
    pool_in[POOL_HIST:POOL_HIST + tl, :] = proj[:, :D_POOL]
    conv_in[CONV_HIST:CONV_HIST + tl, :] = glu

    modt = modt_ref[0]
    g1, sh2, sc2 = modt[2:3], modt[3:4], modt[4:5]
    for j in range(len(POOL_WINDOWS)):
        cs = slice(j * POOL_OUT_GW, (j + 1) * POOL_OUT_GW)
        cs2 = slice(D_MODEL + j * POOL_OUT_GW, D_MODEL + (j + 1) * POOL_OUT_GW)
        ga = _sigmoid(jnp.dot(h_prev[...], gw_ref[:, cs], preferred_element_type=F32) + gb_ref[:, cs])
        gb = _sigmoid(jnp.dot(h_prev[...], gw_ref[:, cs2], preferred_element_type=F32) + gb_ref[:, cs2])
        a_j = jnp.dot(pooled_prev[:, j * POOL_GW:(j + 1) * POOL_GW], pw_ref[j],
                      preferred_element_type=F32) * ps_ref[:, cs]
        b_j = jnp.dot(sy_prev[...], cwo_ref[:, cs], preferred_element_type=F32)
        merged[:, cs] = (ga * a_j + gb * b_j).astype(BF16)

    x1 = xt_ref[...] + g1 * jnp.dot(merged[...], wo_ref[...], preferred_element_type=F32)
    x1_ref[...] = x1
    ms2 = jnp.mean(x1 * x1, axis=-1, keepdims=True)
    h2 = (x1 * lax.rsqrt(ms2 + EPS)) * (n2_ref[...] * (1.0 + sc2)) + sh2
    h2b = h2.astype(BF16)
    words = _pack_words(h2)
    for j, ref in enumerate((hp0_ref, hp1_ref, hp2_ref, hp3_ref)):
        ref[...] = words[:, j * LANES:(j + 1) * LANES]

    logits = lax.dot_general(rwt_ref[...], h2b, (((1,), (1,)), ((), ())),
                             preferred_element_type=F32) + rb_ref[:, 0:1]
    e_iota = lax.broadcasted_iota(I32, (N_EXPERTS, tl), 0)
    v = logits
    ids, vals = [], []
    for _ in range(TOP_K):
        m = jnp.max(v, axis=0, keepdims=True)
        idx = jnp.min(jnp.where(v == m, e_iota, N_EXPERTS), axis=0, keepdims=True)
        ids.append(idx)
        vals.append(m)
        v = jnp.where(e_iota == idx, -jnp.inf, v)
    ex = [jnp.exp(vk - vals[0]) for vk in vals]
    den = ex[0] + ex[1] + ex[2] + ex[3]
    sel = [e_iota == idx for idx in ids]
    chosen = jnp.where((sel[0] | sel[1] | sel[2] | sel[3]) & tail_live, 1.0, 0.0)
    before = jnp.dot(chosen.astype(BF16), tri_ref[...], preferred_element_type=F32)
    rank_all = cnt_run[:, 0:1] + before
    ri = ri_ref if len(ri_ref.shape) == 2 else ri_ref.at[0]
    rw = rw_ref if len(rw_ref.shape) == 2 else rw_ref.at[0]
    for k in range(TOP_K):
        ri[k:k + 1, :] = ids[k]
        rk = jnp.sum(jnp.where(sel[k], rank_all, 0.0), axis=0, keepdims=True)
        ri[TOP_K + k:TOP_K + k + 1, :] = rk.astype(I32)
        rw[k:k + 1, :] = ex[k] / den
        rw[TOP_K + k:TOP_K + k + 1, :] = jnp.zeros((1, tl), F32)
    cnt_run[...] = cnt_run[...] + jnp.sum(chosen, axis=1, keepdims=True)
    cntout_ref[...] = cnt_run[...]

    pos = pos0 + l * tl + lax.broadcasted_iota(I32, (tl, POOL_GW), 0)
    pooled = []
    for gi, w in enumerate(POOL_WINDOWS):
        lanes = slice(gi * POOL_GW, (gi + 1) * POOL_GW)
        s = pool_in[:, lanes]
        d = 1
        while d < w:
            s = s + pltpu.roll(s, d, axis=0)
            d *= 2
        cur = pool_in[POOL_HIST:POOL_HIST + tl, lanes]
        cnt = jnp.minimum(pos + 1, w).astype(F32)
        pooled.append((s[POOL_HIST:] / cnt - cur).astype(BF16))

    n_sh = tl + CONV_HIST - SUBLANES
    for r in range(1, SUBLANES):
        conv_sh[r, 0:n_sh, :] = conv_in[r:r + n_sh, :]

    for base in range(0, tl, rc):
        acc = jnp.broadcast_to(cb_ref[...], (rc // SUBLANES, SUBLANES, D_CONV))
        for k in range(CONV_WIDTH):
            q, r = divmod(k + CONV_HIST - CONV_PAD, SUBLANES)
            start = base + q * SUBLANES
            win = conv_sh[r, start:start + rc, :].reshape(rc // SUBLANES, SUBLANES, D_CONV)
            acc = acc + win * dw_ref[k]
        ybuf[base:base + rc, :] = acc.reshape(rc, D_CONV)

    yc = ybuf[...]
    mu = jnp.mean(yc, axis=-1, keepdims=True)
    dev = yc - mu
    var = jnp.mean(dev * dev, axis=-1, keepdims=True)
    yn = dev * lax.rsqrt(var + EPS) * lg_ref[...] + lb_ref[...]

    h_prev[...] = hbuf[...]
    sy_prev[...] = (yn * _sigmoid(yn)).astype(BF16)
    for gi in range(len(POOL_WINDOWS)):
        pooled_prev[:, gi * POOL_GW:(gi + 1) * POOL_GW] = pooled[gi]

    @pl.when(head_live)
    def _():
        npool_ref[0] = pool_in[tl:tl + POOL_HIST, :]
        nconv_ref[0] = conv_in[tl:tl + CONV_HIST, :]

    conv_in[0:CONV_HIST, :] = conv_in[tl:tl + CONV_HIST, :]
    pool_in[0:POOL_HIST, :] = pool_in[tl:tl + POOL_HIST, :]


def _token_mix(x, mod, pool_init, conv_init, cnt_in, wts, *, b0, tl, pos0, rc):
    _, seq, _ = x.shape
    bsz = pool_init.shape[0]
    assert seq % tl == 0 and tl % rc == 0 and tl >= CONV_HIST
    nl = seq // tl
    n_tiles = bsz * nl
    rows_out = bsz * seq
    x2d = x.reshape(-1, D_MODEL)
    tri = jnp.triu(jnp.ones((tl, tl), BF16), k=1)
    head = lambda g: jnp.minimum(g, n_tiles - 1)
    tail = lambda g: jnp.maximum(g - 1, 0)
    full = lambda shape: pl.BlockSpec(shape, lambda g: (0,) * len(shape))
    per_b = lambda shape: pl.BlockSpec(shape, lambda g: (head(g) // nl,) + (0,) * (len(shape) - 1))
    tok_t = lambda width: pl.BlockSpec((tl, width), lambda g: (tail(g), 0))
    if tl % LANES == 0:
        rt_shape = (2 * TOP_K, n_tiles * tl)
        rt_spec = pl.BlockSpec((2 * TOP_K, tl), lambda g: (0, tail(g)))
    else:
        rt_shape = (n_tiles, 2 * TOP_K, tl)
        rt_spec = pl.BlockSpec((1, 2 * TOP_K, tl), lambda g: (tail(g), 0, 0))
    in_specs = [
        pl.BlockSpec((tl, D_MODEL), lambda g: (b0 * nl + head(g), 0)),
        pl.BlockSpec((tl, D_MODEL), lambda g: (b0 * nl + tail(g), 0)),
        pl.BlockSpec((1, 8, D_MODEL), lambda g: (b0 + head(g) // nl, 0, 0)),
        pl.BlockSpec((1, 8, D_MODEL), lambda g: (b0 + tail(g) // nl, 0, 0)),
        per_b((1, POOL_HIST, D_POOL)),
        per_b((1, CONV_HIST, D_CONV)),
        full((N_EXPERTS, LANES)),
        full((tl, tl)),
        full((1, D_MODEL)), full((1, D_MODEL)),
        full((D_MODEL, D_POOL + 2 * D_CONV)),
        full((len(POOL_WINDOWS), POOL_GW, POOL_OUT_GW)),
        full((1, D_MODEL)),
        full((CONV_WIDTH, SUBLANES, D_CONV)),
        full((1, D_CONV)), full((1, D_CONV)), full((1, D_CONV)),
        full((D_CONV, D_MODEL)),
        full((D_MODEL, 2 * D_MODEL)),
        full((1, 2 * D_MODEL)),
        full((D_MODEL, D_MODEL)),
        full((N_EXPERTS, D_MODEL)),
        full((N_EXPERTS, LANES)),
    ]
    out_shape = (
        jax.ShapeDtypeStruct((rows_out, D_MODEL), F32),
        *[jax.ShapeDtypeStruct((rows_out, LANES), I32)] * N_CHUNKS,
        jax.ShapeDtypeStruct(rt_shape, I32),
        jax.ShapeDtypeStruct(rt_shape, F32),
        jax.ShapeDtypeStruct((bsz, POOL_HIST, D_POOL), F32),
        jax.ShapeDtypeStruct((bsz, CONV_HIST, D_CONV), F32),
        jax.ShapeDtypeStruct((N_EXPERTS, LANES), F32),
    )
    out_specs = (
        tok_t(D_MODEL),
        *[tok_t(LANES)] * N_CHUNKS,
        rt_spec,
        rt_spec,
        per_b((1, POOL_HIST, D_POOL)),
        per_b((1, CONV_HIST, D_CONV)),
        full((N_EXPERTS, LANES)),
    )
    scratch = [
        pltpu.VMEM((SUBLANES, tl + CONV_HIST, D_CONV), F32),
        pltpu.VMEM((tl + POOL_HIST, D_POOL), F32),
        pltpu.VMEM((tl, D_MODEL), BF16),
        pltpu.VMEM((tl, D_CONV), F32),
        pltpu.VMEM((tl, D_MODEL), BF16),
        pltpu.VMEM((tl, D_CONV), BF16),
        pltpu.VMEM((tl, D_POOL), BF16),
        pltpu.VMEM((tl, D_MODEL), BF16),
        pltpu.VMEM((N_EXPERTS, LANES), F32),
    ]
    return pl.pallas_call(
        functools.partial(_mix_kernel, tl=tl, nl=nl, n_tiles=n_tiles, pos0=pos0, rc=rc),
        out_shape=out_shape,
        grid=(n_tiles + 1,),
        in_specs=in_specs,
        out_specs=out_specs,
        scratch_shapes=scratch,
        compiler_params=pltpu.CompilerParams(
            dimension_semantics=("arbitrary",), vmem_limit_bytes=VMEM_LIMIT_BYTES),
        name="token_mix",
    )(x2d, x2d, mod, mod, pool_init, conv_init, cnt_in, tri, *wts)


def _sc_mesh():
    return plsc.VectorSubcoreMesh(core_axis_name="c", subcore_axis_name="s",
                                  num_cores=SC_CORES, num_subcores=SC_SUBCORES)


def _for_each_chunk(n_tok, fn):
    wid = lax.axis_index("s") * SC_CORES + lax.axis_index("c")
    if n_tok % (8 * SC_WORKERS) == 0 and n_tok // SC_WORKERS >= SC_ROWS:
        per_w = n_tok // SC_WORKERS

        @pl.loop(0, pl.cdiv(per_w, SC_ROWS))
        def _(c):
            fn(pl.multiple_of(wid * per_w + jnp.minimum(c * SC_ROWS, per_w - SC_ROWS), 8))
    else:
        assert n_tok % SC_ROWS == 0 and n_tok // SC_ROWS <= SC_WORKERS

        @pl.when(wid < n_tok // SC_ROWS)
        def _():
            fn(pl.multiple_of(wid * SC_ROWS, SC_ROWS))


_SC_SCRATCH = ([pltpu.VMEM((SC_ROWS, LANES), I32)] * N_CHUNKS
               + [pltpu.VMEM((SC_ROWS,), I32)] * TOP_K
               + [pltpu.SemaphoreType.DMA, pltpu.SemaphoreType.DMA])


def _dispatch(h_groups, pos_groups, n_rows):
    n_g = len(h_groups)

    def body(*refs):
        refs = list(refs)
        h = [[refs.pop(0) for _ in range(N_CHUNKS)] for _ in range(n_g)]
        p = [[refs.pop(0) for _ in range(TOP_K)] for _ in range(n_g)]
        o = [refs.pop(0) for _ in range(N_CHUNKS)]
        rows = [refs.pop(0) for _ in range(N_CHUNKS)]
        idx = [refs.pop(0) for _ in range(TOP_K)]
        sem_in, sem_out = refs

        for hg, pg in zip(h, p):
            def move(s, hg=hg, pg=pg):
                loads = [pltpu.async_copy(hg[j].at[pl.ds(s, SC_ROWS)], rows[j], sem_in)
                         for j in range(N_CHUNKS)]
                loads += [pltpu.async_copy(pg[k].at[pl.ds(s, SC_ROWS)], idx[k], sem_in)
                          for k in range(TOP_K)]
                for cp in loads:
                    cp.wait()
                stores = [pltpu.async_copy(rows[j], o[j].at[idx[k]], sem_out)
                          for j in range(N_CHUNKS) for k in range(TOP_K)]
                for cp in stores:
                    cp.wait()

            _for_each_chunk(hg[0].shape[0], move)

    call = pl.kernel(
        body,
        out_type=tuple(jax.ShapeDtypeStruct((n_rows, LANES), I32) for _ in range(N_CHUNKS)),
        mesh=_sc_mesh(), scratch_types=_SC_SCRATCH, name="sc_dispatch")
    flat = [a for g in h_groups for a in g] + [a for g in pos_groups for a in g]
    return call(*flat)


def _combine_gather(y_chunks, pos_groups):
    n_g = len(pos_groups)

    def body(*refs):
        refs = list(refs)
        y = [refs.pop(0) for _ in range(N_CHUNKS)]
        p = [[refs.pop(0) for _ in range(TOP_K)] for _ in range(n_g)]
        g = [refs.pop(0) for _ in range(n_g)]
        rows = [refs.pop(0) for _ in range(N_CHUNKS)]
        idx = [refs.pop(0) for _ in range(TOP_K)]
        sem_in, sem_out = refs

        for pg, gg in zip(p, g):
            def move(s, pg=pg, gg=gg):
                loads = [pltpu.async_copy(pg[k].at[pl.ds(s, SC_ROWS)], idx[k], sem_in)
                         for k in range(TOP_K)]
                for cp in loads:
                    cp.wait()
                for k in range(TOP_K):
                    gathers = [pltpu.async_copy(y[j].at[idx[k]], rows[j], sem_in)
                               for j in range(N_CHUNKS)]
                    for cp in gathers:
                        cp.wait()
                    stores = [pltpu.async_copy(rows[j], gg.at[k * N_CHUNKS + j, pl.ds(s, SC_ROWS)], sem_out)
                              for j in range(N_CHUNKS)]
                    for cp in stores:
                        cp.wait()

            _for_each_chunk(pg[0].shape[0], move)

    call = pl.kernel(
        body,
        out_type=tuple(jax.ShapeDtypeStruct((TOP_K * N_CHUNKS, pg[0].shape[0], LANES), I32)
                       for pg in pos_groups),
        mesh=_sc_mesh(), scratch_types=_SC_SCRATCH, name="sc_combine_gather")
    return call(*y_chunks, *[a for g in pos_groups for a in g])


def _expert_kernel(te_ref, tr_ref, nx_ref, x0_ref, x1_ref, x2_ref, x3_ref,
                   wg_hbm, bg_ref, wu_hbm, bu_ref, wd_hbm, bd_ref,
                   y0_ref, y1_ref, y2_ref, y3_ref, stage, w_bf, sem):
    i = pl.program_id(0)

    def weight_copies(e):
        return [pltpu.make_async_copy(w.at[e], stage.at[m], sem.at[m])
                for m, w in enumerate((wg_hbm, wu_hbm, wd_hbm))]

    @pl.when(i == 0)
    def _():
        for cp in weight_copies(te_ref[0]):
            cp.start()

    @pl.when((i == 0) | (te_ref[i] != te_ref[jnp.maximum(i - 1, 0)]))
    def _():
        for cp in weight_copies(te_ref[i]):
            cp.wait()
        for m in range(3):
            w_bf[m] = stage[m].astype(BF16)

        @pl.when(nx_ref[i] >= 0)
        def _():
            for cp in weight_copies(nx_ref[i]):
                cp.start()

    @pl.when(tr_ref[i] > 0)
    def _():
        words = jnp.concatenate([x0_ref[...], x1_ref[...], x2_ref[...], x3_ref[...]], axis=1)
        hi, lo = _unpack_words(words)
        xt = jnp.concatenate([hi, lo], axis=1).astype(BF16)
        gt = jnp.minimum(jnp.dot(xt, w_bf[0], preferred_element_type=F32) + bg_ref[0], SWIGLU_LIMIT)
        up = jnp.clip(jnp.dot(xt, w_bf[1], preferred_element_type=F32) + bu_ref[0],
                      -SWIGLU_LIMIT, SWIGLU_LIMIT)
        act = gt * _sigmoid(SWIGLU_ALPHA * gt) * (up + 1.0)
        y = jnp.dot(act.astype(BF16), w_bf[2], preferred_element_type=F32) + bd_ref[0]
        out = _pack_words(y)
        for j, ref in enumerate((y0_ref, y1_ref, y2_ref, y3_ref)):
            ref[...] = out[:, j * LANES:(j + 1) * LANES]


def _expert_ffn(xs_chunks, tile_expert, tile_rows, next_expert, ew):
    n_rows = xs_chunks[0].shape[0]
    nt_max = n_rows // MOE_TILE
    wg, bg, wu, bu, wd, bd = ew
    rows = pl.BlockSpec((MOE_TILE, LANES), lambda i, te, tr, nx: (i, 0))
    mat = pl.BlockSpec(memory_space=pl.ANY)
    vec = pl.BlockSpec((1, 1, D_MODEL), lambda i, te, tr, nx: (te[i], 0, 0))
    grid_spec = pltpu.PrefetchScalarGridSpec(
        num_scalar_prefetch=3,
        grid=(nt_max,),
        in_specs=[rows] * N_CHUNKS + [mat, vec, mat, vec, mat, vec],
        out_specs=[rows] * N_CHUNKS,
        scratch_shapes=[pltpu.VMEM((3, D_MODEL, D_MODEL), F32),
                        pltpu.VMEM((3, D_MODEL, D_MODEL), BF16),
                        pltpu.SemaphoreType.DMA((3,))],
    )
    return pl.pallas_call(
        _expert_kernel,
        out_shape=[jax.ShapeDtypeStruct((n_rows, LANES), I32)] * N_CHUNKS,
        grid_spec=grid_spec,
        compiler_params=pltpu.CompilerParams(
            dimension_semantics=("arbitrary",), vmem_limit_bytes=VMEM_LIMIT_BYTES),
        name="expert_ffn",
    )(tile_expert, tile_rows, next_expert, *xs_chunks, wg, bg, wu, bu, wd, bd)


def _final_kernel(x1_ref, g_ref, w_ref, mod_ref, fg_ref, *rest):
    y_ref = rest[-1]
    wt = w_ref[...]
    acc = None
    for k in range(TOP_K):
        words = jnp.concatenate([g_ref[k * N_CHUNKS + j] for j in range(N_CHUNKS)], axis=1)
        hi, lo = _unpack_words(words)
        term = wt[:, k:k + 1] * jnp.concatenate([hi, lo], axis=1)
        acc = term if acc is None else acc + term
    g2 = mod_ref[0][5:6]
    x2 = x1_ref[...] + g2 * acc
    ms = jnp.mean(x2 * x2, axis=-1, keepdims=True)
    y_ref[...] = x2 * lax.rsqrt(ms + EPS) * fg_ref[...]


def _final(x1, gathered, wt_tok, mod, final_g, *, tl, tiles_per_batch, b0=0, out_rows=None, y_prev=None):
    n_tok = x1.shape[0]
    out_rows = n_tok if out_rows is None else out_rows
    blk0 = b0 * tiles_per_batch
    in_specs = [
        pl.BlockSpec((tl, D_MODEL), lambda i: (i, 0)),
        pl.BlockSpec((TOP_K * N_CHUNKS, tl, LANES), lambda i: (0, i, 0)),
        pl.BlockSpec((tl, 2 * TOP_K), lambda i: (i, 0)),
        pl.BlockSpec((1, 8, D_MODEL), lambda i: (b0 + i // tiles_per_batch, 0, 0)),
        pl.BlockSpec((1, D_MODEL), lambda i: (0, 0)),
    ]
    args = [x1, gathered, wt_tok, mod, final_g.reshape(1, D_MODEL)]
    aliases = {}
    if y_prev is not None:
        in_specs.append(pl.BlockSpec(memory_space=pl.ANY))
        args.append(y_prev)
        aliases = {len(args) - 1: 0}
    return pl.pallas_call(
        _final_kernel,
        out_shape=jax.ShapeDtypeStruct((out_rows, D_MODEL), F32),
        grid=(n_tok // tl,),
        in_specs=in_specs,
        out_specs=pl.BlockSpec((tl, D_MODEL), lambda i: (blk0 + i, 0)),
        input_output_aliases=aliases,
        compiler_params=pltpu.CompilerParams(
            dimension_semantics=("arbitrary",), vmem_limit_bytes=VMEM_LIMIT_BYTES),
        name="combine_final",
    )(*args)


def _route_rows(r, lo, hi):
    if r.ndim == 2:
        return r[lo:hi]
    return r[:, lo:hi, :].transpose(1, 0, 2).reshape(hi - lo, -1)


def kernel(x_prompt, x_sample, state_pool, state_conv, c_prompt, c_sample, norm1_g, norm2_g, final_g, w_ada, b_ada, w_in, pool_w, pool_scale, conv_dw, conv_b, conv_ln_g, conv_ln_b, conv_w_out, gate_w, gate_b, w_out, router_w, router_b, exp_w_gate, exp_b_gate, exp_w_up, exp_b_up, exp_w_down, exp_b_down):
    assert norm1_g.shape[0] == 1, "single-layer trunk"
    row = lambda v: v.reshape(1, -1)
    mix_w = (
        row(norm1_g[0]), row(norm2_g[0]),
        w_in[0].astype(BF16), pool_w[0].astype(BF16), row(pool_scale[0]),
        jnp.broadcast_to(conv_dw[0][:, None, :], (CONV_WIDTH, SUBLANES, D_CONV)),
        row(conv_b[0]), row(conv_ln_g[0]), row(conv_ln_b[0]),
        conv_w_out[0].astype(BF16), gate_w[0].astype(BF16), row(gate_b[0]),
        w_out[0].astype(BF16), router_w[0].T.astype(BF16),
        jnp.broadcast_to(router_b[0][:, None], (N_EXPERTS, LANES)),
    )
    ew = (
        exp_w_gate[0], exp_b_gate[0][:, None, :],
        exp_w_up[0], exp_b_up[0][:, None, :],
        exp_w_down[0], exp_b_down[0][:, None, :],
    )
    bp, lp, _ = x_prompt.shape
    bs, ls, _ = x_sample.shape
    t_p = bp * lp
    tl_p, tl_s = MIX_TILE, ls
    nb_small = max(bp // PROMPT_TAIL_FRACTION, 1)
    chunk_nb = [nb_small, bp - nb_small] if bp > nb_small else [bp]
    chunk_b0 = [sum(chunk_nb[:c]) for c in range(len(chunk_nb))]
    n_chunks = len(chunk_nb)

    mod_p = _modulation(c_prompt, w_ada[0], b_ada[0])
    mod_s = _modulation(c_sample, w_ada[0], b_ada[0])
    pad_state = lambda s, hist: jnp.pad(s, ((0, 0), (hist - s.shape[1], 0), (0, 0)))
    split = lambda out: (out[0], out[1:1 + N_CHUNKS], *out[1 + N_CHUNKS:])

    zero_cnt = jnp.zeros((N_EXPERTS, LANES), F32)
    mixed = [split(_token_mix(
        x_prompt, mod_p, jnp.zeros((nb, POOL_HIST, D_POOL), F32), jnp.zeros((nb, CONV_HIST, D_CONV), F32),
        zero_cnt, mix_w, b0=b0, tl=tl_p, pos0=0, rc=CONV_ROWS))
        for b0, nb in zip(chunk_b0, chunk_nb)]
    x1_s, hp_s, ri_s, rw_s, npool_s, nconv_s, cnt_last = split(_token_mix(
        x_sample, mod_s, pad_state(state_pool[0], POOL_HIST), pad_state(state_conv[0], CONV_HIST),
        mixed[-1][6], mix_w, b0=0, tl=tl_s, pos0=PAST_LEN, rc=CONV_ROWS))

    def finish(c, g_c, y_prev):
        x1_c, rw_c = mixed[c][0], mixed[c][3]
        return _final(x1_c, g_c, _route_rows(rw_c, 0, 2 * TOP_K).T, mod_p, final_g, tl=tl_p,
                      tiles_per_batch=lp // tl_p, b0=chunk_b0[c], out_rows=t_p, y_prev=y_prev)

    gathered = []
    for c, (x1_c, hp_c, ri_c, rw_c, _, _, cnt_c) in enumerate(mixed):
        last = c == n_chunks - 1
        n_tok = chunk_nb[c] * lp + (bs * ls if last else 0)
        n_rows = (TOP_K * n_tok // MOE_TILE + N_EXPERTS) * MOE_TILE
        counts = (cnt_last if last else cnt_c)[:, 0].astype(I32)
        tiles_e = (counts + MOE_TILE - 1) // MOE_TILE
        tile_end = jnp.cumsum(tiles_e)
        tile_start = tile_end - tiles_e
        row_off = tile_start * MOE_TILE
        e_ids = jnp.arange(N_EXPERTS, dtype=I32)
        tile_id = jnp.arange(n_rows // MOE_TILE, dtype=I32)
        e_last = jnp.max(jnp.where(tiles_e > 0, e_ids, 0))
        tile_expert = jnp.minimum(
            jnp.sum(tile_id[:, None] >= tile_end[None, :], axis=1), e_last).astype(I32)
        is_e = tile_expert[:, None] == e_ids[None, :]
        left = jnp.sum(jnp.where(is_e, (counts - (tile_id[:, None] - tile_start[None, :]) * MOE_TILE), 0), axis=1)
        tile_rows = jnp.clip(left, 0, MOE_TILE).astype(I32)
        later = (e_ids[None, :] > e_ids[:, None]) & (tiles_e[None, :] > 0)
        next_of = jnp.min(jnp.where(later, e_ids[None, :], N_EXPERTS), axis=1)
        next_of = jnp.where(next_of == N_EXPERTS, -1, next_of)
        next_expert = jnp.sum(jnp.where(is_e, next_of[None, :], 0), axis=1).astype(I32)

        def slots(ri, row_off=row_off):
            ids, ranks = _route_rows(ri, 0, TOP_K), _route_rows(ri, TOP_K, 2 * TOP_K)
            table = [row_off[e] for e in range(N_EXPERTS)]
            bit = 1
            while len(table) > 1:
                odd = (ids & bit) != 0
                table = [jnp.where(odd, table[2 * i + 1], table[2 * i]) for i in range(len(table) // 2)]
                bit *= 2
            pos = ranks + table[0]
            return [pos[k] for k in range(TOP_K)]

        h_groups = [hp_c] + ([hp_s] if last else [])
        pos_groups = [slots(ri_c)] + ([slots(ri_s)] if last else [])
        xs = _dispatch(h_groups, pos_groups, n_rows)
        ys = _expert_ffn(xs, tile_expert, tile_rows, next_expert, ew)
        gathered.append(_combine_gather(ys, pos_groups))

    y_p = None
    for c in reversed(range(n_chunks)):
        y_p = finish(c, gathered[c][0], y_p)
    y_s = _final(x1_s, gathered[-1][1], _route_rows(rw_s, 0, 2 * TOP_K).T, mod_s, final_g, tl=tl_s,
                 tiles_per_batch=1)
    unpad = lambda s, n: s[:, s.shape[1] - n:][None]
    npool_p = jnp.concatenate([m[4] for m in mixed], axis=0)
    nconv_p = jnp.concatenate([m[5] for m in mixed], axis=0)
    return (y_p.reshape(bp, lp, D_MODEL), y_s.reshape(bs, ls, D_MODEL),
            unpad(npool_p, POOL_PAD), unpad(nconv_p, CONV_PAD),
            unpad(npool_s, POOL_PAD), unpad(nconv_s, CONV_PAD))
```

```python
import functools

import jax
import jax.numpy as jnp
from jax import lax
from jax.experimental import pallas as pl
from jax.experimental.pallas import tpu as pltpu
from jax.experimental.pallas import tpu_sc as plsc

D_MODEL = 1024
D_POOL = 512
D_CONV = 512
POOL_WINDOWS = (2, 4, 8, 16)
POOL_GW = 128
POOL_OUT_GW = 256
POOL_PAD = 15
CONV_WIDTH = 31
CONV_PAD = 30
N_EXPERTS = 32
TOP_K = 4
SWIGLU_LIMIT = 7.0
SWIGLU_ALPHA = 1.702
EPS = 1e-6
PAST_LEN = 2048

POOL_HIST = 16
CONV_HIST = 32
VMEM_LIMIT_BYTES = 56 * 1024 * 1024

LANES = 128
SUBLANES = 8
D_WORDS = D_MODEL // 2
N_CHUNKS = D_WORDS // LANES
MIX_TILE = 512
CONV_ROWS = 32
MOE_TILE = 512
PROMPT_TAIL_FRACTION = 5
SC_CORES = 2
SC_SUBCORES = 16
SC_WORKERS = SC_CORES * SC_SUBCORES
SC_ROWS = 128

F32 = jnp.float32
BF16 = jnp.bfloat16
I32 = jnp.int32
HI_MASK = -65536
NEG_LOG2E = -1.4426950408889634


def _sigmoid(v):
    return 1.0 / (1.0 + jnp.exp2(v * NEG_LOG2E))


def _pack_words(v):
    r = v.astype(BF16).astype(F32)
    hi = lax.bitcast_convert_type(r[:, :D_WORDS], I32)
    lo = lax.bitcast_convert_type(r[:, D_WORDS:], I32)
    return (hi & HI_MASK) | lax.shift_right_logical(lo, 16)


def _unpack_words(w):
    hi = lax.bitcast_convert_type(w & HI_MASK, F32)
    lo = lax.bitcast_convert_type(lax.shift_left(w, 16), F32)
    return hi, lo


def _mod_kernel(c_ref, w_ref, b_ref, o_ref):
    c = c_ref[...]
    s = (c * _sigmoid(c)).astype(BF16)
    o_ref[...] = jnp.dot(s, w_ref[...].astype(BF16), preferred_element_type=F32) + b_ref[...]


def _modulation(c, w_ada, b_ada):
    bsz = c.shape[0]
    out = pl.pallas_call(
        _mod_kernel,
        out_shape=jax.ShapeDtypeStruct((bsz, 6 * D_MODEL), F32),
        grid=(6,),
        in_specs=[
            pl.BlockSpec((bsz, D_MODEL), lambda j: (0, 0)),
            pl.BlockSpec((D_MODEL, D_MODEL), lambda j: (0, j)),
            pl.BlockSpec((1, D_MODEL), lambda j: (0, j)),
        ],
        out_specs=pl.BlockSpec((bsz, D_MODEL), lambda j: (0, j)),
        compiler_params=pltpu.CompilerParams(
            dimension_semantics=("arbitrary",), vmem_limit_bytes=VMEM_LIMIT_BYTES),
        name="adaln_mod",
    )(c, w_ada, b_ada.reshape(1, 6 * D_MODEL))
    out = out.reshape(bsz, 6, D_MODEL)
    return jnp.pad(out, ((0, 0), (0, 2), (0, 0)))


def _mix_kernel(xh_ref, xt_ref, modh_ref, modt_ref, pinit_ref, cinit_ref, cntin_ref, tri_ref,
                 n1_ref, n2_ref, win_ref, pw_ref, ps_ref, dw_ref, cb_ref, lg_ref, lb_ref,
                 cwo_ref, gw_ref, gb_ref, wo_ref, rwt_ref, rb_ref,
                 x1_ref, hp0_ref, hp1_ref, hp2_ref, hp3_ref, ri_ref, rw_ref,
                 npool_ref, nconv_ref, cntout_ref,
                 conv_sh, pool_in, hbuf, ybuf, h_prev, sy_prev, pooled_prev, merged, cnt_run,
                 *, tl, nl, n_tiles, pos0, rc):
    g = pl.program_id(0)
    gh = jnp.minimum(g, n_tiles - 1)
    l = lax.rem(gh, nl)
    head_live = g < n_tiles
    tail_live = g >= 1
    conv_in = conv_sh.at[0]

    @pl.when(g == 0)
    def _():
        cnt_run[...] = cntin_ref[...]
        h_prev[...] = jnp.zeros_like(h_prev)
        sy_prev[...] = jnp.zeros_like(sy_prev)
        pooled_prev[...] = jnp.zeros_like(pooled_prev)

    @pl.when(l == 0)
    def _():
        conv_in[0:CONV_HIST, :] = cinit_ref[0]
        pool_in[0:POOL_HIST, :] = pinit_ref[0]

    modh = modh_ref[0]
    sh1, sc1 = modh[0:1], modh[1:2]
    x = xh_ref[...]
    ms = jnp.mean(x * x, axis=-1, keepdims=True)
    hbuf[...] = ((x * lax.rsqrt(ms + EPS)) * (n1_ref[...] * (1.0 + sc1)) + sh1).astype(BF16)

    proj = jnp.dot(hbuf[...], win_ref[...], preferred_element_type=F32)
    glu = proj[:, D_POOL:D_POOL + D_CONV] * _sigmoid(proj[:, D_POOL + D_CONV:])
    pool_in[POOL_HIST:POOL_HIST + tl, :] = proj[:, :D_POOL]
    conv_in[CONV_HIST:CONV_HIST + tl, :] = glu

    modt = modt_ref[0]
    g1, sh2, sc2 = modt[2:3], modt[3:4], modt[4:5]
    for j in range(len(POOL_WINDOWS)):
        cs = slice(j * POOL_OUT_GW, (j + 1) * POOL_OUT_GW)
        cs2 = slice(D_MODEL + j * POOL_OUT_GW, D_MODEL + (j + 1) * POOL_OUT_GW)
        ga = _sigmoid(jnp.dot(h_prev[...], gw_ref[:, cs], preferred_element_type=F32) + gb_ref[:, cs])
        gb = _sigmoid(jnp.dot(h_prev[...], gw_ref[:, cs2], preferred_element_type=F32) + gb_ref[:, cs2])
        a_j = jnp.dot(pooled_prev[:, j * POOL_GW:(j + 1) * POOL_GW], pw_ref[j],
                      preferred_element_type=F32) * ps_ref[:, cs]
        b_j = jnp.dot(sy_prev[...], cwo_ref[:, cs], preferred_element_type=F32)
        merged[:, cs] = (ga * a_j + gb * b_j).astype(BF16)

    x1 = xt_ref[...] + g1 * jnp.dot(merged[...], wo_ref[...], preferred_element_type=F32)
    x1_ref[...] = x1
    ms2 = jnp.mean(x1 * x1, axis=-1, keepdims=True)
    h2 = (x1 * lax.rsqrt(ms2 + EPS)) * (n2_ref[...] * (1.0 + sc2)) + sh2
    h2b = h2.astype(BF16)
    words = _pack_words(h2)
    for j, ref in enumerate((hp0_ref, hp1_ref, hp2_ref, hp3_ref)):
        ref[...] = words[:, j * LANES:(j + 1) * LANES]

    logits = lax.dot_general(rwt_ref[...], h2b, (((1,), (1,)), ((), ())),
                             preferred_element_type=F32) + rb_ref[:, 0:1]
    e_iota = lax.broadcasted_iota(I32, (N_EXPERTS, tl), 0)
    v = logits
    ids, vals = [], []
    for _ in range(TOP_K):
        m = jnp.max(v, axis=0, keepdims=True)
        idx = jnp.min(jnp.where(v == m, e_iota, N_EXPERTS), axis=0, keepdims=True)
        ids.append(idx)
        vals.append(m)
        v = jnp.where(e_iota == idx, -jnp.inf, v)
    ex = [jnp.exp(vk - vals[0]) for vk in vals]
    den = ex[0] + ex[1] + ex[2] + ex[3]
    sel = [e_iota == idx for idx in ids]
    chosen = jnp.where((sel[0] | sel[1] | sel[2] | sel[3]) & tail_live, 1.0, 0.0)
    before = jnp.dot(chosen.astype(BF16), tri_ref[...], preferred_element_type=F32)
    rank_all = cnt_run[:, 0:1] + before
    ri = ri_ref if len(ri_ref.shape) == 2 else ri_ref.at[0]
    rw = rw_ref if len(rw_ref.shape) == 2 else rw_ref.at[0]
    for k in range(TOP_K):
        ri[k:k + 1, :] = ids[k]
        rk = jnp.sum(jnp.where(sel[k], rank_all, 0.0), axis=0, keepdims=True)
        ri[TOP_K + k:TOP_K + k + 1, :] = rk.astype(I32)
        rw[k:k + 1, :] = ex[k] / den
        rw[TOP_K + k:TOP_K + k + 1, :] = jnp.zeros((1, tl), F32)
    cnt_run[...] = cnt_run[...] + jnp.sum(chosen, axis=1, keepdims=True)
    cntout_ref[...] = cnt_run[...]

    pos = pos0 + l * tl + lax.broadcasted_iota(I32, (tl, POOL_GW), 0)
    pooled = []
    for gi, w in enumerate(POOL_WINDOWS):
        lanes = slice(gi * POOL_GW, (gi + 1) * POOL_GW)
        s = pool_in[:, lanes]
        d = 1
        while d < w:
            s = s + pltpu.roll(s, d, axis=0)
            d *= 2
        cur = pool_in[POOL_HIST:POOL_HIST + tl, lanes]
        cnt = jnp.minimum(pos + 1, w).astype(F32)
        pooled.append((s[POOL_HIST:] / cnt - cur).astype(BF16))

    n_sh = tl + CONV_HIST - SUBLANES
    for r in range(1, SUBLANES):
        conv_sh[r, 0:n_sh, :] = conv_in[r:r + n_sh, :]

    for base in range(0, tl, rc):
        acc = jnp.broadcast_to(cb_ref[...], (rc // SUBLANES, SUBLANES, D_CONV))
        for k in range(CONV_WIDTH):
            q, r = divmod(k + CONV_HIST - CONV_PAD, SUBLANES)
            start = base + q * SUBLANES
            win = conv_sh[r, start:start + rc, :].reshape(rc // SUBLANES, SUBLANES, D_CONV)
            acc = acc + win * dw_ref[k]
        ybuf[base:base + rc, :] = acc.reshape(rc, D_CONV)

    yc = ybuf[...]
    mu = jnp.mean(yc, axis=-1, keepdims=True)
    dev = yc - mu
    var = jnp.mean(dev * dev, axis=-1, keepdims=True)
    yn = dev * lax.rsqrt(var + EPS) * lg_ref[...] + lb_ref[...]

    h_prev[...] = hbuf[...]
    sy_prev[...] = (yn * _sigmoid(yn)).astype(BF16)
    for gi in range(len(POOL_WINDOWS)):
        pooled_prev[:, gi * POOL_GW:(gi + 1) * POOL_GW] = pooled[gi]

    @pl.when(head_live)
    def _():
        npool_ref[0] = pool_in[tl:tl + POOL_HIST, :]
        nconv_ref[0] = conv_in[tl:tl + CONV_HIST, :]

    conv_in[0:CONV_HIST, :] = conv_in[tl:tl + CONV_HIST, :]
    pool_in[0:POOL_HIST, :] = pool_in[tl:tl + POOL_HIST, :]


def _token_mix(x, mod, pool_init, conv_init, cnt_in, wts, *, b0, tl, pos0, rc):
    _, seq, _ = x.shape
    bsz = pool_init.shape[0]
    assert seq % tl == 0 and tl % rc == 0 and tl >= CONV_HIST
    nl = seq // tl
    n_tiles = bsz * nl
    rows_out = bsz * seq
    x2d = x.reshape(-1, D_MODEL)
    tri = jnp.triu(jnp.ones((tl, tl), BF16), k=1)
    head = lambda g: jnp.minimum(g, n_tiles - 1)
    tail = lambda g: jnp.maximum(g - 1, 0)
    full = lambda shape: pl.BlockSpec(shape, lambda g: (0,) * len(shape))
    per_b = lambda shape: pl.BlockSpec(shape, lambda g: (head(g) // nl,) + (0,) * (len(shape) - 1))
    tok_t = lambda width: pl.BlockSpec((tl, width), lambda g: (tail(g), 0))
    if tl % LANES == 0:
        rt_shape = (2 * TOP_K, n_tiles * tl)
        rt_spec = pl.BlockSpec((2 * TOP_K, tl), lambda g: (0, tail(g)))
    else:
        rt_shape = (n_tiles, 2 * TOP_K, tl)
        rt_spec = pl.BlockSpec((1, 2 * TOP_K, tl), lambda g: (tail(g), 0, 0))
    in_specs = [
        pl.BlockSpec((tl, D_MODEL), lambda g: (b0 * nl + head(g), 0)),
        pl.BlockSpec((tl, D_MODEL), lambda g: (b0 * nl + tail(g), 0)),
        pl.BlockSpec((1, 8, D_MODEL), lambda g: (b0 + head(g) // nl, 0, 0)),
        pl.BlockSpec((1, 8, D_MODEL), lambda g: (b0 + tail(g) // nl, 0, 0)),
        per_b((1, POOL_HIST, D_POOL)),
        per_b((1, CONV_HIST, D_CONV)),
        full((N_EXPERTS, LANES)),
        full((tl, tl)),
        full((1, D_MODEL)), full((1, D_MODEL)),
        full((D_MODEL, D_POOL + 2 * D_CONV)),
        full((len(POOL_WINDOWS), POOL_GW, POOL_OUT_GW)),
        full((1, D_MODEL)),
        full((CONV_WIDTH, SUBLANES, D_CONV)),
        full((1, D_CONV)), full((1, D_CONV)), full((1, D_CONV)),
        full((D_CONV, D_MODEL)),
        full((D_MODEL, 2 * D_MODEL)),
        full((1, 2 * D_MODEL)),
        full((D_MODEL, D_MODEL)),
        full((N_EXPERTS, D_MODEL)),
        full((N_EXPERTS, LANES)),
    ]
    out_shape = (
        jax.ShapeDtypeStruct((rows_out, D_MODEL), F32),
        *[jax.ShapeDtypeStruct((rows_out, LANES), I32)] * N_CHUNKS,
        jax.ShapeDtypeStruct(rt_shape, I32),
        jax.ShapeDtypeStruct(rt_shape, F32),
        jax.ShapeDtypeStruct((bsz, POOL_HIST, D_POOL), F32),
        jax.ShapeDtypeStruct((bsz, CONV_HIST, D_CONV), F32),
        jax.ShapeDtypeStruct((N_EXPERTS, LANES), F32),
    )
    out_specs = (
        tok_t(D_MODEL),
        *[tok_t(LANES)] * N_CHUNKS,
        rt_spec,
        rt_spec,
        per_b((1, POOL_HIST, D_POOL)),
        per_b((1, CONV_HIST, D_CONV)),
        full((N_EXPERTS, LANES)),
    )
    scratch = [
        pltpu.VMEM((SUBLANES, tl + CONV_HIST, D_CONV), F32),
        pltpu.VMEM((tl + POOL_HIST, D_POOL), F32),
        pltpu.VMEM((tl, D_MODEL), BF16),
        pltpu.VMEM((tl, D_CONV), F32),
        pltpu.VMEM((tl, D_MODEL), BF16),
        pltpu.VMEM((tl, D_CONV), BF16),
        pltpu.VMEM((tl, D_POOL), BF16),
        pltpu.VMEM((tl, D_MODEL), BF16),
        pltpu.VMEM((N_EXPERTS, LANES), F32),
    ]
    return pl.pallas_call(
        functools.partial(_mix_kernel, tl=tl, nl=nl, n_tiles=n_tiles, pos0=pos0, rc=rc),
        out_shape=out_shape,
        grid=(n_tiles + 1,),
        in_specs=in_specs,
        out_specs=out_specs,
        scratch_shapes=scratch,
        compiler_params=pltpu.CompilerParams(
            dimension_semantics=("arbitrary",), vmem_limit_bytes=VMEM_LIMIT_BYTES),
        name="token_mix",
    )(x2d, x2d, mod, mod, pool_init, conv_init, cnt_in, tri, *wts)


def _sc_mesh():
    return plsc.VectorSubcoreMesh(core_axis_name="c", subcore_axis_name="s",
                                  num_cores=SC_CORES, num_subcores=SC_SUBCORES)


def _for_each_chunk(n_tok, fn):
    wid = lax.axis_index("s") * SC_CORES + lax.axis_index("c")
    if n_tok % (8 * SC_WORKERS) == 0 and n_tok // SC_WORKERS >= SC_ROWS:
        per_w = n_tok // SC_WORKERS

        @pl.loop(0, pl.cdiv(per_w, SC_ROWS))
        def _(c):
            fn(pl.multiple_of(wid * per_w + jnp.minimum(c * SC_ROWS, per_w - SC_ROWS), 8))
    else:
        assert n_tok % SC_ROWS == 0 and n_tok // SC_ROWS <= SC_WORKERS

        @pl.when(wid < n_tok // SC_ROWS)
        def _():
            fn(pl.multiple_of(wid * SC_ROWS, SC_ROWS))


_SC_SCRATCH = ([pltpu.VMEM((SC_ROWS, LANES), I32)] * N_CHUNKS
               + [pltpu.VMEM((SC_ROWS,), I32)] * TOP_K
               + [pltpu.SemaphoreType.DMA, pltpu.SemaphoreType.DMA])


def _dispatch(h_groups, pos_groups, n_rows):
    n_g = len(h_groups)

    def body(*refs):
        refs = list(refs)
        h = [[refs.pop(0) for _ in range(N_CHUNKS)] for _ in range(n_g)]
        p = [[refs.pop(0) for _ in range(TOP_K)] for _ in range(n_g)]
        o = [refs.pop(0) for _ in range(N_CHUNKS)]
        rows = [refs.pop(0) for _ in range(N_CHUNKS)]
        idx = [refs.pop(0) for _ in range(TOP_K)]
        sem_in, sem_out = refs

        for hg, pg in zip(h, p):
            def move(s, hg=hg, pg=pg):
                loads = [pltpu.async_copy(hg[j].at[pl.ds(s, SC_ROWS)], rows[j], sem_in)
                         for j in range(N_CHUNKS)]
                loads += [pltpu.async_copy(pg[k].at[pl.ds(s, SC_ROWS)], idx[k], sem_in)
                          for k in range(TOP_K)]
                for cp in loads:
                    cp.wait()
                stores = [pltpu.async_copy(rows[j], o[j].at[idx[k]], sem_out)
                          for j in range(N_CHUNKS) for k in range(TOP_K)]
                for cp in stores:
                    cp.wait()

            _for_each_chunk(hg[0].shape[0], move)

    call = pl.kernel(
        body,
        out_type=tuple(jax.ShapeDtypeStruct((n_rows, LANES), I32) for _ in range(N_CHUNKS)),
        mesh=_sc_mesh(), scratch_types=_SC_SCRATCH, name="sc_dispatch")
    flat = [a for g in h_groups for a in g] + [a for g in pos_groups for a in g]
    return call(*flat)


def _combine_gather(y_chunks, pos_groups):
    n_g = len(pos_groups)

    def body(*refs):
        refs = list(refs)
        y = [refs.pop(0) for _ in range(N_CHUNKS)]
        p = [[refs.pop(0) for _ in range(TOP_K)] for _ in range(n_g)]
        g = [refs.pop(0) for _ in range(n_g)]
        rows = [refs.pop(0) for _ in range(N_CHUNKS)]
        idx = [refs.pop(0) for _ in range(TOP_K)]
        sem_in, sem_out = refs

        for pg, gg in zip(p, g):
            def move(s, pg=pg, gg=gg):
                loads = [pltpu.async_copy(pg[k].at[pl.ds(s, SC_ROWS)], idx[k], sem_in)
                         for k in range(TOP_K)]
                for cp in loads:
                    cp.wait()
                for k in range(TOP_K):
                    gathers = [pltpu.async_copy(y[j].at[idx[k]], rows[j], sem_in)
                               for j in range(N_CHUNKS)]
                    for cp in gathers:
                        cp.wait()
                    stores = [pltpu.async_copy(rows[j], gg.at[k * N_CHUNKS + j, pl.ds(s, SC_ROWS)], sem_out)
                              for j in range(N_CHUNKS)]
                    for cp in stores:
                        cp.wait()

            _for_each_chunk(pg[0].shape[0], move)

    call = pl.kernel(
        body,
        out_type=tuple(jax.ShapeDtypeStruct((TOP_K * N_CHUNKS, pg[0].shape[0], LANES), I32)
                       for pg in pos_groups),
        mesh=_sc_mesh(), scratch_types=_SC_SCRATCH, name="sc_combine_gather")
    return call(*y_chunks, *[a for g in pos_groups for a in g])


def _expert_kernel(te_ref, tr_ref, nx_ref, x0_ref, x1_ref, x2_ref, x3_ref,
                   wg_hbm, bg_ref, wu_hbm, bu_ref, wd_hbm, bd_ref,
                   y0_ref, y1_ref, y2_ref, y3_ref, stage, w_bf, sem):
    i = pl.program_id(0)

    def weight_copies(e):
        return [pltpu.make_async_copy(w.at[e], stage.at[m], sem.at[m])
                for m, w in enumerate((wg_hbm, wu_hbm, wd_hbm))]

    @pl.when(i == 0)
    def _():
        for cp in weight_copies(te_ref[0]):
            cp.start()

    @pl.when((i == 0) | (te_ref[i] != te_ref[jnp.maximum(i - 1, 0)]))
    def _():
        for cp in weight_copies(te_ref[i]):
            cp.wait()
        for m in range(3):
            w_bf[m] = stage[m].astype(BF16)

        @pl.when(nx_ref[i] >= 0)
        def _():
            for cp in weight_copies(nx_ref[i]):
                cp.start()

    @pl.when(tr_ref[i] > 0)
    def _():
        words = jnp.concatenate([x0_ref[...], x1_ref[...], x2_ref[...], x3_ref[...]], axis=1)
        hi, lo = _unpack_words(words)
        xt = jnp.concatenate([hi, lo], axis=1).astype(BF16)
        gt = jnp.minimum(jnp.dot(xt, w_bf[0], preferred_element_type=F32) + bg_ref[0], SWIGLU_LIMIT)
        up = jnp.clip(jnp.dot(xt, w_bf[1], preferred_element_type=F32) + bu_ref[0],
                      -SWIGLU_LIMIT, SWIGLU_LIMIT)
        act = gt * _sigmoid(SWIGLU_ALPHA * gt) * (up + 1.0)
        y = jnp.dot(act.astype(BF16), w_bf[2], preferred_element_type=F32) + bd_ref[0]
        out = _pack_words(y)
        for j, ref in enumerate((y0_ref, y1_ref, y2_ref, y3_ref)):
            ref[...] = out[:, j * LANES:(j + 1) * LANES]


def _expert_ffn(xs_chunks, tile_expert, tile_rows, next_expert, ew):
    n_rows = xs_chunks[0].shape[0]
    nt_max = n_rows // MOE_TILE
    wg, bg, wu, bu, wd, bd = ew
    rows = pl.BlockSpec((MOE_TILE, LANES), lambda i, te, tr, nx: (i, 0))
    mat = pl.BlockSpec(memory_space=pl.ANY)
    vec = pl.BlockSpec((1, 1, D_MODEL), lambda i, te, tr, nx: (te[i], 0, 0))
    grid_spec = pltpu.PrefetchScalarGridSpec(
        num_scalar_prefetch=3,
        grid=(nt_max,),
        in_specs=[rows] * N_CHUNKS + [mat, vec, mat, vec, mat, vec],
        out_specs=[rows] * N_CHUNKS,
        scratch_shapes=[pltpu.VMEM((3, D_MODEL, D_MODEL), F32),
                        pltpu.VMEM((3, D_MODEL, D_MODEL), BF16),
                        pltpu.SemaphoreType.DMA((3,))],
    )
    return pl.pallas_call(
        _expert_kernel,
        out_shape=[jax.ShapeDtypeStruct((n_rows, LANES), I32)] * N_CHUNKS,
        grid_spec=grid_spec,
        compiler_params=pltpu.CompilerParams(
            dimension_semantics=("arbitrary",), vmem_limit_bytes=VMEM_LIMIT_BYTES),
        name="expert_ffn",
    )(tile_expert, tile_rows, next_expert, *xs_chunks, wg, bg, wu, bu, wd, bd)


def _final_kernel(x1_ref, g_ref, w_ref, mod_ref, fg_ref, *rest):
    y_ref = rest[-1]
    wt = w_ref[...]
    acc = None
    for k in range(TOP_K):
        words = jnp.concatenate([g_ref[k * N_CHUNKS + j] for j in range(N_CHUNKS)], axis=1)
        hi, lo = _unpack_words(words)
        term = wt[:, k:k + 1] * jnp.concatenate([hi, lo], axis=1)
        acc = term if acc is None else acc + term
    g2 = mod_ref[0][5:6]
    x2 = x1_ref[...] + g2 * acc
    ms = jnp.mean(x2 * x2, axis=-1, keepdims=True)
    y_ref[...] = x2 * lax.rsqrt(ms + EPS) * fg_ref[...]


def _final(x1, gathered, wt_tok, mod, final_g, *, tl, tiles_per_batch, b0=0, out_rows=None, y_prev=None):
    n_tok = x1.shape[0]
    out_rows = n_tok if out_rows is None else out_rows
    blk0 = b0 * tiles_per_batch
    in_specs = [
        pl.BlockSpec((tl, D_MODEL), lambda i: (i, 0)),
        pl.BlockSpec((TOP_K * N_CHUNKS, tl, LANES), lambda i: (0, i, 0)),
        pl.BlockSpec((tl, 2 * TOP_K), lambda i: (i, 0)),
        pl.BlockSpec((1, 8, D_MODEL), lambda i: (b0 + i // tiles_per_batch, 0, 0)),
        pl.BlockSpec((1, D_MODEL), lambda i: (0, 0)),
    ]
    args = [x1, gathered, wt_tok, mod, final_g.reshape(1, D_MODEL)]
    aliases = {}
    if y_prev is not None:
        in_specs.append(pl.BlockSpec(memory_space=pl.ANY))
        args.append(y_prev)
        aliases = {len(args) - 1: 0}
    return pl.pallas_call(
        _final_kernel,
        out_shape=jax.ShapeDtypeStruct((out_rows, D_MODEL), F32),
        grid=(n_tok // tl,),
        in_specs=in_specs,
        out_specs=pl.BlockSpec((tl, D_MODEL), lambda i: (blk0 + i, 0)),
        input_output_aliases=aliases,
        compiler_params=pltpu.CompilerParams(
            dimension_semantics=("arbitrary",), vmem_limit_bytes=VMEM_LIMIT_BYTES),
        name="combine_final",
    )(*args)


def _route_rows(r, lo, hi):
    if r.ndim == 2:
        return r[lo:hi]
    return r[:, lo:hi, :].transpose(1, 0, 2).reshape(hi - lo, -1)


def kernel(x_prompt, x_sample, state_pool, state_conv, c_prompt, c_sample, norm1_g, norm2_g, final_g, w_ada, b_ada, w_in, pool_w, pool_scale, conv_dw, conv_b, conv_ln_g, conv_ln_b, conv_w_out, gate_w, gate_b, w_out, router_w, router_b, exp_w_gate, exp_b_gate, exp_w_up, exp_b_up, exp_w_down, exp_b_down):
    assert norm1_g.shape[0] == 1, "single-layer trunk"
    row = lambda v: v.reshape(1, -1)
    mix_w = (
        row(norm1_g[0]), row(norm2_g[0]),
        w_in[0].astype(BF16), pool_w[0].astype(BF16), row(pool_scale[0]),
        jnp.broadcast_to(conv_dw[0][:, None, :], (CONV_WIDTH, SUBLANES, D_CONV)),
        row(conv_b[0]), row(conv_ln_g[0]), row(conv_ln_b[0]),
        conv_w_out[0].astype(BF16), gate_w[0].astype(BF16), row(gate_b[0]),
        w_out[0].astype(BF16), router_w[0].T.astype(BF16),
        jnp.broadcast_to(router_b[0][:, None], (N_EXPERTS, LANES)),
    )
    ew = (
        exp_w_gate[0], exp_b_gate[0][:, None, :],
        exp_w_up[0], exp_b_up[0][:, None, :],
        exp_w_down[0], exp_b_down[0][:, None, :],
    )
    bp, lp, _ = x_prompt.shape
    bs, ls, _ = x_sample.shape
    t_p = bp * lp
    tl_p, tl_s = MIX_TILE, ls
    nb_small = max(bp // PROMPT_TAIL_FRACTION, 1)
    chunk_nb = [nb_small, bp - nb_small] if bp > nb_small else [bp]
    chunk_b0 = [sum(chunk_nb[:c]) for c in range(len(chunk_nb))]
    n_chunks = len(chunk_nb)

    mod_p = _modulation(c_prompt, w_ada[0], b_ada[0])
    mod_s = _modulation(c_sample, w_ada[0], b_ada[0])
    pad_state = lambda s, hist: jnp.pad(s, ((0, 0), (hist - s.shape[1], 0), (0, 0)))
    split = lambda out: (out[0], out[1:1 + N_CHUNKS], *out[1 + N_CHUNKS:])

    zero_cnt = jnp.zeros((N_EXPERTS, LANES), F32)
    mixed = [split(_token_mix(
        x_prompt, mod_p, jnp.zeros((nb, POOL_HIST, D_POOL), F32), jnp.zeros((nb, CONV_HIST, D_CONV), F32),
        zero_cnt, mix_w, b0=b0, tl=tl_p, pos0=0, rc=CONV_ROWS))
        for b0, nb in zip(chunk_b0, chunk_nb)]
    x1_s, hp_s, ri_s, rw_s, npool_s, nconv_s, cnt_last = split(_token_mix(
        x_sample, mod_s, pad_state(state_pool[0], POOL_HIST), pad_state(state_conv[0], CONV_HIST),
        mixed[-1][6], mix_w, b0=0, tl=tl_s, pos0=PAST_LEN, rc=CONV_ROWS))

    def finish(c, g_c, y_prev):
        x1_c, rw_c = mixed[c][0], mixed[c][3]
        return _final(x1_c, g_c, _route_rows(rw_c, 0, 2 * TOP_K).T, mod_p, final_g, tl=tl_p,
                      tiles_per_batch=lp // tl_p, b0=chunk_b0[c], out_rows=t_p, y_prev=y_prev)

    gathered = []
    for c, (x1_c, hp_c, ri_c, rw_c, _, _, cnt_c) in enumerate(mixed):
        last = c == n_chunks - 1
        n_tok = chunk_nb[c] * lp + (bs * ls if last else 0)
        n_rows = (TOP_K * n_tok // MOE_TILE + N_EXPERTS) * MOE_TILE
        counts = (cnt_last if last else cnt_c)[:, 0].astype(I32)
        tiles_e = (counts + MOE_TILE - 1) // MOE_TILE
        tile_end = jnp.cumsum(tiles_e)
        tile_start = tile_end - tiles_e
        row_off = tile_start * MOE_TILE
        e_ids = jnp.arange(N_EXPERTS, dtype=I32)
        tile_id = jnp.arange(n_rows // MOE_TILE, dtype=I32)
        e_last = jnp.max(jnp.where(tiles_e > 0, e_ids, 0))
        tile_expert = jnp.minimum(
            jnp.sum(tile_id[:, None] >= tile_end[None, :], axis=1), e_last).astype(I32)
        is_e = tile_expert[:, None] == e_ids[None, :]
        left = jnp.sum(jnp.where(is_e, (counts - (tile_id[:, None] - tile_start[None, :]) * MOE_TILE), 0), axis=1)
        tile_rows = jnp.clip(left, 0, MOE_TILE).astype(I32)
        later = (e_ids[None, :] > e_ids[:, None]) & (tiles_e[None, :] > 0)
        next_of = jnp.min(jnp.where(later, e_ids[None, :], N_EXPERTS), axis=1)
        next_of = jnp.where(next_of == N_EXPERTS, -1, next_of)
        next_expert = jnp.sum(jnp.where(is_e, next_of[None, :], 0), axis=1).astype(I32)

        def slots(ri, row_off=row_off):
            ids, ranks = _route_rows(ri, 0, TOP_K), _route_rows(ri, TOP_K, 2 * TOP_K)
            table = [row_off[e] for e in range(N_EXPERTS)]
            bit = 1
            while len(table) > 1:
                odd = (ids & bit) != 0
                table = [jnp.where(odd, table[2 * i + 1], table[2 * i]) for i in range(len(table) // 2)]
                bit *= 2
            pos = ranks + table[0]
            return [pos[k] for k in range(TOP_K)]

        h_groups = [hp_c] + ([hp_s] if last else [])
        pos_groups = [slots(ri_c)] + ([slots(ri_s)] if last else [])
        xs = _dispatch(h_groups, pos_groups, n_rows)
        ys = _expert_ffn(xs, tile_expert, tile_rows, next_expert, ew)
        gathered.append(_combine_gather(ys, pos_groups))

    y_p = None
    for c in reversed(range(n_chunks)):
        y_p = finish(c, gathered[c][0], y_p)
    y_s = _final(x1_s, gathered[-1][1], _route_rows(rw_s, 0, 2 * TOP_K).T, mod_s, final_g, tl=tl_s,
                 tiles_per_batch=1)
    unpad = lambda s, n: s[:, s.shape[1] - n:][None]
    npool_p = jnp.concatenate([m[4] for m in mixed], axis=0)
    nconv_p = jnp.concatenate([m[5] for m in mixed], axis=0)
    return (y_p.reshape(bp, lp, D_MODEL), y_s.reshape(bs, ls, D_MODEL),
            unpad(npool_p, POOL_PAD), unpad(nconv_p, CONV_PAD),
            unpad(npool_s, POOL_PAD), unpad(nconv_s, CONV_PAD))
```

```python
import functools

import jax
import jax.numpy as jnp
from jax import lax
from jax.experimental import pallas as pl
from jax.experimental.pallas import tpu as pltpu
from jax.experimental.pallas import tpu_sc as plsc

D_MODEL = 1024
D_POOL = 512
D_CONV = 512
POOL_WINDOWS = (2, 4, 8, 16)
POOL_GW = 128
POOL_OUT_GW = 256
POOL_PAD = 15
CONV_WIDTH = 31
CONV_PAD = 30
N_EXPERTS = 32
TOP_K = 4
SWIGLU_LIMIT = 7.0
SWIGLU_ALPHA = 1.702
EPS = 1e-6
PAST_LEN = 2048

POOL_HIST = 16
CONV_HIST = 32
VMEM_LIMIT_BYTES = 56 * 1024 * 1024

LANES = 128
SUBLANES = 8
D_WORDS = D_MODEL // 2
N_CHUNKS = D_WORDS // LANES
MIX_TILE = 512
CONV_ROWS = 32
MOE_TILE = 512
WEIGHT_DMA_PRIORITY = 1
PROMPT_TAIL_FRACTION = 4
SC_CORES = 2
SC_SUBCORES = 16
SC_WORKERS = SC_CORES * SC_SUBCORES
SC_ROWS = 128

F32 = jnp.float32
BF16 = jnp.bfloat16
I32 = jnp.int32
HI_MASK = -65536
NEG_LOG2E = -1.4426950408889634


def _sigmoid(v):
    return 1.0 / (1.0 + jnp.exp2(v * NEG_LOG2E))


def _pack_words(v):
    r = v.astype(BF16).astype(F32)
    hi = lax.bitcast_convert_type(r[:, :D_WORDS], I32)
    lo = lax.bitcast_convert_type(r[:, D_WORDS:], I32)
    return (hi & HI_MASK) | lax.shift_right_logical(lo, 16)


def _unpack_words(w):
    hi = lax.bitcast_convert_type(w & HI_MASK, F32)
    lo = lax.bitcast_convert_type(lax.shift_left(w, 16), F32)
    return hi, lo


def _mod_kernel(c_ref, w_ref, b_ref, o_ref):
    c = c_ref[...]
    s = (c * _sigmoid(c)).astype(BF16)
    o_ref[...] = jnp.dot(s, w_ref[...].astype(BF16), preferred_element_type=F32) + b_ref[...]


def _modulation(c, w_ada, b_ada):
    bsz = c.shape[0]
    out = pl.pallas_call(
        _mod_kernel,
        out_shape=jax.ShapeDtypeStruct((bsz, 6 * D_MODEL), F32),
        grid=(6,),
        in_specs=[
            pl.BlockSpec((bsz, D_MODEL), lambda j: (0, 0)),
            pl.BlockSpec((D_MODEL, D_MODEL), lambda j: (0, j)),
            pl.BlockSpec((1, D_MODEL), lambda j: (0, j)),
        ],
        out_specs=pl.BlockSpec((bsz, D_MODEL), lambda j: (0, j)),
        compiler_params=pltpu.CompilerParams(
            dimension_semantics=("arbitrary",), vmem_limit_bytes=VMEM_LIMIT_BYTES),
        name="adaln_mod",
    )(c, w_ada, b_ada.reshape(1, 6 * D_MODEL))
    out = out.reshape(bsz, 6, D_MODEL)
    return jnp.pad(out, ((0, 0), (0, 2), (0, 0)))


def _mix_kernel(xh_ref, xt_ref, modh_ref, modt_ref, pinit_ref, cinit_ref, cntin_ref, tri_ref,
                 n1_ref, n2_ref, win_ref, pw_ref, ps_ref, dw_ref, cb_ref, lg_ref, lb_ref,
                 cwo_ref, gw_ref, gb_ref, wo_ref, rwt_ref, rb_ref,
                 x1_ref, hp0_ref, hp1_ref, hp2_ref, hp3_ref, ri_ref, rw_ref,
                 npool_ref, nconv_ref, cntout_ref,
                 conv_sh, pool_in, hbuf, ybuf, h_prev, sy_prev, pooled_prev, merged, cnt_run,
                 *, tl, nl, n_tiles, pos0, rc):
    g = pl.program_id(0)
    gh = jnp.minimum(g, n_tiles - 1)
    l = lax.rem(gh, nl)
    head_live = g < n_tiles
    tail_live = g >= 1
    conv_in = conv_sh.at[0]

    @pl.when(g == 0)
    def _():
        cnt_run[...] = cntin_ref[...]
        h_prev[...] = jnp.zeros_like(h_prev)
        sy_prev[...] = jnp.zeros_like(sy_prev)
        pooled_prev[...] = jnp.zeros_like(pooled_prev)

    @pl.when(l == 0)
    def _():
        conv_in[0:CONV_HIST, :] = cinit_ref[0]
        pool_in[0:POOL_HIST, :] = pinit_ref[0]

    modh = modh_ref[0]
    sh1, sc1 = modh[0:1], modh[1:2]
    x = xh_ref[...]
    ms = jnp.mean(x * x, axis=-1, keepdims=True)
    hbuf[...] = ((x * lax.rsqrt(ms + EPS)) * (n1_ref[...] * (1.0 + sc1)) + sh1).astype(BF16)

    proj = jnp.dot(hbuf[...], win_ref[...], preferred_element_type=F32)
    glu = proj[:, D_POOL:D_POOL + D_CONV] * _sigmoid(proj[:, D_POOL + D_CONV:])
    pool_in[POOL_HIST:POOL_HIST + tl, :] = proj[:, :D_POOL]
    conv_in[CONV_HIST:CONV_HIST + tl, :] = glu

    modt = modt_ref[0]
    g1, sh2, sc2 = modt[2:3], modt[3:4], modt[4:5]
    for j in range(len(POOL_WINDOWS)):
        cs = slice(j * POOL_OUT_GW, (j + 1) * POOL_OUT_GW)
        cs2 = slice(D_MODEL + j * POOL_OUT_GW, D_MODEL + (j + 1) * POOL_OUT_GW)
        ga = _sigmoid(jnp.dot(h_prev[...], gw_ref[:, cs], preferred_element_type=F32) + gb_ref[:, cs])
        gb = _sigmoid(jnp.dot(h_prev[...], gw_ref[:, cs2], preferred_element_type=F32) + gb_ref[:, cs2])
        a_j = jnp.dot(pooled_prev[:, j * POOL_GW:(j + 1) * POOL_GW], pw_ref[j],
                      preferred_element_type=F32) * ps_ref[:, cs]
        b_j = jnp.dot(sy_prev[...], cwo_ref[:, cs], preferred_element_type=F32)
        merged[:, cs] = (ga * a_j + gb * b_j).astype(BF16)

    x1 = xt_ref[...] + g1 * jnp.dot(merged[...], wo_ref[...], preferred_element_type=F32)
    x1_ref[...] = x1
    ms2 = jnp.mean(x1 * x1, axis=-1, keepdims=True)
    h2 = (x1 * lax.rsqrt(ms2 + EPS)) * (n2_ref[...] * (1.0 + sc2)) + sh2
    h2b = h2.astype(BF16)
    words = _pack_words(h2)
    for j, ref in enumerate((hp0_ref, hp1_ref, hp2_ref, hp3_ref)):
        ref[...] = words[:, j * LANES:(j + 1) * LANES]

    logits = lax.dot_general(rwt_ref[...], h2b, (((1,), (1,)), ((), ())),
                             preferred_element_type=F32) + rb_ref[:, 0:1]
    e_iota = lax.broadcasted_iota(I32, (N_EXPERTS, tl), 0)
    v = logits
    ids, vals = [], []
    for _ in range(TOP_K):
        m = jnp.max(v, axis=0, keepdims=True)
        idx = jnp.min(jnp.where(v == m, e_iota, N_EXPERTS), axis=0, keepdims=True)
        ids.append(idx)
        vals.append(m)
        v = jnp.where(e_iota == idx, -jnp.inf, v)
    ex = [jnp.exp(vk - vals[0]) for vk in vals]
    den = ex[0] + ex[1] + ex[2] + ex[3]
    sel = [e_iota == idx for idx in ids]
    chosen = jnp.where((sel[0] | sel[1] | sel[2] | sel[3]) & tail_live, 1.0, 0.0)
    before = jnp.dot(chosen.astype(BF16), tri_ref[...], preferred_element_type=F32)
    rank_all = cnt_run[:, 0:1] + before
    ri = ri_ref if len(ri_ref.shape) == 2 else ri_ref.at[0]
    rw = rw_ref if len(rw_ref.shape) == 2 else rw_ref.at[0]
    for k in range(TOP_K):
        ri[k:k + 1, :] = ids[k]
        rk = jnp.sum(jnp.where(sel[k], rank_all, 0.0), axis=0, keepdims=True)
        ri[TOP_K + k:TOP_K + k + 1, :] = rk.astype(I32)
        rw[k:k + 1, :] = ex[k] / den
        rw[TOP_K + k:TOP_K + k + 1, :] = jnp.zeros((1, tl), F32)
    cnt_run[...] = cnt_run[...] + jnp.sum(chosen, axis=1, keepdims=True)
    cntout_ref[...] = cnt_run[...]

    pos = pos0 + l * tl + lax.broadcasted_iota(I32, (tl, POOL_GW), 0)
    pooled = []
    for gi, w in enumerate(POOL_WINDOWS):
        lanes = slice(gi * POOL_GW, (gi + 1) * POOL_GW)
        s = pool_in[:, lanes]
        d = 1
        while d < w:
            s = s + pltpu.roll(s, d, axis=0)
            d *= 2
        cur = pool_in[POOL_HIST:POOL_HIST + tl, lanes]
        cnt = jnp.minimum(pos + 1, w).astype(F32)
        pooled.append((s[POOL_HIST:] / cnt - cur).astype(BF16))

    n_sh = tl + CONV_HIST - SUBLANES
    for r in range(1, SUBLANES):
        conv_sh[r, 0:n_sh, :] = conv_in[r:r + n_sh, :]

    for base in range(0, tl, rc):
        acc = jnp.broadcast_to(cb_ref[...], (rc // SUBLANES, SUBLANES, D_CONV))
        for k in range(CONV_WIDTH):
            q, r = divmod(k + CONV_HIST - CONV_PAD, SUBLANES)
            start = base + q * SUBLANES
            win = conv_sh[r, start:start + rc, :].reshape(rc // SUBLANES, SUBLANES, D_CONV)
            acc = acc + win * dw_ref[k]
        ybuf[base:base + rc, :] = acc.reshape(rc, D_CONV)

    yc = ybuf[...]
    mu = jnp.mean(yc, axis=-1, keepdims=True)
    dev = yc - mu
    var = jnp.mean(dev * dev, axis=-1, keepdims=True)
    yn = dev * lax.rsqrt(var + EPS) * lg_ref[...] + lb_ref[...]

    h_prev[...] = hbuf[...]
    sy_prev[...] = (yn * _sigmoid(yn)).astype(BF16)
    for gi in range(len(POOL_WINDOWS)):
        pooled_prev[:, gi * POOL_GW:(gi + 1) * POOL_GW] = pooled[gi]

    @pl.when(head_live)
    def _():
        npool_ref[0] = pool_in[tl:tl + POOL_HIST, :]
        nconv_ref[0] = conv_in[tl:tl + CONV_HIST, :]

    conv_in[0:CONV_HIST, :] = conv_in[tl:tl + CONV_HIST, :]
    pool_in[0:POOL_HIST, :] = pool_in[tl:tl + POOL_HIST, :]


def _token_mix(x, mod, pool_init, conv_init, cnt_in, wts, *, b0, tl, pos0, rc):
    _, seq, _ = x.shape
    bsz = pool_init.shape[0]
    assert seq % tl == 0 and tl % rc == 0 and tl >= CONV_HIST
    nl = seq // tl
    n_tiles = bsz * nl
    rows_out = bsz * seq
    x2d = x.reshape(-1, D_MODEL)
    tri = jnp.triu(jnp.ones((tl, tl), BF16), k=1)
    head = lambda g: jnp.minimum(g, n_tiles - 1)
    tail = lambda g: jnp.maximum(g - 1, 0)
    full = lambda shape: pl.BlockSpec(shape, lambda g: (0,) * len(shape))
    per_b = lambda shape: pl.BlockSpec(shape, lambda g: (head(g) // nl,) + (0,) * (len(shape) - 1))
    tok_t = lambda width: pl.BlockSpec((tl, width), lambda g: (tail(g), 0))
    if tl % LANES == 0:
        rt_shape = (2 * TOP_K, n_tiles * tl)
        rt_spec = pl.BlockSpec((2 * TOP_K, tl), lambda g: (0, tail(g)))
    else:
        rt_shape = (n_tiles, 2 * TOP_K, tl)
        rt_spec = pl.BlockSpec((1, 2 * TOP_K, tl), lambda g: (tail(g), 0, 0))
    in_specs = [
        pl.BlockSpec((tl, D_MODEL), lambda g: (b0 * nl + head(g), 0)),
        pl.BlockSpec((tl, D_MODEL), lambda g: (b0 * nl + tail(g), 0)),
        pl.BlockSpec((1, 8, D_MODEL), lambda g: (b0 + head(g) // nl, 0, 0)),
        pl.BlockSpec((1, 8, D_MODEL), lambda g: (b0 + tail(g) // nl, 0, 0)),
        per_b((1, POOL_HIST, D_POOL)),
        per_b((1, CONV_HIST, D_CONV)),
        full((N_EXPERTS, LANES)),
        full((tl, tl)),
        full((1, D_MODEL)), full((1, D_MODEL)),
        full((D_MODEL, D_POOL + 2 * D_CONV)),
        full((len(POOL_WINDOWS), POOL_GW, POOL_OUT_GW)),
        full((1, D_MODEL)),
        full((CONV_WIDTH, SUBLANES, D_CONV)),
        full((1, D_CONV)), full((1, D_CONV)), full((1, D_CONV)),
        full((D_CONV, D_MODEL)),
        full((D_MODEL, 2 * D_MODEL)),
        full((1, 2 * D_MODEL)),
        full((D_MODEL, D_MODEL)),
        full((N_EXPERTS, D_MODEL)),
        full((N_EXPERTS, LANES)),
    ]
    out_shape = (
        jax.ShapeDtypeStruct((rows_out, D_MODEL), F32),
        *[jax.ShapeDtypeStruct((rows_out, LANES), I32)] * N_CHUNKS,
        jax.ShapeDtypeStruct(rt_shape, I32),
        jax.ShapeDtypeStruct(rt_shape, F32),
        jax.ShapeDtypeStruct((bsz, POOL_HIST, D_POOL), F32),
        jax.ShapeDtypeStruct((bsz, CONV_HIST, D_CONV), F32),
        jax.ShapeDtypeStruct((N_EXPERTS, LANES), F32),
    )
    out_specs = (
        tok_t(D_MODEL),
        *[tok_t(LANES)] * N_CHUNKS,
        rt_spec,
        rt_spec,
        per_b((1, POOL_HIST, D_POOL)),
        per_b((1, CONV_HIST, D_CONV)),
        full((N_EXPERTS, LANES)),
    )
    scratch = [
        pltpu.VMEM((SUBLANES, tl + CONV_HIST, D_CONV), F32),
        pltpu.VMEM((tl + POOL_HIST, D_POOL), F32),
        pltpu.VMEM((tl, D_MODEL), BF16),
        pltpu.VMEM((tl, D_CONV), F32),
        pltpu.VMEM((tl, D_MODEL), BF16),
        pltpu.VMEM((tl, D_CONV), BF16),
        pltpu.VMEM((tl, D_POOL), BF16),
        pltpu.VMEM((tl, D_MODEL), BF16),
        pltpu.VMEM((N_EXPERTS, LANES), F32),
    ]
    return pl.pallas_call(
        functools.partial(_mix_kernel, tl=tl, nl=nl, n_tiles=n_tiles, pos0=pos0, rc=rc),
        out_shape=out_shape,
        grid=(n_tiles + 1,),
        in_specs=in_specs,
        out_specs=out_specs,
        scratch_shapes=scratch,
        compiler_params=pltpu.CompilerParams(
            dimension_semantics=("arbitrary",), vmem_limit_bytes=VMEM_LIMIT_BYTES),
        name="token_mix",
    )(x2d, x2d, mod, mod, pool_init, conv_init, cnt_in, tri, *wts)


def _sc_mesh():
    return plsc.VectorSubcoreMesh(core_axis_name="c", subcore_axis_name="s",
                                  num_cores=SC_CORES, num_subcores=SC_SUBCORES)


def _for_each_chunk(n_tok, fn):
    wid = lax.axis_index("s") * SC_CORES + lax.axis_index("c")
    if n_tok % (8 * SC_WORKERS) == 0 and n_tok // SC_WORKERS >= SC_ROWS:
        per_w = n_tok // SC_WORKERS

        @pl.loop(0, pl.cdiv(per_w, SC_ROWS))
        def _(c):
            fn(pl.multiple_of(wid * per_w + jnp.minimum(c * SC_ROWS, per_w - SC_ROWS), 8))
    else:
        assert n_tok % SC_ROWS == 0 and n_tok // SC_ROWS <= SC_WORKERS

        @pl.when(wid < n_tok // SC_ROWS)
        def _():
            fn(pl.multiple_of(wid * SC_ROWS, SC_ROWS))


_SC_SCRATCH = ([pltpu.VMEM((SC_ROWS, LANES), I32)] * N_CHUNKS
               + [pltpu.VMEM((SC_ROWS,), I32)] * TOP_K
               + [pltpu.SemaphoreType.DMA, pltpu.SemaphoreType.DMA])


def _dispatch(h_groups, pos_groups, n_rows):
    n_g = len(h_groups)

    def body(*refs):
        refs = list(refs)
        h = [[refs.pop(0) for _ in range(N_CHUNKS)] for _ in range(n_g)]
        p = [[refs.pop(0) for _ in range(TOP_K)] for _ in range(n_g)]
        o = [refs.pop(0) for _ in range(N_CHUNKS)]
        rows = [refs.pop(0) for _ in range(N_CHUNKS)]
        idx = [refs.pop(0) for _ in range(TOP_K)]
        sem_in, sem_out = refs

        for hg, pg in zip(h, p):
            def move(s, hg=hg, pg=pg):
                loads = [pltpu.async_copy(hg[j].at[pl.ds(s, SC_ROWS)], rows[j], sem_in)
                         for j in range(N_CHUNKS)]
                loads += [pltpu.async_copy(pg[k].at[pl.ds(s, SC_ROWS)], idx[k], sem_in)
                          for k in range(TOP_K)]
                for cp in loads:
                    cp.wait()
                stores = [pltpu.async_copy(rows[j], o[j].at[idx[k]], sem_out)
                          for j in range(N_CHUNKS) for k in range(TOP_K)]
                for cp in stores:
                    cp.wait()

            _for_each_chunk(hg[0].shape[0], move)

    call = pl.kernel(
        body,
        out_type=tuple(jax.ShapeDtypeStruct((n_rows, LANES), I32) for _ in range(N_CHUNKS)),
        mesh=_sc_mesh(), scratch_types=_SC_SCRATCH, name="sc_dispatch")
    flat = [a for g in h_groups for a in g] + [a for g in pos_groups for a in g]
    return call(*flat)


def _combine_gather(y_chunks, pos_groups):
    n_g = len(pos_groups)

    def body(*refs):
        refs = list(refs)
        y = [refs.pop(0) for _ in range(N_CHUNKS)]
        p = [[refs.pop(0) for _ in range(TOP_K)] for _ in range(n_g)]
        g = [refs.pop(0) for _ in range(n_g)]
        rows = [refs.pop(0) for _ in range(N_CHUNKS)]
        idx = [refs.pop(0) for _ in range(TOP_K)]
        sem_in, sem_out = refs

        for pg, gg in zip(p, g):
            def move(s, pg=pg, gg=gg):
                loads = [pltpu.async_copy(pg[k].at[pl.ds(s, SC_ROWS)], idx[k], sem_in)
                         for k in range(TOP_K)]
                for cp in loads:
                    cp.wait()
                for k in range(TOP_K):
                    gathers = [pltpu.async_copy(y[j].at[idx[k]], rows[j], sem_in)
                               for j in range(N_CHUNKS)]
                    for cp in gathers:
                        cp.wait()
                    stores = [pltpu.async_copy(rows[j], gg.at[k * N_CHUNKS + j, pl.ds(s, SC_ROWS)], sem_out)
                              for j in range(N_CHUNKS)]
                    for cp in stores:
                        cp.wait()

            _for_each_chunk(pg[0].shape[0], move)

    call = pl.kernel(
        body,
        out_type=tuple(jax.ShapeDtypeStruct((TOP_K * N_CHUNKS, pg[0].shape[0], LANES), I32)
                       for pg in pos_groups),
        mesh=_sc_mesh(), scratch_types=_SC_SCRATCH, name="sc_combine_gather")
    return call(*y_chunks, *[a for g in pos_groups for a in g])


def _expert_kernel(te_ref, tr_ref, nx_ref, x0_ref, x1_ref, x2_ref, x3_ref,
                   wg_hbm, bg_ref, wu_hbm, bu_ref, wd_hbm, bd_ref,
                   y0_ref, y1_ref, y2_ref, y3_ref, stage, w_bf, sem):
    i = pl.program_id(0)

    def weight_copies(e):
        return [pltpu.make_async_copy(w.at[e], stage.at[m], sem.at[m])
                for m, w in enumerate((wg_hbm, wu_hbm, wd_hbm))]

    @pl.when(i == 0)
    def _():
        for cp in weight_copies(te_ref[0]):
            cp.start(priority=WEIGHT_DMA_PRIORITY)

    @pl.when((i == 0) | (te_ref[i] != te_ref[jnp.maximum(i - 1, 0)]))
    def _():
        for cp in weight_copies(te_ref[i]):
            cp.wait()
        for m in range(3):
            w_bf[m] = stage[m].astype(BF16)

        @pl.when(nx_ref[i] >= 0)
        def _():
            for cp in weight_copies(nx_ref[i]):
                cp.start(priority=WEIGHT_DMA_PRIORITY)

    @pl.when(tr_ref[i] > 0)
    def _():
        words = jnp.concatenate([x0_ref[...], x1_ref[...], x2_ref[...], x3_ref[...]], axis=1)
        hi, lo = _unpack_words(words)
        xt = jnp.concatenate([hi, lo], axis=1).astype(BF16)
        gt = jnp.minimum(jnp.dot(xt, w_bf[0], preferred_element_type=F32) + bg_ref[0], SWIGLU_LIMIT)
        up = jnp.clip(jnp.dot(xt, w_bf[1], preferred_element_type=F32) + bu_ref[0],
                      -SWIGLU_LIMIT, SWIGLU_LIMIT)
        act = gt * _sigmoid(SWIGLU_ALPHA * gt) * (up + 1.0)
        y = jnp.dot(act.astype(BF16), w_bf[2], preferred_element_type=F32) + bd_ref[0]
        out = _pack_words(y)
        for j, ref in enumerate((y0_ref, y1_ref, y2_ref, y3_ref)):
            ref[...] = out[:, j * LANES:(j + 1) * LANES]


def _expert_ffn(xs_chunks, tile_expert, tile_rows, next_expert, ew):
    n_rows = xs_chunks[0].shape[0]
    nt_max = n_rows // MOE_TILE
    wg, bg, wu, bu, wd, bd = ew
    rows = pl.BlockSpec((MOE_TILE, LANES), lambda i, te, tr, nx: (i, 0))
    mat = pl.BlockSpec(memory_space=pl.ANY)
    vec = pl.BlockSpec((1, 1, D_MODEL), lambda i, te, tr, nx: (te[i], 0, 0))
    grid_spec = pltpu.PrefetchScalarGridSpec(
        num_scalar_prefetch=3,
        grid=(nt_max,),
        in_specs=[rows] * N_CHUNKS + [mat, vec, mat, vec, mat, vec],
        out_specs=[rows] * N_CHUNKS,
        scratch_shapes=[pltpu.VMEM((3, D_MODEL, D_MODEL), F32),
                        pltpu.VMEM((3, D_MODEL, D_MODEL), BF16),
                        pltpu.SemaphoreType.DMA((3,))],
    )
    return pl.pallas_call(
        _expert_kernel,
        out_shape=[jax.ShapeDtypeStruct((n_rows, LANES), I32)] * N_CHUNKS,
        grid_spec=grid_spec,
        compiler_params=pltpu.CompilerParams(
            dimension_semantics=("arbitrary",), vmem_limit_bytes=VMEM_LIMIT_BYTES),
        name="expert_ffn",
    )(tile_expert, tile_rows, next_expert, *xs_chunks, wg, bg, wu, bu, wd, bd)


def _final_kernel(x1_ref, g_ref, w_ref, mod_ref, fg_ref, *rest):
    y_ref = rest[-1]
    wt = w_ref[...]
    acc = None
    for k in range(TOP_K):
        words = jnp.concatenate([g_ref[k * N_CHUNKS + j] for j in range(N_CHUNKS)], axis=1)
        hi, lo = _unpack_words(words)
        term = wt[:, k:k + 1] * jnp.concatenate([hi, lo], axis=1)
        acc = term if acc is None else acc + term
    g2 = mod_ref[0][5:6]
    x2 = x1_ref[...] + g2 * acc
    ms = jnp.mean(x2 * x2, axis=-1, keepdims=True)
    y_ref[...] = x2 * lax.rsqrt(ms + EPS) * fg_ref[...]


def _final(x1, gathered, wt_tok, mod, final_g, *, tl, tiles_per_batch, b0=0, out_rows=None, y_prev=None):
    n_tok = x1.shape[0]
    out_rows = n_tok if out_rows is None else out_rows
    blk0 = b0 * tiles_per_batch
    in_specs = [
        pl.BlockSpec((tl, D_MODEL), lambda i: (i, 0)),
        pl.BlockSpec((TOP_K * N_CHUNKS, tl, LANES), lambda i: (0, i, 0)),
        pl.BlockSpec((tl, 2 * TOP_K), lambda i: (i, 0)),
        pl.BlockSpec((1, 8, D_MODEL), lambda i: (b0 + i // tiles_per_batch, 0, 0)),
        pl.BlockSpec((1, D_MODEL), lambda i: (0, 0)),
    ]
    args = [x1, gathered, wt_tok, mod, final_g.reshape(1, D_MODEL)]
    aliases = {}
    if y_prev is not None:
        in_specs.append(pl.BlockSpec(memory_space=pl.ANY))
        args.append(y_prev)
        aliases = {len(args) - 1: 0}
    return pl.pallas_call(
        _final_kernel,
        out_shape=jax.ShapeDtypeStruct((out_rows, D_MODEL), F32),
        grid=(n_tok // tl,),
        in_specs=in_specs,
        out_specs=pl.BlockSpec((tl, D_MODEL), lambda i: (blk0 + i, 0)),
        input_output_aliases=aliases,
        compiler_params=pltpu.CompilerParams(
            dimension_semantics=("arbitrary",), vmem_limit_bytes=VMEM_LIMIT_BYTES),
        name="combine_final",
    )(*args)


def _route_rows(r, lo, hi):
    if r.ndim == 2:
        return r[lo:hi]
    return r[:, lo:hi, :].transpose(1, 0, 2).reshape(hi - lo, -1)


def kernel(x_prompt, x_sample, state_pool, state_conv, c_prompt, c_sample, norm1_g, norm2_g, final_g, w_ada, b_ada, w_in, pool_w, pool_scale, conv_dw, conv_b, conv_ln_g, conv_ln_b, conv_w_out, gate_w, gate_b, w_out, router_w, router_b, exp_w_gate, exp_b_gate, exp_w_up, exp_b_up, exp_w_down, exp_b_down):
    assert norm1_g.shape[0] == 1, "single-layer trunk"
    row = lambda v: v.reshape(1, -1)
    mix_w = (
        row(norm1_g[0]), row(norm2_g[0]),
        w_in[0].astype(BF16), pool_w[0].astype(BF16), row(pool_scale[0]),
        jnp.broadcast_to(conv_dw[0][:, None, :], (CONV_WIDTH, SUBLANES, D_CONV)),
        row(conv_b[0]), row(conv_ln_g[0]), row(conv_ln_b[0]),
        conv_w_out[0].astype(BF16), gate_w[0].astype(BF16), row(gate_b[0]),
        w_out[0].astype(BF16), router_w[0].T.astype(BF16),
        jnp.broadcast_to(router_b[0][:, None], (N_EXPERTS, LANES)),
    )
    ew = (
        exp_w_gate[0], exp_b_gate[0][:, None, :],
        exp_w_up[0], exp_b_up[0][:, None, :],
        exp_w_down[0], exp_b_down[0][:, None, :],
    )
    bp, lp, _ = x_prompt.shape
    bs, ls, _ = x_sample.shape
    t_p = bp * lp
    tl_p, tl_s = MIX_TILE, ls
    nb_small = max(bp // PROMPT_TAIL_FRACTION, 1)
    chunk_nb = [nb_small, bp - nb_small] if bp > nb_small else [bp]
    chunk_b0 = [sum(chunk_nb[:c]) for c in range(len(chunk_nb))]
    n_chunks = len(chunk_nb)

    mod_p = _modulation(c_prompt, w_ada[0], b_ada[0])
    mod_s = _modulation(c_sample, w_ada[0], b_ada[0])
    pad_state = lambda s, hist: jnp.pad(s, ((0, 0), (hist - s.shape[1], 0), (0, 0)))
    split = lambda out: (out[0], out[1:1 + N_CHUNKS], *out[1 + N_CHUNKS:])

    zero_cnt = jnp.zeros((N_EXPERTS, LANES), F32)
    mixed = [split(_token_mix(
        x_prompt, mod_p, jnp.zeros((nb, POOL_HIST, D_POOL), F32), jnp.zeros((nb, CONV_HIST, D_CONV), F32),
        zero_cnt, mix_w, b0=b0, tl=tl_p, pos0=0, rc=CONV_ROWS))
        for b0, nb in zip(chunk_b0, chunk_nb)]
    x1_s, hp_s, ri_s, rw_s, npool_s, nconv_s, cnt_last = split(_token_mix(
        x_sample, mod_s, pad_state(state_pool[0], POOL_HIST), pad_state(state_conv[0], CONV_HIST),
        mixed[-1][6], mix_w, b0=0, tl=tl_s, pos0=PAST_LEN, rc=CONV_ROWS))

    def finish(c, g_c, y_prev):
        x1_c, rw_c = mixed[c][0], mixed[c][3]
        return _final(x1_c, g_c, _route_rows(rw_c, 0, 2 * TOP_K).T, mod_p, final_g, tl=tl_p,
                      tiles_per_batch=lp // tl_p, b0=chunk_b0[c], out_rows=t_p, y_prev=y_prev)

    gathered = []
    for c, (x1_c, hp_c, ri_c, rw_c, _, _, cnt_c) in enumerate(mixed):
        last = c == n_chunks - 1
        n_tok = chunk_nb[c] * lp + (bs * ls if last else 0)
        n_rows = (TOP_K * n_tok // MOE_TILE + N_EXPERTS) * MOE_TILE
        counts = (cnt_last if last else cnt_c)[:, 0].astype(I32)
        tiles_e = (counts + MOE_TILE - 1) // MOE_TILE
        tile_end = jnp.cumsum(tiles_e)
        tile_start = tile_end - tiles_e
        row_off = tile_start * MOE_TILE
        e_ids = jnp.arange(N_EXPERTS, dtype=I32)
        tile_id = jnp.arange(n_rows // MOE_TILE, dtype=I32)
        e_last = jnp.max(jnp.where(tiles_e > 0, e_ids, 0))
        tile_expert = jnp.minimum(
            jnp.sum(tile_id[:, None] >= tile_end[None, :], axis=1), e_last).astype(I32)
        is_e = tile_expert[:, None] == e_ids[None, :]
        left = jnp.sum(jnp.where(is_e, (counts - (tile_id[:, None] - tile_start[None, :]) * MOE_TILE), 0), axis=1)
        tile_rows = jnp.clip(left, 0, MOE_TILE).astype(I32)
        later = (e_ids[None, :] > e_ids[:, None]) & (tiles_e[None, :] > 0)
        next_of = jnp.min(jnp.where(later, e_ids[None, :], N_EXPERTS), axis=1)
        next_of = jnp.where(next_of == N_EXPERTS, -1, next_of)
        next_expert = jnp.sum(jnp.where(is_e, next_of[None, :], 0), axis=1).astype(I32)

        def slots(ri, row_off=row_off):
            ids, ranks = _route_rows(ri, 0, TOP_K), _route_rows(ri, TOP_K, 2 * TOP_K)
            table = [row_off[e] for e in range(N_EXPERTS)]
            bit = 1
            while len(table) > 1:
                odd = (ids & bit) != 0
                table = [jnp.where(odd, table[2 * i + 1], table[2 * i]) for i in range(len(table) // 2)]
                bit *= 2
            pos = ranks + table[0]
            return [pos[k] for k in range(TOP_K)]

        h_groups = [hp_c] + ([hp_s] if last else [])
        pos_groups = [slots(ri_c)] + ([slots(ri_s)] if last else [])
        xs = _dispatch(h_groups, pos_groups, n_rows)
        ys = _expert_ffn(xs, tile_expert, tile_rows, next_expert, ew)
        gathered.append(_combine_gather(ys, pos_groups))

    y_p = None
    for c in reversed(range(n_chunks)):
        y_p = finish(c, gathered[c][0], y_p)
    y_s = _final(x1_s, gathered[-1][1], _route_rows(rw_s, 0, 2 * TOP_K).T, mod_s, final_g, tl=tl_s,
                 tiles_per_batch=1)
    unpad = lambda s, n: s[:, s.shape[1] - n:][None]
    npool_p = jnp.concatenate([m[4] for m in mixed], axis=0)
    nconv_p = jnp.concatenate([m[5] for m in mixed], axis=0)
    return (y_p.reshape(bp, lp, D_MODEL), y_s.reshape(bs, ls, D_MODEL),
            unpad(npool_p, POOL_PAD), unpad(nconv_p, CONV_PAD),
            unpad(npool_s, POOL_PAD), unpad(nconv_s, CONV_PAD))
```
